```python
import math
import jax, jax.numpy as jnp
from jax import lax
import numpy as np

D_MODEL = 1024
BATCH = 4
SEQ = 4096
DEPTH = 4
DEC_BATCH = 128
DEC_SEQ = 8
PAST_LEN = 2048
PAGE_SIZE = 128

HEAD_DIM = 64
A_HEADS = 8
A_KV = 4
A_GROUP = A_HEADS // A_KV
A_WIDTH = A_HEADS * HEAD_DIM
IDX_HEADS = 8
IDX_DIM = 64
IDX_W_SCALE = (IDX_HEADS ** -0.5) * (IDX_DIM ** -0.5)
DSA_TOPK = 256
B_HEADS = 8
B_KV = 4
B_GROUP = B_HEADS // B_KV
B_WIDTH = B_HEADS * HEAD_DIM
MOBA_BLOCK = 256
MOBA_TOPK = 3
C_HEADS = 4
C_HD = 128
C_WIDTH = C_HEADS * C_HD
C_CHUNK = 64
FORGET_BIAS = 3.0
N_BRANCH = 3
PEER_HEADS = 8
PEER_NKEYS = 128
PEER_EXPERTS = PEER_NKEYS * PEER_NKEYS
PEER_QDIM = 256
PEER_TOPK = 16
PLE_DIM = 256
ROPE_THETA = 10000.0
EPS = 1e-6
Q_BLOCK = 128
MOBA_Q_BLOCK = 32
TOKEN_BLOCK = 256

IN_SIZES = (A_WIDTH, A_KV * HEAD_DIM, A_KV * HEAD_DIM, IDX_HEADS * IDX_DIM, IDX_HEADS, IDX_DIM,
            B_WIDTH, B_KV * HEAD_DIM, B_KV * HEAD_DIM,
            C_WIDTH, C_WIDTH, C_WIDTH, C_WIDTH, C_HEADS, C_HEADS,
            N_BRANCH * D_MODEL)
IN_TOTAL = sum(IN_SIZES)
IN_SPLITS = tuple(int(s) for s in np.cumsum(IN_SIZES)[:-1])

kernel_name = "hybrid_dsa_moba_mlstm_peer_step"

F32 = jnp.float32


def _rms(x, g):
    xf = x.astype(F32)
    y = xf * lax.rsqrt(jnp.mean(xf * xf, axis=-1, keepdims=True) + EPS)
    return (y * g.astype(F32)).astype(x.dtype)


def _rope(x, pos):
    half = x.shape[-1] // 2
    freqs = jnp.power(jnp.float32(ROPE_THETA), -jnp.arange(half, dtype=F32) / half)
    ang = pos.astype(F32)[:, None] * freqs[None, :]
    cos = jnp.cos(ang)[None, :, None, :]
    sin = jnp.sin(ang)[None, :, None, :]
    xf = x.astype(F32)
    x1, x2 = xf[..., :half], xf[..., half:]
    return jnp.concatenate([x1 * cos - x2 * sin, x1 * sin + x2 * cos], axis=-1).astype(x.dtype)


def _to_blocks(a, nb):
    return jnp.moveaxis(a.reshape(a.shape[0], nb, a.shape[1] // nb, *a.shape[2:]), 1, 0)


def _from_blocks(a):
    a = jnp.moveaxis(a, 0, 1)
    return a.reshape(a.shape[0], a.shape[1] * a.shape[2], *a.shape[3:])


def _dsa_attend(q, qi, wi, k_all, v_all, ki_all, q_pos):
    n_q = q.shape[1]
    n_k = k_all.shape[1]
    top = min(DSA_TOPK, n_k // 4)
    blk = Q_BLOCK if n_q % Q_BLOCK == 0 else n_q
    nb = n_q // blk
    k_pos = jnp.arange(n_k, dtype=jnp.int32)
    scale = HEAD_DIM ** -0.5
    ki_f = ki_all.astype(F32)

    def block(args):
        qb, qib, wib, pb = args
        s = jax.nn.relu(jnp.einsum('bqhd,bld->bqhl', qib.astype(F32), ki_f))
        score = jnp.einsum('bqhl,bqh->bql', s, wib.astype(F32))
        score = jnp.where(k_pos[None, None, :] <= pb[None, :, None], score, -jnp.inf)
        _, sel = lax.top_k(score, top)
        valid = sel <= pb[None, :, None]
        kg = jax.vmap(lambda kk, ii: kk[ii])(k_all, sel)
        vg = jax.vmap(lambda vv, ii: vv[ii])(v_all, sel)
        logits = jnp.einsum('bqkgd,bqskd->bqkgs', qb.astype(F32), kg.astype(F32)) * scale
        logits = jnp.where(valid[:, :, None, None, :], logits, -jnp.inf)
        p = jax.nn.softmax(logits, axis=-1)
        return jnp.einsum('bqkgs,bqskd->bqkgd', p, vg.astype(F32)).astype(q.dtype)

    out = lax.map(block, (_to_blocks(q, nb), _to_blocks(qi, nb), _to_blocks(wi, nb), q_pos.reshape(nb, blk)))
    return _from_blocks(out)


def _moba_attend(q, k_all, v_all, q_pos):
    n_b, n_q = q.shape[:2]
    n_k = k_all.shape[1]
    n_blk = -(-n_k // MOBA_BLOCK)
    pad = n_blk * MOBA_BLOCK - n_k
    kb = jnp.pad(k_all, ((0, 0), (0, pad), (0, 0), (0, 0))).reshape(n_b, n_blk, MOBA_BLOCK, B_KV, HEAD_DIM)
    vb = jnp.pad(v_all, ((0, 0), (0, pad), (0, 0), (0, 0))).reshape(n_b, n_blk, MOBA_BLOCK, B_KV, HEAD_DIM)
    k_mean = jnp.mean(kb.astype(F32), axis=2)
    kb_h = jnp.moveaxis(kb, 3, 1)
    vb_h = jnp.moveaxis(vb, 3, 1)
    top = min(MOBA_TOPK, n_blk)
    blk = MOBA_Q_BLOCK if n_q % MOBA_Q_BLOCK == 0 else n_q
    nb = n_q // blk
    blk_ids = jnp.arange(n_blk, dtype=jnp.int32)
    in_blk = jnp.arange(MOBA_BLOCK, dtype=jnp.int32)
    b_ix = jnp.arange(n_b)
    kv_ix = jnp.arange(B_KV)
    scale = HEAD_DIM ** -0.5

    def block(args):
        qb, pb = args
        qf = qb.astype(F32)
        own = pb // MOBA_BLOCK
        gate = jnp.einsum('bqkgd,bnkd->bqkgn', qf, k_mean)
        past = blk_ids[None, :] < own[:, None]
        gate = jnp.where(past[None, :, None, None, :], gate, -jnp.inf)
        _, sel = lax.top_k(gate, top)
        sel_ok = sel < own[None, :, None, None, None]
        bi = b_ix[:, None, None, None, None]
        ki = kv_ix[None, None, :, None, None]
        kg = kb_h[bi, ki, sel]
        vg = vb_h[bi, ki, sel]
        l_sel = jnp.einsum('bqkgd,bqkgnsd->bqkgns', qf, kg.astype(F32)) * scale
        l_sel = jnp.where(sel_ok[..., None], l_sel, -jnp.inf).reshape(n_b, blk, B_KV, B_GROUP, top * MOBA_BLOCK)
        ko = kb[b_ix[:, None], own[None, :]]
        vo = vb[b_ix[:, None], own[None, :]]
        l_own = jnp.einsum('bqkgd,bqskd->bqkgs', qf, ko.astype(F32)) * scale
        own_pos = own[:, None] * MOBA_BLOCK + in_blk[None, :]
        l_own = jnp.where((own_pos <= pb[:, None])[None, :, None, None, :], l_own, -jnp.inf)
        p = jax.nn.softmax(jnp.concatenate([l_sel, l_own], axis=-1), axis=-1)
        p_sel = p[..., :top * MOBA_BLOCK].reshape(n_b, blk, B_KV, B_GROUP, top, MOBA_BLOCK)
        p_own = p[..., top * MOBA_BLOCK:]
        out = (jnp.einsum('bqkgns,bqkgnsd->bqkgd', p_sel, vg.astype(F32))
               + jnp.einsum('bqkgs,bqskd->bqkgd', p_own, vo.astype(F32)))
        return out.astype(q.dtype)

    out = lax.map(block, (_to_blocks(q, nb), q_pos.reshape(nb, blk)))
    return _from_blocks(out)


def _mlstm(q, k, v, i_pre, f_pre, C0, n0, m0):
    n_s = q.shape[1]
    L = C_CHUNK if n_s % C_CHUNK == 0 else n_s
    nc = n_s // L
    q = q.astype(F32)
    k = k.astype(F32) * (C_HD ** -0.5)
    v = v.astype(F32)
    i_pre = i_pre.astype(F32)
    logf = jax.nn.log_sigmoid(f_pre.astype(F32))
    tri = jnp.tril(jnp.ones((L, L), dtype=bool))

    def step(carry, xs):
        C, n, m = carry
        qc, kc, vc, ic, fc = xs
        b = jnp.moveaxis(jnp.cumsum(fc, axis=1), 1, 2)
        it = jnp.moveaxis(ic, 1, 2)
        dmat = jnp.where(tri, b[..., :, None] - b[..., None, :] + it[..., None, :], -jnp.inf)
        m_t = jnp.maximum(m[..., None] + b, jnp.max(dmat, axis=-1))
        inter = jnp.exp(m[..., None] + b - m_t)
        s = jnp.einsum('blhd,bshd->bhls', qc, kc) * jnp.exp(dmat - m_t[..., None])
        num = jnp.einsum('bhls,bshd->bhld', s, vc) + inter[..., None] * jnp.einsum('blhd,bhde->bhle', qc, C)
        den = jnp.sum(s, axis=-1) + inter * jnp.einsum('blhd,bhd->bhl', qc, n)
        h = num / jnp.maximum(jnp.abs(den), jnp.exp(-m_t))[..., None]
        m_new = m_t[..., -1]
        decay = jnp.exp(m + b[..., -1] - m_new)
        wk = jnp.exp(b[..., -1:] - b + it - m_new[..., None])
        C_new = decay[..., None, None] * C + jnp.einsum('bhs,bshd,bshe->bhde', wk, kc, vc)
        n_new = decay[..., None] * n + jnp.einsum('bhs,bshd->bhd', wk, kc)
        return (C_new, n_new, m_new), jnp.moveaxis(h, 1, 2)

    xs = (_to_blocks(q, nc), _to_blocks(k, nc), _to_blocks(v, nc), _to_blocks(i_pre, nc), _to_blocks(logf, nc))
    (C, n, m), h = lax.scan(step, (C0.astype(F32), n0.astype(F32), m0.astype(F32)), xs)
    return _from_blocks(h), C, n, m


def _peer(h, w_q, subkeys, u_tab, v_tab):
    n_b, n_s, d = h.shape
    hf = h.reshape(n_b * n_s, d)
    n_tok = hf.shape[0]
    blk = min(TOKEN_BLOCK, n_tok)
    nb = -(-n_tok // blk)
    hp = jnp.pad(hf, ((0, nb * blk - n_tok), (0, 0))).reshape(nb, blk, d)
    half = PEER_QDIM // 2
    sk = subkeys.astype(F32)

    def block(hb):
        q = (hb @ w_q).reshape(blk, PEER_HEADS, 2, half).astype(F32)
        s = jnp.einsum('nhpd,hpkd->nhpk', q, sk)
        sv, si = lax.top_k(s, PEER_TOPK)
        cand = (sv[:, :, 0, :, None] + sv[:, :, 1, None, :]).reshape(blk, PEER_HEADS, PEER_TOPK * PEER_TOPK)
        cid = (si[:, :, 0, :, None] * PEER_NKEYS + si[:, :, 1, None, :]).reshape(blk, PEER_HEADS, PEER_TOPK * PEER_TOPK)
        tv, ti = lax.top_k(cand, PEER_TOPK)
        eid = jnp.take_along_axis(cid, ti, axis=-1)
        g = jax.nn.softmax(tv, axis=-1)
        u = u_tab[eid].astype(F32)
        a = jax.nn.gelu(jnp.einsum('nd,nhkd->nhk', hb.astype(F32), u))
        out = jnp.einsum('nhk,nhkd->nd', g * a, v_tab[eid].astype(F32))
        return out.astype(h.dtype)

    out = lax.map(block, hp).reshape(nb * blk, d)[:n_tok]
    return out.reshape(n_b, n_s, d)


def _layer(x, p_l, pos, lw, past):
    n_b, n_s, _ = x.shape
    h = _rms(x, lw['g_mix'])
    z = h @ lw['w_in']
    (aq, ak, av, aiq, aiw, aik, bq, bk, bv, cq, ck, cv, co, ci, cf, gate) = jnp.split(z, IN_SPLITS, axis=-1)
    aq = _rope(_rms(aq.reshape(n_b, n_s, A_HEADS, HEAD_DIM), lw['g_qa']), pos).reshape(n_b, n_s, A_KV, A_GROUP, HEAD_DIM)
    ak = _rope(_rms(ak.reshape(n_b, n_s, A_KV, HEAD_DIM), lw['g_ka']), pos)
    av = av.reshape(n_b, n_s, A_KV, HEAD_DIM)
    aiq = _rope(aiq.reshape(n_b, n_s, IDX_HEADS, IDX_DIM), pos)
    aik = _rope(aik.reshape(n_b, n_s, 1, IDX_DIM), pos)[:, :, 0]
    aiw = aiw * IDX_W_SCALE
    bq = _rope(_rms(bq.reshape(n_b, n_s, B_HEADS, HEAD_DIM), lw['g_qb']), pos).reshape(n_b, n_s, B_KV, B_GROUP, HEAD_DIM)
    bk = _rope(_rms(bk.reshape(n_b, n_s, B_KV, HEAD_DIM), lw['g_kb']), pos)
    bv = bv.reshape(n_b, n_s, B_KV, HEAD_DIM)
    cq = cq.reshape(n_b, n_s, C_HEADS, C_HD)
    ck = ck.reshape(n_b, n_s, C_HEADS, C_HD)
    cv = cv.reshape(n_b, n_s, C_HEADS, C_HD)
    ci = ci + lw['b_if'][:C_HEADS]
    cf = cf + lw['b_if'][C_HEADS:]
    if past is None:
        a_k_all, a_v_all, a_i_all, b_k_all, b_v_all = ak, av, aik, bk, bv
        C0 = jnp.zeros((n_b, C_HEADS, C_HD, C_HD), F32)
        n0 = jnp.zeros((n_b, C_HEADS, C_HD), F32)
        m0 = jnp.zeros((n_b, C_HEADS), F32)
    else:
        a_k_all = jnp.concatenate([past['a_k'], ak], axis=1)
        a_v_all = jnp.concatenate([past['a_v'], av], axis=1)
        a_i_all = jnp.concatenate([past['a_idx'], aik], axis=1)
        b_k_all = jnp.concatenate([past['b_k'], bk], axis=1)
        b_v_all = jnp.concatenate([past['b_v'], bv], axis=1)
        C0, n0, m0 = past['c_C'], past['c_n'], past['c_m']
    oa = _dsa_attend(aq, aiq, aiw, a_k_all, a_v_all, a_i_all, pos).reshape(n_b, n_s, A_WIDTH)
    ob = _moba_attend(bq, b_k_all, b_v_all, pos).reshape(n_b, n_s, B_WIDTH)
    hc, C1, n1, m1 = _mlstm(cq, ck, cv, ci, cf, C0, n0, m0)
    oc = (_rms(hc.astype(x.dtype), lw['g_c_out'].reshape(C_HEADS, C_HD))
          * jax.nn.sigmoid(co.reshape(n_b, n_s, C_HEADS, C_HD))).reshape(n_b, n_s, C_WIDTH)
    g_a, g_b, g_c = jnp.split(gate, N_BRANCH, axis=-1)
    mix = (jax.nn.sigmoid(g_a) * (oa @ lw['w_br_a'])
           + jax.nn.sigmoid(g_b) * (ob @ lw['w_br_b'])
           + jax.nn.sigmoid(g_c) * (oc @ lw['w_br_c']))
    x = x + mix @ lw['w_out']
    x = x + _peer(_rms(x, lw['g_ffn']), lw['w_peer_q'], lw['peer_subkeys'], lw['peer_u'], lw['peer_v'])
    x = x + jax.nn.sigmoid(x @ lw['w_ple_gate']) * (p_l @ lw['w_ple'])
    return x, (ak, av, aik, bk, bv, C1, n1, m1)


def setup_inputs(seed: int = 0) -> dict:
    key = jax.random.key(seed)
    ks = iter(jax.random.split(key, 40))

    def nrm(shape, s=1.0):
        return jax.random.normal(next(ks), shape, F32) * s

    n_pages = PAST_LEN // PAGE_SIZE
    n_used = DEC_BATCH * n_pages
    n_pool = n_used + (n_used + 3) // 4
    inp = {}
    inp['x_prompt'] = nrm((BATCH, SEQ, D_MODEL))
    inp['x_sample'] = nrm((DEC_BATCH, DEC_SEQ, D_MODEL))
    inp['p_prompt'] = nrm((DEPTH, BATCH, SEQ, PLE_DIM))
    inp['p_sample'] = nrm((DEPTH, DEC_BATCH, DEC_SEQ, PLE_DIM))
    inp['cache_a_k'] = nrm((DEPTH, n_pool, PAGE_SIZE, A_KV, HEAD_DIM))
    inp['cache_a_v'] = nrm((DEPTH, n_pool, PAGE_SIZE, A_KV, HEAD_DIM))
    inp['cache_a_idx'] = nrm((DEPTH, n_pool, PAGE_SIZE, IDX_DIM))
    inp['cache_b_k'] = nrm((DEPTH, n_pool, PAGE_SIZE, B_KV, HEAD_DIM))
    inp['cache_b_v'] = nrm((DEPTH, n_pool, PAGE_SIZE, B_KV, HEAD_DIM))
    inp['state_c_C'] = nrm((DEPTH, DEC_BATCH, C_HEADS, C_HD, C_HD), 0.3)
    inp['state_c_n'] = nrm((DEPTH, DEC_BATCH, C_HEADS, C_HD), 0.3)
    inp['state_c_m'] = nrm((DEPTH, DEC_BATCH, C_HEADS), 0.5)
    inp['page_table'] = jax.random.permutation(next(ks), n_pool)[:n_used].reshape(DEC_BATCH, n_pages).astype(jnp.int32)
    inp['g_mix'] = 1.0 + nrm((DEPTH, D_MODEL), 0.02)
    inp['w_in'] = nrm((DEPTH, D_MODEL, IN_TOTAL), D_MODEL ** -0.5)
    inp['b_if'] = jnp.concatenate([nrm((DEPTH, C_HEADS), 0.1), FORGET_BIAS + nrm((DEPTH, C_HEADS), 0.5)], axis=-1)
    inp['g_qa'] = 1.0 + nrm((DEPTH, HEAD_DIM), 0.02)
    inp['g_ka'] = 1.0 + nrm((DEPTH, HEAD_DIM), 0.02)
    inp['g_qb'] = 1.0 + nrm((DEPTH, HEAD_DIM), 0.02)
    inp['g_kb'] = 1.0 + nrm((DEPTH, HEAD_DIM), 0.02)
    inp['g_c_out'] = 1.0 + nrm((DEPTH, C_WIDTH), 0.02)
    inp['w_br_a'] = nrm((DEPTH, A_WIDTH, D_MODEL), A_WIDTH ** -0.5)
    inp['w_br_b'] = nrm((DEPTH, B_WIDTH, D_MODEL), B_WIDTH ** -0.5)
    inp['w_br_c'] = nrm((DEPTH, C_WIDTH, D_MODEL), C_WIDTH ** -0.5)
    inp['w_out'] = nrm((DEPTH, D_MODEL, D_MODEL), D_MODEL ** -0.5)
    inp['g_ffn'] = 1.0 + nrm((DEPTH, D_MODEL), 0.02)
    inp['w_peer_q'] = nrm((DEPTH, D_MODEL, PEER_HEADS * PEER_QDIM), D_MODEL ** -0.5)
    inp['peer_subkeys'] = nrm((DEPTH, PEER_HEADS, 2, PEER_NKEYS, PEER_QDIM // 2), (PEER_QDIM // 2) ** -0.5)
    inp['peer_u'] = nrm((DEPTH, PEER_EXPERTS, D_MODEL), D_MODEL ** -0.5)
    inp['peer_v'] = nrm((DEPTH, PEER_EXPERTS, D_MODEL), 0.1)
    inp['w_ple'] = nrm((DEPTH, PLE_DIM, D_MODEL), PLE_DIM ** -0.5)
    inp['w_ple_gate'] = nrm((DEPTH, D_MODEL, D_MODEL), D_MODEL ** -0.5)
    return inp


def reference(x_prompt, x_sample, p_prompt, p_sample, cache_a_k, cache_a_v, cache_a_idx, cache_b_k, cache_b_v,
              state_c_C, state_c_n, state_c_m, page_table, g_mix, w_in, b_if, g_qa, g_ka, g_qb, g_kb, g_c_out,
              w_br_a, w_br_b, w_br_c, w_out, g_ffn, w_peer_q, peer_subkeys, peer_u, peer_v, w_ple, w_ple_gate):
    n_dec = x_sample.shape[0]
    past_len = page_table.shape[1] * PAGE_SIZE
    pos_p = jnp.arange(x_prompt.shape[1], dtype=jnp.int32)
    pos_s = past_len + jnp.arange(x_sample.shape[1], dtype=jnp.int32)

    def paged(cache_l):
        rows = cache_l[page_table]
        return rows.reshape(n_dec, past_len, *cache_l.shape[2:])

    yp, ys = x_prompt, x_sample
    states_p, states_s = [], []
    for l in range(DEPTH):
        lw = {'g_mix': g_mix[l], 'w_in': w_in[l], 'b_if': b_if[l], 'g_qa': g_qa[l], 'g_ka': g_ka[l],
              'g_qb': g_qb[l], 'g_kb': g_kb[l], 'g_c_out': g_c_out[l], 'w_br_a': w_br_a[l], 'w_br_b': w_br_b[l],
              'w_br_c': w_br_c[l], 'w_out': w_out[l], 'g_ffn': g_ffn[l], 'w_peer_q': w_peer_q[l],
              'peer_subkeys': peer_subkeys[l], 'peer_u': peer_u[l], 'peer_v': peer_v[l], 'w_ple': w_ple[l],
              'w_ple_gate': w_ple_gate[l]}
        yp, st_p = _layer(yp, p_prompt[l], pos_p, lw, None)
        past = {'a_k': paged(cache_a_k[l]), 'a_v': paged(cache_a_v[l]), 'a_idx': paged(cache_a_idx[l]),
                'b_k': paged(cache_b_k[l]), 'b_v': paged(cache_b_v[l]),
                'c_C': state_c_C[l], 'c_n': state_c_n[l], 'c_m': state_c_m[l]}
        ys, st_s = _layer(ys, p_sample[l], pos_s, lw, past)
        states_p.append(st_p)
        states_s.append(st_s)
    (pa_k, pa_v, pa_i, pb_k, pb_v, pc_C, pc_n, pc_m) = [jnp.stack(a) for a in zip(*states_p)]
    (sa_k, sa_v, sa_i, sb_k, sb_v, sc_C, sc_n, sc_m) = [jnp.stack(a) for a in zip(*states_s)]
    return (yp, ys, pa_k, pa_v, pa_i, pb_k, pb_v, pc_C, pc_n, pc_m,
            sa_k, sa_v, sa_i, sb_k, sb_v, sc_C, sc_n, sc_m)
```

```python
import functools
import math

import jax
import jax.numpy as jnp
import numpy as np
from jax import lax
from jax.experimental import pallas as pl
from jax.experimental.pallas import tpu as pltpu

D_MODEL = 1024
DEPTH = 4
PAGE_SIZE = 128
HEAD_DIM = 64
A_HEADS = 8
A_KV = 4
A_GROUP = A_HEADS // A_KV
A_WIDTH = A_HEADS * HEAD_DIM
IDX_HEADS = 8
IDX_DIM = 64
IDX_W_SCALE = (IDX_HEADS ** -0.5) * (IDX_DIM ** -0.5)
DSA_TOPK = 256
B_HEADS = 8
B_KV = 4
B_GROUP = B_HEADS // B_KV
B_WIDTH = B_HEADS * HEAD_DIM
MOBA_BLOCK = 256
MOBA_TOPK = 3
C_HEADS = 4
C_HD = 128
C_WIDTH = C_HEADS * C_HD
C_CHUNK = 64
N_BRANCH = 3
PEER_HEADS = 8
PEER_NKEYS = 128
PEER_QDIM = 256
PEER_TOPK = 16
PLE_DIM = 256
ROPE_THETA = 10000.0
EPS = 1e-6
Q_BLOCK = 128
MOBA_Q_BLOCK = 32
TOKEN_BLOCK = 256

IN_SIZES = (A_WIDTH, A_KV * HEAD_DIM, A_KV * HEAD_DIM, IDX_HEADS * IDX_DIM, IDX_HEADS, IDX_DIM,
            B_WIDTH, B_KV * HEAD_DIM, B_KV * HEAD_DIM,
            C_WIDTH, C_WIDTH, C_WIDTH, C_WIDTH, C_HEADS, C_HEADS,
            N_BRANCH * D_MODEL)
IN_SPLITS = tuple(int(s) for s in np.cumsum(IN_SIZES)[:-1])

F32 = jnp.float32
BF16 = jnp.bfloat16

LANES = 128
SUBLANES = 8
VMEM_LIMIT = 48 * 1024 * 1024


def _mm_kernel(x_ref, g_ref, w_ref, o_ref, h_ref, *, normalize):
    @pl.when(pl.program_id(1) == 0)
    def _():
        x = x_ref[...]
        if normalize:
            x = x * lax.rsqrt(jnp.mean(x * x, axis=-1, keepdims=True) + EPS) * g_ref[...]
        h_ref[...] = x.astype(BF16)

    o_ref[...] = jnp.dot(h_ref[...], w_ref[...], preferred_element_type=F32)


def _pick_tile(n, cands):
    for c in cands:
        if n % c == 0:
            return c
    return n


def matmul(x, w, g=None):
    m, k = x.shape
    n = w.shape[1]
    n_pad = -(-n // LANES) * LANES
    wb = w.astype(BF16)
    if n_pad != n:
        wb = jnp.pad(wb, ((0, 0), (0, n_pad - n)))
    tm = _pick_tile(m, (1024, 512, 256, 128))
    tn = _pick_tile(n_pad, (512, 256, 128))
    gg = (jnp.ones((k,), F32) if g is None else g.astype(F32)).reshape(1, k)
    out = pl.pallas_call(
        functools.partial(_mm_kernel, normalize=g is not None),
        grid=(m // tm, n_pad // tn),
        in_specs=[pl.BlockSpec((tm, k), lambda i, j: (i, 0)),
                  pl.BlockSpec((1, k), lambda i, j: (0, 0)),
                  pl.BlockSpec((k, tn), lambda i, j: (0, j))],
        out_specs=pl.BlockSpec((tm, tn), lambda i, j: (i, j)),
        out_shape=jax.ShapeDtypeStruct((m, n_pad), F32),
        scratch_shapes=[pltpu.VMEM((tm, k), BF16)],
        compiler_params=pltpu.CompilerParams(dimension_semantics=("arbitrary", "arbitrary"),
                                             vmem_limit_bytes=VMEM_LIMIT),
        name="matmul",
    )(x, gg, wb)
    return out[:, :n] if n_pad != n else out


def _mm3(x, w, g=None):
    b, s, k = x.shape
    return matmul(x.reshape(b * s, k), w, g).reshape(b, s, w.shape[1])


def _rms(x, g):
    xf = x.astype(F32)
    y = xf * lax.rsqrt(jnp.mean(xf * xf, axis=-1, keepdims=True) + EPS)
    return (y * g.astype(F32)).astype(x.dtype)


def _rope(x, pos):
    half = x.shape[-1] // 2
    freqs = jnp.power(jnp.float32(ROPE_THETA), -jnp.arange(half, dtype=F32) / half)
    ang = pos.astype(F32)[:, None] * freqs[None, :]
    cos = jnp.cos(ang)[None, :, None, :]
    sin = jnp.sin(ang)[None, :, None, :]
    xf = x.astype(F32)
    x1, x2 = xf[..., :half], xf[..., half:]
    return jnp.concatenate([x1 * cos - x2 * sin, x1 * sin + x2 * cos], axis=-1).astype(x.dtype)


def _to_blocks(a, nb):
    return jnp.moveaxis(a.reshape(a.shape[0], nb, a.shape[1] // nb, *a.shape[2:]), 1, 0)


def _from_blocks(a):
    a = jnp.moveaxis(a, 0, 1)
    return a.reshape(a.shape[0], a.shape[1] * a.shape[2], *a.shape[3:])


def _dsa_attend(q, qi, wi, k_all, v_all, ki_all, q_pos):
    n_q = q.shape[1]
    n_k = k_all.shape[1]
    top = min(DSA_TOPK, n_k // 4)
    blk = Q_BLOCK if n_q % Q_BLOCK == 0 else n_q
    nb = n_q // blk
    k_pos = jnp.arange(n_k, dtype=jnp.int32)
    scale = HEAD_DIM ** -0.5
    ki_f = ki_all.astype(F32)

    def block(args):
        qb, qib, wib, pb = args
        s = jax.nn.relu(jnp.einsum('bqhd,bld->bqhl', qib.astype(F32), ki_f))
        score = jnp.einsum('bqhl,bqh->bql', s, wib.astype(F32))
        score = jnp.where(k_pos[None, None, :] <= pb[None, :, None], score, -jnp.inf)
        _, sel = lax.top_k(score, top)
        valid = sel <= pb[None, :, None]
        kg = jax.vmap(lambda kk, ii: kk[ii])(k_all, sel)
        vg = jax.vmap(lambda vv, ii: vv[ii])(v_all, sel)
        logits = jnp.einsum('bqkgd,bqskd->bqkgs', qb.astype(F32), kg.astype(F32)) * scale
        logits = jnp.where(valid[:, :, None, None, :], logits, -jnp.inf)
        p = jax.nn.softmax(logits, axis=-1)
        return jnp.einsum('bqkgs,bqskd->bqkgd', p, vg.astype(F32)).astype(q.dtype)

    out = lax.map(block, (_to_blocks(q, nb), _to_blocks(qi, nb), _to_blocks(wi, nb), q_pos.reshape(nb, blk)))
    return _from_blocks(out)


def _moba_attend(q, k_all, v_all, q_pos):
    n_b, n_q = q.shape[:2]
    n_k = k_all.shape[1]
    n_blk = -(-n_k // MOBA_BLOCK)
    pad = n_blk * MOBA_BLOCK - n_k
    kb = jnp.pad(k_all, ((0, 0), (0, pad), (0, 0), (0, 0))).reshape(n_b, n_blk, MOBA_BLOCK, B_KV, HEAD_DIM)
    vb = jnp.pad(v_all, ((0, 0), (0, pad), (0, 0), (0, 0))).reshape(n_b, n_blk, MOBA_BLOCK, B_KV, HEAD_DIM)
    k_mean = jnp.mean(kb.astype(F32), axis=2)
    kb_h = jnp.moveaxis(kb, 3, 1)
    vb_h = jnp.moveaxis(vb, 3, 1)
    top = min(MOBA_TOPK, n_blk)
    blk = MOBA_Q_BLOCK if n_q % MOBA_Q_BLOCK == 0 else n_q
    nb = n_q // blk
    blk_ids = jnp.arange(n_blk, dtype=jnp.int32)
    in_blk = jnp.arange(MOBA_BLOCK, dtype=jnp.int32)
    b_ix = jnp.arange(n_b)
    kv_ix = jnp.arange(B_KV)
    scale = HEAD_DIM ** -0.5

    def block(args):
        qb, pb = args
        qf = qb.astype(F32)
        own = pb // MOBA_BLOCK
        gate = jnp.einsum('bqkgd,bnkd->bqkgn', qf, k_mean)
        past = blk_ids[None, :] < own[:, None]
        gate = jnp.where(past[None, :, None, None, :], gate, -jnp.inf)
        _, sel = lax.top_k(gate, top)
        sel_ok = sel < own[None, :, None, None, None]
        bi = b_ix[:, None, None, None, None]
        ki = kv_ix[None, None, :, None, None]
        kg = kb_h[bi, ki, sel]
        vg = vb_h[bi, ki, sel]
        l_sel = jnp.einsum('bqkgd,bqkgnsd->bqkgns', qf, kg.astype(F32)) * scale
        l_sel = jnp.where(sel_ok[..., None], l_sel, -jnp.inf).reshape(n_b, blk, B_KV, B_GROUP, top * MOBA_BLOCK)
        ko = kb[b_ix[:, None], own[None, :]]
        vo = vb[b_ix[:, None], own[None, :]]
        l_own = jnp.einsum('bqkgd,bqskd->bqkgs', qf, ko.astype(F32)) * scale
        own_pos = own[:, None] * MOBA_BLOCK + in_blk[None, :]
        l_own = jnp.where((own_pos <= pb[:, None])[None, :, None, None, :], l_own, -jnp.inf)
        p = jax.nn.softmax(jnp.concatenate([l_sel, l_own], axis=-1), axis=-1)
        p_sel = p[..., :top * MOBA_BLOCK].reshape(n_b, blk, B_KV, B_GROUP, top, MOBA_BLOCK)
        p_own = p[..., top * MOBA_BLOCK:]
        out = (jnp.einsum('bqkgns,bqkgnsd->bqkgd', p_sel, vg.astype(F32))
               + jnp.einsum('bqkgs,bqskd->bqkgd', p_own, vo.astype(F32)))
        return out.astype(q.dtype)

    out = lax.map(block, (_to_blocks(q, nb), q_pos.reshape(nb, blk)))
    return _from_blocks(out)


def _mlstm(q, k, v, i_pre, f_pre, C0, n0, m0):
    n_s = q.shape[1]
    L = C_CHUNK if n_s % C_CHUNK == 0 else n_s
    nc = n_s // L
    q = q.astype(F32)
    k = k.astype(F32) * (C_HD ** -0.5)
    v = v.astype(F32)
    i_pre = i_pre.astype(F32)
    logf = jax.nn.log_sigmoid(f_pre.astype(F32))
    tri = jnp.tril(jnp.ones((L, L), dtype=bool))

    def step(carry, xs):
        C, n, m = carry
        qc, kc, vc, ic, fc = xs
        b = jnp.moveaxis(jnp.cumsum(fc, axis=1), 1, 2)
        it = jnp.moveaxis(ic, 1, 2)
        dmat = jnp.where(tri, b[..., :, None] - b[..., None, :] + it[..., None, :], -jnp.inf)
        m_t = jnp.maximum(m[..., None] + b, jnp.max(dmat, axis=-1))
        inter = jnp.exp(m[..., None] + b - m_t)
        s = jnp.einsum('blhd,bshd->bhls', qc, kc) * jnp.exp(dmat - m_t[..., None])
        num = jnp.einsum('bhls,bshd->bhld', s, vc) + inter[..., None] * jnp.einsum('blhd,bhde->bhle', qc, C)
        den = jnp.sum(s, axis=-1) + inter * jnp.einsum('blhd,bhd->bhl', qc, n)
        h = num / jnp.maximum(jnp.abs(den), jnp.exp(-m_t))[..., None]
        m_new = m_t[..., -1]
        decay = jnp.exp(m + b[..., -1] - m_new)
        wk = jnp.exp(b[..., -1:] - b + it - m_new[..., None])
        C_new = decay[..., None, None] * C + jnp.einsum('bhs,bshd,bshe->bhde', wk, kc, vc)
        n_new = decay[..., None] * n + jnp.einsum('bhs,bshd->bhd', wk, kc)
        return (C_new, n_new, m_new), jnp.moveaxis(h, 1, 2)

    xs = (_to_blocks(q, nc), _to_blocks(k, nc), _to_blocks(v, nc), _to_blocks(i_pre, nc), _to_blocks(logf, nc))
    (C, n, m), h = lax.scan(step, (C0.astype(F32), n0.astype(F32), m0.astype(F32)), xs)
    return _from_blocks(h), C, n, m


def _peer(h_in, g_ffn, w_q, subkeys, u_tab, v_tab):
    n_b, n_s, d = h_in.shape
    hq = _mm3(h_in, w_q, g_ffn)
    hn = _rms(h_in, g_ffn)
    hf = hn.reshape(n_b * n_s, d)
    qf = hq.reshape(n_b * n_s, -1)
    n_tok = hf.shape[0]
    blk = min(TOKEN_BLOCK, n_tok)
    nb = n_tok // blk
    half = PEER_QDIM // 2
    sk = subkeys.astype(F32)

    def block(args):
        hb, qb = args
        q = qb.reshape(blk, PEER_HEADS, 2, half).astype(F32)
        s = jnp.einsum('nhpd,hpkd->nhpk', q, sk)
        sv, si = lax.top_k(s, PEER_TOPK)
        cand = (sv[:, :, 0, :, None] + sv[:, :, 1, None, :]).reshape(blk, PEER_HEADS, PEER_TOPK * PEER_TOPK)
        cid = (si[:, :, 0, :, None] * PEER_NKEYS + si[:, :, 1, None, :]).reshape(blk, PEER_HEADS, PEER_TOPK * PEER_TOPK)
        tv, ti = lax.top_k(cand, PEER_TOPK)
        eid = jnp.take_along_axis(cid, ti, axis=-1)
        g = jax.nn.softmax(tv, axis=-1)
        u = u_tab[eid].astype(F32)
        a = jax.nn.gelu(jnp.einsum('nd,nhkd->nhk', hb.astype(F32), u))
        out = jnp.einsum('nhk,nhkd->nd', g * a, v_tab[eid].astype(F32))
        return out.astype(h_in.dtype)

    out = lax.map(block, (hf.reshape(nb, blk, d), qf.reshape(nb, blk, -1))).reshape(nb * blk, d)
    return out.reshape(n_b, n_s, d)


def _layer(x, p_l, pos, lw, past):
    n_b, n_s, _ = x.shape
    z = _mm3(x, lw['w_in'], lw['g_mix'])
    (aq, ak, av, aiq, aiw, aik, bq, bk, bv, cq, ck, cv, co, ci, cf, gate) = jnp.split(z, IN_SPLITS, axis=-1)
    aq = _rope(_rms(aq.reshape(n_b, n_s, A_HEADS, HEAD_DIM), lw['g_qa']), pos).reshape(n_b, n_s, A_KV, A_GROUP, HEAD_DIM)
    ak = _rope(_rms(ak.reshape(n_b, n_s, A_KV, HEAD_DIM), lw['g_ka']), pos)
    av = av.reshape(n_b, n_s, A_KV, HEAD_DIM)
    aiq = _rope(aiq.reshape(n_b, n_s, IDX_HEADS, IDX_DIM), pos)
    aik = _rope(aik.reshape(n_b, n_s, 1, IDX_DIM), pos)[:, :, 0]
    aiw = aiw * IDX_W_SCALE
    bq = _rope(_rms(bq.reshape(n_b, n_s, B_HEADS, HEAD_DIM), lw['g_qb']), pos).reshape(n_b, n_s, B_KV, B_GROUP, HEAD_DIM)
    bk = _rope(_rms(bk.reshape(n_b, n_s, B_KV, HEAD_DIM), lw['g_kb']), pos)
    bv = bv.reshape(n_b, n_s, B_KV, HEAD_DIM)
    cq = cq.reshape(n_b, n_s, C_HEADS, C_HD)
    ck = ck.reshape(n_b, n_s, C_HEADS, C_HD)
    cv = cv.reshape(n_b, n_s, C_HEADS, C_HD)
    ci = ci + lw['b_if'][:C_HEADS]
    cf = cf + lw['b_if'][C_HEADS:]
    if past is None:
        a_k_all, a_v_all, a_i_all, b_k_all, b_v_all = ak, av, aik, bk, bv
        C0 = jnp.zeros((n_b, C_HEADS, C_HD, C_HD), F32)
        n0 = jnp.zeros((n_b, C_HEADS, C_HD), F32)
        m0 = jnp.zeros((n_b, C_HEADS), F32)
    else:
        a_k_all = jnp.concatenate([past['a_k'], ak], axis=1)
        a_v_all = jnp.concatenate([past['a_v'], av], axis=1)
        a_i_all = jnp.concatenate([past['a_idx'], aik], axis=1)
        b_k_all = jnp.concatenate([past['b_k'], bk], axis=1)
        b_v_all = jnp.concatenate([past['b_v'], bv], axis=1)
        C0, n0, m0 = past['c_C'], past['c_n'], past['c_m']
    oa = _dsa_attend(aq, aiq, aiw, a_k_all, a_v_all, a_i_all, pos).reshape(n_b, n_s, A_WIDTH)
    ob = _moba_attend(bq, b_k_all, b_v_all, pos).reshape(n_b, n_s, B_WIDTH)
    hc, C1, n1, m1 = _mlstm(cq, ck, cv, ci, cf, C0, n0, m0)
    oc = (_rms(hc.astype(x.dtype), lw['g_c_out'].reshape(C_HEADS, C_HD))
          * jax.nn.sigmoid(co.reshape(n_b, n_s, C_HEADS, C_HD))).reshape(n_b, n_s, C_WIDTH)
    g_a, g_b, g_c = jnp.split(gate, N_BRANCH, axis=-1)
    mix = (jax.nn.sigmoid(g_a) * _mm3(oa, lw['w_br_a'])
           + jax.nn.sigmoid(g_b) * _mm3(ob, lw['w_br_b'])
           + jax.nn.sigmoid(g_c) * _mm3(oc, lw['w_br_c']))
    x = x + _mm3(mix, lw['w_out'])
    x = x + _peer(x, lw['g_ffn'], lw['w_peer_q'], lw['peer_subkeys'], lw['peer_u'], lw['peer_v'])
    x = x + jax.nn.sigmoid(_mm3(x, lw['w_ple_gate'])) * _mm3(p_l, lw['w_ple'])
    return x, (ak, av, aik, bk, bv, C1, n1, m1)


def kernel(x_prompt, x_sample, p_prompt, p_sample, cache_a_k, cache_a_v, cache_a_idx, cache_b_k, cache_b_v,
           state_c_C, state_c_n, state_c_m, page_table, g_mix, w_in, b_if, g_qa, g_ka, g_qb, g_kb, g_c_out,
           w_br_a, w_br_b, w_br_c, w_out, g_ffn, w_peer_q, peer_subkeys, peer_u, peer_v, w_ple, w_ple_gate):
    n_dec = x_sample.shape[0]
    past_len = page_table.shape[1] * PAGE_SIZE
    pos_p = jnp.arange(x_prompt.shape[1], dtype=jnp.int32)
    pos_s = past_len + jnp.arange(x_sample.shape[1], dtype=jnp.int32)

    def paged(cache_l):
        rows = cache_l[page_table]
        return rows.reshape(n_dec, past_len, *cache_l.shape[2:])

    yp, ys = x_prompt, x_sample
    states_p, states_s = [], []
    for l in range(DEPTH):
        lw = {'g_mix': g_mix[l], 'w_in': w_in[l], 'b_if': b_if[l], 'g_qa': g_qa[l], 'g_ka': g_ka[l],
              'g_qb': g_qb[l], 'g_kb': g_kb[l], 'g_c_out': g_c_out[l], 'w_br_a': w_br_a[l], 'w_br_b': w_br_b[l],
              'w_br_c': w_br_c[l], 'w_out': w_out[l], 'g_ffn': g_ffn[l], 'w_peer_q': w_peer_q[l],
              'peer_subkeys': peer_subkeys[l], 'peer_u': peer_u[l], 'peer_v': peer_v[l], 'w_ple': w_ple[l],
              'w_ple_gate': w_ple_gate[l]}
        yp, st_p = _layer(yp, p_prompt[l], pos_p, lw, None)
        past = {'a_k': paged(cache_a_k[l]), 'a_v': paged(cache_a_v[l]), 'a_idx': paged(cache_a_idx[l]),
                'b_k': paged(cache_b_k[l]), 'b_v': paged(cache_b_v[l]),
                'c_C': state_c_C[l], 'c_n': state_c_n[l], 'c_m': state_c_m[l]}
        ys, st_s = _layer(ys, p_sample[l], pos_s, lw, past)
        states_p.append(st_p)
        states_s.append(st_s)
    (pa_k, pa_v, pa_i, pb_k, pb_v, pc_C, pc_n, pc_m) = [jnp.stack(a) for a in zip(*states_p)]
    (sa_k, sa_v, sa_i, sb_k, sb_v, sc_C, sc_n, sc_m) = [jnp.stack(a) for a in zip(*states_s)]
    return (yp, ys, pa_k, pa_v, pa_i, pb_k, pb_v, pc_C, pc_n, pc_m,
            sa_k, sa_v, sa_i, sb_k, sb_v, sc_C, sc_n, sc_m)
```

```python
import functools
import math

import jax
import jax.numpy as jnp
import numpy as np
from jax import lax
from jax.experimental import pallas as pl
from jax.experimental.pallas import tpu as pltpu

D_MODEL = 1024
DEPTH = 4
PAGE_SIZE = 128
HEAD_DIM = 64
A_HEADS = 8
A_KV = 4
A_GROUP = A_HEADS // A_KV
A_WIDTH = A_HEADS * HEAD_DIM
IDX_HEADS = 8
IDX_DIM = 64
IDX_W_SCALE = (IDX_HEADS ** -0.5) * (IDX_DIM ** -0.5)
DSA_TOPK = 256
B_HEADS = 8
B_KV = 4
B_GROUP = B_HEADS // B_KV
B_WIDTH = B_HEADS * HEAD_DIM
MOBA_BLOCK = 256
MOBA_TOPK = 3
C_HEADS = 4
C_HD = 128
C_WIDTH = C_HEADS * C_HD
C_CHUNK = 64
N_BRANCH = 3
PEER_HEADS = 8
PEER_NKEYS = 128
PEER_QDIM = 256
PEER_TOPK = 16
PLE_DIM = 256
ROPE_THETA = 10000.0
EPS = 1e-6
Q_BLOCK = 128
MOBA_Q_BLOCK = 32
TOKEN_BLOCK = 256

IN_SIZES = (A_WIDTH, A_KV * HEAD_DIM, A_KV * HEAD_DIM, IDX_HEADS * IDX_DIM, IDX_HEADS, IDX_DIM,
            B_WIDTH, B_KV * HEAD_DIM, B_KV * HEAD_DIM,
            C_WIDTH, C_WIDTH, C_WIDTH, C_WIDTH, C_HEADS, C_HEADS,
            N_BRANCH * D_MODEL)
IN_SPLITS = tuple(int(s) for s in np.cumsum(IN_SIZES)[:-1])

F32 = jnp.float32
BF16 = jnp.bfloat16

LANES = 128
SUBLANES = 8
VMEM_LIMIT = 48 * 1024 * 1024


def _mm_kernel(x_ref, g_ref, w_ref, o_ref, h_ref, *, normalize):
    @pl.when(pl.program_id(1) == 0)
    def _():
        x = x_ref[...]
        if normalize:
            x = x * lax.rsqrt(jnp.mean(x * x, axis=-1, keepdims=True) + EPS) * g_ref[...]
        h_ref[...] = x.astype(BF16)

    o_ref[...] = jnp.dot(h_ref[...], w_ref[...], preferred_element_type=F32)


def _pick_tile(n, cands):
    for c in cands:
        if n % c == 0:
            return c
    return n


def matmul(x, w, g=None):
    m, k = x.shape
    n = w.shape[1]
    n_pad = -(-n // LANES) * LANES
    wb = w.astype(BF16)
    if n_pad != n:
        wb = jnp.pad(wb, ((0, 0), (0, n_pad - n)))
    tm = _pick_tile(m, (1024, 512, 256, 128))
    tn = _pick_tile(n_pad, (512, 256, 128))
    gg = (jnp.ones((k,), F32) if g is None else g.astype(F32)).reshape(1, k)
    out = pl.pallas_call(
        functools.partial(_mm_kernel, normalize=g is not None),
        grid=(m // tm, n_pad // tn),
        in_specs=[pl.BlockSpec((tm, k), lambda i, j: (i, 0)),
                  pl.BlockSpec((1, k), lambda i, j: (0, 0)),
                  pl.BlockSpec((k, tn), lambda i, j: (0, j))],
        out_specs=pl.BlockSpec((tm, tn), lambda i, j: (i, j)),
        out_shape=jax.ShapeDtypeStruct((m, n_pad), F32),
        scratch_shapes=[pltpu.VMEM((tm, k), BF16)],
        compiler_params=pltpu.CompilerParams(dimension_semantics=("arbitrary", "arbitrary"),
                                             vmem_limit_bytes=VMEM_LIMIT),
        name="matmul",
    )(x, gg, wb)
    return out[:, :n] if n_pad != n else out


def _mm3(x, w, g=None):
    b, s, k = x.shape
    return matmul(x.reshape(b * s, k), w, g).reshape(b, s, w.shape[1])


INT_MIN = -2 ** 31
INT_MAX = 2 ** 31 - 1
NEG_BIG = -1e30
ATT_TQ = 128
ATT_TK = 256


def _col_sum_i32(mask):
    rows, tq = mask.shape
    return jnp.sum(mask.astype(jnp.int32).reshape(rows // SUBLANES, SUBLANES, tq), axis=0)


def _dsa_prompt_kernel(qit_ref, wt_ref, ki_ref, qt_ref, k_ref, vt_ref, o_ref,
                       key_ref, j_ref, m_ref, l_ref, acc_ref, *, top):
    tq, tk = ATT_TQ, ATT_TK
    i = pl.program_id(1)
    n_chunks = ((i + 1) * tq + tk - 1) // tk
    qpos = i * tq + lax.broadcasted_iota(jnp.int32, (1, tq), 1)
    row = lax.broadcasted_iota(jnp.int32, (tk, tq), 0)

    def score_chunk(c, carry):
        start = pl.multiple_of(c * tk, tk)
        kic = ki_ref[0, pl.ds(start, tk), :]
        s = jnp.zeros((tk, tq), F32)
        for h in range(IDX_HEADS):
            d = jnp.dot(kic, qit_ref[0, h], preferred_element_type=F32)
            s = s + wt_ref[0, h:h + 1, :] * jnp.maximum(d, 0.0)
        s = jnp.where(s == 0.0, 0.0, s)
        bits = pltpu.bitcast(s, jnp.int32)
        key = jnp.where(bits < 0, bits ^ INT_MAX, bits)
        key = jnp.where(start + row <= qpos, key, INT_MIN)
        key_ref[pl.ds(start, tk), :] = key
        return carry

    lax.fori_loop(0, n_chunks, score_chunk, 0)

    def count(pred):
        def body(c, acc):
            start = pl.multiple_of(c * tk, tk)
            return acc + _col_sum_i32(pred(key_ref[pl.ds(start, tk), :], start + row))
        acc = lax.fori_loop(0, n_chunks, body, jnp.zeros((SUBLANES, tq), jnp.int32))
        return jnp.sum(acc, axis=0, keepdims=True)

    def bis_cond(st):
        it, lo, hi, _ = st
        return (it < 33) & (jnp.max(jnp.where(hi != lo + 1, 1, 0)) > 0)

    def bis_body(st):
        it, lo, hi, cnt_lo = st
        mid = (lo >> 1) + (hi >> 1) + (lo & hi & 1)
        c = count(lambda kc, _: kc >= mid)
        ge = c >= top
        lo2 = jnp.where(ge, mid, lo)
        hi2 = jnp.where(c == top, mid + 1, jnp.where(ge, hi, mid))
        return it + 1, lo2, hi2, jnp.where(ge, c, cnt_lo)

    _, t, _, cnt_t = lax.while_loop(
        bis_cond, bis_body,
        (jnp.int32(0), jnp.full((1, tq), INT_MIN, jnp.int32), jnp.full((1, tq), INT_MAX, jnp.int32),
         jnp.zeros((1, tq), jnp.int32) + n_chunks * tk))

    need = (cnt_t > top) & (t > INT_MIN)
    j_ref[...] = jnp.full((1, tq), INT_MAX, jnp.int32)

    @pl.when(jnp.max(need.astype(jnp.int32)) > 0)
    def _():
        r = top - count(lambda kc, _: kc > t)

        def jb(_, st):
            lo, hi = st
            mid = (lo + hi) >> 1
            ok = count(lambda kc, kp: (kc == t) & (kp <= mid)) >= r
            return jnp.where(ok, lo, mid), jnp.where(ok, mid, hi)

        n_bits = int(math.ceil(math.log2(key_ref.shape[0]))) + 1
        _, hi = lax.fori_loop(0, n_bits, jb, (jnp.full((1, tq), -1, jnp.int32),
                                              jnp.zeros((1, tq), jnp.int32) + (n_chunks * tk - 1)))
        j_ref[...] = jnp.where(need, hi, INT_MAX)

    m_ref[...] = jnp.full(m_ref.shape, NEG_BIG, F32)
    l_ref[...] = jnp.zeros(l_ref.shape, F32)
    acc_ref[...] = jnp.zeros(acc_ref.shape, F32)
    jcut = j_ref[...]

    def att_chunk(c, carry):
        start = pl.multiple_of(c * tk, tk)
        kc = key_ref[pl.ds(start, tk), :]
        sel = (kc > INT_MIN) & ((kc > t) | ((kc == t) & (start + row <= jcut)))
        for h in range(A_HEADS):
            kv = h // A_GROUP
            s = jnp.dot(k_ref[0, kv, pl.ds(start, tk), :], qt_ref[0, h], preferred_element_type=F32)
            m_old = m_ref[h]
            m_new = jnp.maximum(m_old, jnp.max(jnp.where(sel, s, NEG_BIG), axis=0, keepdims=True))
            p = jnp.where(sel, jnp.exp(s - m_new), 0.0)
            alpha = jnp.exp(m_old - m_new)
            l_ref[h] = alpha * l_ref[h] + jnp.sum(p, axis=0, keepdims=True)
            acc_ref[h] = alpha * acc_ref[h] + jnp.dot(vt_ref[0, kv, c], p.astype(BF16),
                                                      preferred_element_type=F32)
            m_ref[h] = m_new
        return carry

    lax.fori_loop(0, n_chunks, att_chunk, 0)
    for h in range(A_HEADS):
        o_ref[0, h] = acc_ref[h] / l_ref[h]


def dsa_prompt(qit, wt, ki, qt, k, vt, top):
    n_b, _, _, n_s = qt.shape
    tq, tk = ATT_TQ, ATT_TK
    return pl.pallas_call(
        functools.partial(_dsa_prompt_kernel, top=top),
        grid=(n_b, n_s // tq),
        in_specs=[pl.BlockSpec((1, IDX_HEADS, IDX_DIM, tq), lambda b, i: (b, 0, 0, i)),
                  pl.BlockSpec((1, IDX_HEADS, tq), lambda b, i: (b, 0, i)),
                  pl.BlockSpec((1, n_s, IDX_DIM), lambda b, i: (b, 0, 0)),
                  pl.BlockSpec((1, A_HEADS, HEAD_DIM, tq), lambda b, i: (b, 0, 0, i)),
                  pl.BlockSpec((1, A_KV, n_s, HEAD_DIM), lambda b, i: (b, 0, 0, 0)),
                  pl.BlockSpec((1, A_KV, n_s // tk, HEAD_DIM, tk), lambda b, i: (b, 0, 0, 0, 0))],
        out_specs=pl.BlockSpec((1, A_HEADS, HEAD_DIM, tq), lambda b, i: (b, 0, 0, i)),
        out_shape=jax.ShapeDtypeStruct((n_b, A_HEADS, HEAD_DIM, n_s), F32),
        scratch_shapes=[pltpu.VMEM((n_s, tq), jnp.int32),
                        pltpu.VMEM((1, tq), jnp.int32),
                        pltpu.VMEM((A_HEADS, 1, tq), F32),
                        pltpu.VMEM((A_HEADS, 1, tq), F32),
                        pltpu.VMEM((A_HEADS, HEAD_DIM, tq), F32)],
        compiler_params=pltpu.CompilerParams(dimension_semantics=("arbitrary", "arbitrary"),
                                             vmem_limit_bytes=VMEM_LIMIT),
        name="dsa_prompt",
    )(qit, wt, ki, qt, k, vt)


def _moba_prompt_kernel(qt_ref, k_ref, vt_ref, avg_ref, o_ref, kmean_ref, selm_ref, m_ref, l_ref, acc_ref,
                        *, top):
    tq, tk = ATT_TQ, MOBA_BLOCK
    n_blk = avg_ref.shape[0]
    i = pl.program_id(1)
    n_own = (i * tq) // tk
    qpos = i * tq + lax.broadcasted_iota(jnp.int32, (1, tq), 1)
    row = lax.broadcasted_iota(jnp.int32, (tk, tq), 0)
    nrow = lax.broadcasted_iota(jnp.int32, (n_blk, tq), 0)

    @pl.when(i == 0)
    def _():
        for kv in range(B_KV):
            kmean_ref[kv] = jnp.dot(avg_ref[...], k_ref[0, kv], preferred_element_type=F32).astype(BF16)

    for h in range(B_HEADS):
        g = jnp.dot(kmean_ref[h // B_GROUP], qt_ref[0, h], preferred_element_type=F32)
        g = jnp.where(nrow < n_own, g, -jnp.inf)
        picked = jnp.zeros((n_blk, tq), jnp.bool_)
        for _ in range(top):
            mx = jnp.max(g, axis=0, keepdims=True)
            first = jnp.min(jnp.where(g == mx, nrow, n_blk), axis=0, keepdims=True)
            hit = nrow == first
            picked = picked | hit
            g = jnp.where(hit, -jnp.inf, g)
        selm_ref[h] = jnp.where(picked & (nrow < n_own), 1.0, 0.0)

    m_ref[...] = jnp.full(m_ref.shape, NEG_BIG, F32)
    l_ref[...] = jnp.zeros(l_ref.shape, F32)
    acc_ref[...] = jnp.zeros(acc_ref.shape, F32)

    def att_block(n, carry):
        start = pl.multiple_of(n * tk, tk)
        causal = start + row <= qpos
        own = (jnp.zeros((1, tq), jnp.int32) + n) == n_own
        for h in range(B_HEADS):
            kv = h // B_GROUP
            picked_n = jnp.max(jnp.where(nrow == n, selm_ref[h], 0.0), axis=0, keepdims=True) > 0.0
            sel = (own & causal) | picked_n
            s = jnp.dot(k_ref[0, kv, pl.ds(start, tk), :], qt_ref[0, h], preferred_element_type=F32)
            m_old = m_ref[h]
            m_new = jnp.maximum(m_old, jnp.max(jnp.where(sel, s, NEG_BIG), axis=0, keepdims=True))
            p = jnp.where(sel, jnp.exp(s - m_new), 0.0)
            alpha = jnp.exp(m_old - m_new)
            l_ref[h] = alpha * l_ref[h] + jnp.sum(p, axis=0, keepdims=True)
            acc_ref[h] = alpha * acc_ref[h] + jnp.dot(vt_ref[0, kv, n], p.astype(BF16),
                                                      preferred_element_type=F32)
            m_ref[h] = m_new
        return carry

    lax.fori_loop(0, n_own + 1, att_block, 0)
    for h in range(B_HEADS):
        o_ref[0, h] = acc_ref[h] / l_ref[h]


def moba_prompt(qt, k, vt):
    n_b, _, _, n_s = qt.shape
    tq, tk = ATT_TQ, MOBA_BLOCK
    n_blk = n_s // tk
    avg = (jnp.repeat(jnp.eye(n_blk, dtype=F32), tk, axis=1) / tk).astype(BF16)
    return pl.pallas_call(
        functools.partial(_moba_prompt_kernel, top=min(MOBA_TOPK, n_blk)),
        grid=(n_b, n_s // tq),
        in_specs=[pl.BlockSpec((1, B_HEADS, HEAD_DIM, tq), lambda b, i: (b, 0, 0, i)),
                  pl.BlockSpec((1, B_KV, n_s, HEAD_DIM), lambda b, i: (b, 0, 0, 0)),
                  pl.BlockSpec((1, B_KV, n_blk, HEAD_DIM, tk), lambda b, i: (b, 0, 0, 0, 0)),
                  pl.BlockSpec((n_blk, n_s), lambda b, i: (0, 0))],
        out_specs=pl.BlockSpec((1, B_HEADS, HEAD_DIM, tq), lambda b, i: (b, 0, 0, i)),
        out_shape=jax.ShapeDtypeStruct((n_b, B_HEADS, HEAD_DIM, n_s), F32),
        scratch_shapes=[pltpu.VMEM((B_KV, n_blk, HEAD_DIM), BF16),
                        pltpu.VMEM((B_HEADS, n_blk, tq), F32),
                        pltpu.VMEM((B_HEADS, 1, tq), F32),
                        pltpu.VMEM((B_HEADS, 1, tq), F32),
                        pltpu.VMEM((B_HEADS, HEAD_DIM, tq), F32)],
        compiler_params=pltpu.CompilerParams(dimension_semantics=("arbitrary", "arbitrary"),
                                             vmem_limit_bytes=VMEM_LIMIT),
        name="moba_prompt",
    )(qt, k, vt, avg)


PEER_TT = 256
PEER_TE = 1024
PEER_NSORT = PEER_TOPK + 1
PEER_SVROWS = 24


def _extract_sorted(s, n, emit):
    rows, tt = s.shape
    rowi = lax.broadcasted_iota(jnp.int32, (rows, tt), 0)
    for r in range(n):
        mx = jnp.max(s, axis=0, keepdims=True)
        emit(r, mx)
        if r + 1 < n:
            first = jnp.min(jnp.where(s == mx, rowi, rows), axis=0, keepdims=True)
            s = jnp.where(rowi == first, -jnp.inf, s)


def _peer_select_kernel(x_ref, g_ref, wqt_ref, sk_ref, hb_ref, s1_ref, e1_ref, thr_ref, e0_ref,
                        q_ref, s_ref, sv_ref, top_ref):
    half = PEER_QDIM // 2
    x = x_ref[...]
    hb = (x * lax.rsqrt(jnp.mean(x * x, axis=-1, keepdims=True) + EPS) * g_ref[...]).astype(BF16)
    hb_ref[...] = hb
    q_ref[...] = lax.dot_general(wqt_ref[...], hb, (((1,), (1,)), ((), ())),
                                 preferred_element_type=F32).astype(BF16)
    sv_ref[...] = jnp.full(sv_ref.shape, -jnp.inf, F32)

    def head(h, carry):
        for p in range(2):
            hp = h * 2 + p
            s = jnp.dot(sk_ref[hp], q_ref[pl.ds(pl.multiple_of(hp * half, half), half), :],
                        preferred_element_type=F32)
            s_ref[p] = s

            def emit(r, v, p=p):
                sv_ref[p, r:r + 1, :] = v
            _extract_sorted(s, PEER_NSORT, emit)
        sv0, sv1 = sv_ref[0], sv_ref[1]
        cand = jnp.concatenate(
            [sv0[0:1] + sv1] + [sv0[a:a + 1] + sv1[0:8] for a in range(1, 8)] + [sv0[8:24] + sv1[0:1]], axis=0)

        def emit_top(r, v):
            top_ref[r:r + 1, :] = v
        _extract_sorted(cand, PEER_NSORT, emit_top)
        tau = 0.5 * (top_ref[PEER_TOPK - 1:PEER_TOPK, :] + top_ref[PEER_TOPK:PEER_TOPK + 1, :])
        cmax = sv0[0:1] + sv1[0:1]
        z = jnp.sum(jnp.where(cand >= tau, jnp.exp(cand - cmax), 0.0), axis=0, keepdims=True)
        s0, s1 = s_ref[0], s_ref[1]
        s1_ref[h] = s1
        e1_ref[h] = jnp.exp(s1 - sv1[0:1]) / z
        thr_ref[h] = tau - s0
        e0_ref[h] = jnp.exp(s0 - sv0[0:1])
        return carry

    lax.fori_loop(0, PEER_HEADS, head, 0)


def _peer_apply_kernel(hb_ref, u_ref, vt_ref, s1_ref, e1_ref, thr_ref, e0_ref, o_ref, g_ref):
    e = pl.program_id(1)
    for il in range(PEER_TE // PEER_NKEYS):
        acc = jnp.zeros((PEER_NKEYS, hb_ref.shape[0]), F32)
        for h in range(PEER_HEADS):
            thr_row = thr_ref[h, 0, il:il + 1, :]
            e0_row = e0_ref[h, 0, il:il + 1, :]
            acc = acc + jnp.where(s1_ref[h] >= thr_row, e1_ref[h] * e0_row, 0.0)
        g_ref[il * PEER_NKEYS:(il + 1) * PEER_NKEYS, :] = acc
    a = lax.dot_general(u_ref[...], hb_ref[...], (((1,), (1,)), ((), ())), preferred_element_type=F32)
    ga = (g_ref[...] * jax.nn.gelu(a)).astype(BF16)
    contrib = jnp.dot(vt_ref[...], ga, preferred_element_type=F32)

    @pl.when(e == 0)
    def _():
        o_ref[...] = contrib

    @pl.when(e != 0)
    def _():
        o_ref[...] += contrib


def peer(x, g_ffn, w_q, subkeys, u_tab, v_tab):
    n_tok, d = x.shape
    n_exp = u_tab.shape[0]
    tt = _pick_tile(n_tok, (PEER_TT, 128))
    half = PEER_QDIM // 2
    wqt = w_q.T.astype(BF16)
    sk = subkeys.reshape(PEER_HEADS * 2, PEER_NKEYS, half).astype(BF16)
    stat = jax.ShapeDtypeStruct((PEER_HEADS, PEER_NKEYS, n_tok), F32)
    stat_spec = pl.BlockSpec((PEER_HEADS, PEER_NKEYS, tt), lambda t: (0, 0, t))
    hb, s1, e1, thr, e0 = pl.pallas_call(
        _peer_select_kernel,
        grid=(n_tok // tt,),
        in_specs=[pl.BlockSpec((tt, d), lambda t: (t, 0)),
                  pl.BlockSpec((1, d), lambda t: (0, 0)),
                  pl.BlockSpec((PEER_HEADS * PEER_QDIM, d), lambda t: (0, 0)),
                  pl.BlockSpec((PEER_HEADS * 2, PEER_NKEYS, half), lambda t: (0, 0, 0))],
        out_specs=[pl.BlockSpec((tt, d), lambda t: (t, 0)), stat_spec, stat_spec, stat_spec, stat_spec],
        out_shape=[jax.ShapeDtypeStruct((n_tok, d), BF16), stat, stat, stat, stat],
        scratch_shapes=[pltpu.VMEM((PEER_HEADS * PEER_QDIM, tt), BF16),
                        pltpu.VMEM((2, PEER_NKEYS, tt), F32),
                        pltpu.VMEM((2, PEER_SVROWS, tt), F32),
                        pltpu.VMEM((PEER_SVROWS, tt), F32)],
        compiler_params=pltpu.CompilerParams(dimension_semantics=("arbitrary",),
                                             vmem_limit_bytes=VMEM_LIMIT),
        name="peer_select",
    )(x, g_ffn.astype(F32).reshape(1, d), wqt, sk)

    n_il = PEER_TE // PEER_NKEYS
    thr4 = thr.reshape(PEER_HEADS, PEER_NKEYS // n_il, n_il, n_tok)
    e04 = e0.reshape(PEER_HEADS, PEER_NKEYS // n_il, n_il, n_tok)
    stat2 = pl.BlockSpec((PEER_HEADS, PEER_NKEYS, tt), lambda t, e: (0, 0, t))
    stat4 = pl.BlockSpec((PEER_HEADS, 1, n_il, tt), lambda t, e: (0, e, 0, t))
    out_t = pl.pallas_call(
        _peer_apply_kernel,
        grid=(n_tok // tt, n_exp // PEER_TE),
        in_specs=[pl.BlockSpec((tt, d), lambda t, e: (t, 0)),
                  pl.BlockSpec((PEER_TE, d), lambda t, e: (e, 0)),
                  pl.BlockSpec((d, PEER_TE), lambda t, e: (0, e)),
                  stat2, stat2, stat4, stat4],
        out_specs=pl.BlockSpec((d, tt), lambda t, e: (0, t)),
        out_shape=jax.ShapeDtypeStruct((d, n_tok), F32),
        scratch_shapes=[pltpu.VMEM((PEER_TE, tt), F32)],
        compiler_params=pltpu.CompilerParams(dimension_semantics=("arbitrary", "arbitrary"),
                                             vmem_limit_bytes=VMEM_LIMIT),
        name="peer_apply",
    )(hb, u_tab.astype(BF16), v_tab.T.astype(BF16), s1, e1, thr4, e04)
    return out_t.T


def _rms(x, g):
    xf = x.astype(F32)
    y = xf * lax.rsqrt(jnp.mean(xf * xf, axis=-1, keepdims=True) + EPS)
    return (y * g.astype(F32)).astype(x.dtype)


def _rope(x, pos):
    half = x.shape[-1] // 2
    freqs = jnp.power(jnp.float32(ROPE_THETA), -jnp.arange(half, dtype=F32) / half)
    ang = pos.astype(F32)[:, None] * freqs[None, :]
    cos = jnp.cos(ang)[None, :, None, :]
    sin = jnp.sin(ang)[None, :, None, :]
    xf = x.astype(F32)
    x1, x2 = xf[..., :half], xf[..., half:]
    return jnp.concatenate([x1 * cos - x2 * sin, x1 * sin + x2 * cos], axis=-1).astype(x.dtype)


def _to_blocks(a, nb):
    return jnp.moveaxis(a.reshape(a.shape[0], nb, a.shape[1] // nb, *a.shape[2:]), 1, 0)


def _from_blocks(a):
    a = jnp.moveaxis(a, 0, 1)
    return a.reshape(a.shape[0], a.shape[1] * a.shape[2], *a.shape[3:])


def _dsa_attend(q, qi, wi, k_all, v_all, ki_all, q_pos):
    n_q = q.shape[1]
    n_k = k_all.shape[1]
    top = min(DSA_TOPK, n_k // 4)
    blk = Q_BLOCK if n_q % Q_BLOCK == 0 else n_q
    nb = n_q // blk
    k_pos = jnp.arange(n_k, dtype=jnp.int32)
    scale = HEAD_DIM ** -0.5
    ki_f = ki_all.astype(F32)

    def block(args):
        qb, qib, wib, pb = args
        s = jax.nn.relu(jnp.einsum('bqhd,bld->bqhl', qib.astype(F32), ki_f))
        score = jnp.einsum('bqhl,bqh->bql', s, wib.astype(F32))
        score = jnp.where(k_pos[None, None, :] <= pb[None, :, None], score, -jnp.inf)
        _, sel = lax.top_k(score, top)
        valid = sel <= pb[None, :, None]
        kg = jax.vmap(lambda kk, ii: kk[ii])(k_all, sel)
        vg = jax.vmap(lambda vv, ii: vv[ii])(v_all, sel)
        logits = jnp.einsum('bqkgd,bqskd->bqkgs', qb.astype(F32), kg.astype(F32)) * scale
        logits = jnp.where(valid[:, :, None, None, :], logits, -jnp.inf)
        p = jax.nn.softmax(logits, axis=-1)
        return jnp.einsum('bqkgs,bqskd->bqkgd', p, vg.astype(F32)).astype(q.dtype)

    out = lax.map(block, (_to_blocks(q, nb), _to_blocks(qi, nb), _to_blocks(wi, nb), q_pos.reshape(nb, blk)))
    return _from_blocks(out)


def _moba_attend(q, k_all, v_all, q_pos):
    n_b, n_q = q.shape[:2]
    n_k = k_all.shape[1]
    n_blk = -(-n_k // MOBA_BLOCK)
    pad = n_blk * MOBA_BLOCK - n_k
    kb = jnp.pad(k_all, ((0, 0), (0, pad), (0, 0), (0, 0))).reshape(n_b, n_blk, MOBA_BLOCK, B_KV, HEAD_DIM)
    vb = jnp.pad(v_all, ((0, 0), (0, pad), (0, 0), (0, 0))).reshape(n_b, n_blk, MOBA_BLOCK, B_KV, HEAD_DIM)
    k_mean = jnp.mean(kb.astype(F32), axis=2)
    kb_h = jnp.moveaxis(kb, 3, 1)
    vb_h = jnp.moveaxis(vb, 3, 1)
    top = min(MOBA_TOPK, n_blk)
    blk = MOBA_Q_BLOCK if n_q % MOBA_Q_BLOCK == 0 else n_q
    nb = n_q // blk
    blk_ids = jnp.arange(n_blk, dtype=jnp.int32)
    in_blk = jnp.arange(MOBA_BLOCK, dtype=jnp.int32)
    b_ix = jnp.arange(n_b)
    kv_ix = jnp.arange(B_KV)
    scale = HEAD_DIM ** -0.5

    def block(args):
        qb, pb = args
        qf = qb.astype(F32)
        own = pb // MOBA_BLOCK
        gate = jnp.einsum('bqkgd,bnkd->bqkgn', qf, k_mean)
        past = blk_ids[None, :] < own[:, None]
        gate = jnp.where(past[None, :, None, None, :], gate, -jnp.inf)
        _, sel = lax.top_k(gate, top)
        sel_ok = sel < own[None, :, None, None, None]
        bi = b_ix[:, None, None, None, None]
        ki = kv_ix[None, None, :, None, None]
        kg = kb_h[bi, ki, sel]
        vg = vb_h[bi, ki, sel]
        l_sel = jnp.einsum('bqkgd,bqkgnsd->bqkgns', qf, kg.astype(F32)) * scale
        l_sel = jnp.where(sel_ok[..., None], l_sel, -jnp.inf).reshape(n_b, blk, B_KV, B_GROUP, top * MOBA_BLOCK)
        ko = kb[b_ix[:, None], own[None, :]]
        vo = vb[b_ix[:, None], own[None, :]]
        l_own = jnp.einsum('bqkgd,bqskd->bqkgs', qf, ko.astype(F32)) * scale
        own_pos = own[:, None] * MOBA_BLOCK + in_blk[None, :]
        l_own = jnp.where((own_pos <= pb[:, None])[None, :, None, None, :], l_own, -jnp.inf)
        p = jax.nn.softmax(jnp.concatenate([l_sel, l_own], axis=-1), axis=-1)
        p_sel = p[..., :top * MOBA_BLOCK].reshape(n_b, blk, B_KV, B_GROUP, top, MOBA_BLOCK)
        p_own = p[..., top * MOBA_BLOCK:]
        out = (jnp.einsum('bqkgns,bqkgnsd->bqkgd', p_sel, vg.astype(F32))
               + jnp.einsum('bqkgs,bqskd->bqkgd', p_own, vo.astype(F32)))
        return out.astype(q.dtype)

    out = lax.map(block, (_to_blocks(q, nb), q_pos.reshape(nb, blk)))
    return _from_blocks(out)


def _mlstm(q, k, v, i_pre, f_pre, C0, n0, m0):
    n_s = q.shape[1]
    L = C_CHUNK if n_s % C_CHUNK == 0 else n_s
    nc = n_s // L
    q = q.astype(F32)
    k = k.astype(F32) * (C_HD ** -0.5)
    v = v.astype(F32)
    i_pre = i_pre.astype(F32)
    logf = jax.nn.log_sigmoid(f_pre.astype(F32))
    tri = jnp.tril(jnp.ones((L, L), dtype=bool))

    def step(carry, xs):
        C, n, m = carry
        qc, kc, vc, ic, fc = xs
        b = jnp.moveaxis(jnp.cumsum(fc, axis=1), 1, 2)
        it = jnp.moveaxis(ic, 1, 2)
        dmat = jnp.where(tri, b[..., :, None] - b[..., None, :] + it[..., None, :], -jnp.inf)
        m_t = jnp.maximum(m[..., None] + b, jnp.max(dmat, axis=-1))
        inter = jnp.exp(m[..., None] + b - m_t)
        s = jnp.einsum('blhd,bshd->bhls', qc, kc) * jnp.exp(dmat - m_t[..., None])
        num = jnp.einsum('bhls,bshd->bhld', s, vc) + inter[..., None] * jnp.einsum('blhd,bhde->bhle', qc, C)
        den = jnp.sum(s, axis=-1) + inter * jnp.einsum('blhd,bhd->bhl', qc, n)
        h = num / jnp.maximum(jnp.abs(den), jnp.exp(-m_t))[..., None]
        m_new = m_t[..., -1]
        decay = jnp.exp(m + b[..., -1] - m_new)
        wk = jnp.exp(b[..., -1:] - b + it - m_new[..., None])
        C_new = decay[..., None, None] * C + jnp.einsum('bhs,bshd,bshe->bhde', wk, kc, vc)
        n_new = decay[..., None] * n + jnp.einsum('bhs,bshd->bhd', wk, kc)
        return (C_new, n_new, m_new), jnp.moveaxis(h, 1, 2)

    xs = (_to_blocks(q, nc), _to_blocks(k, nc), _to_blocks(v, nc), _to_blocks(i_pre, nc), _to_blocks(logf, nc))
    (C, n, m), h = lax.scan(step, (C0.astype(F32), n0.astype(F32), m0.astype(F32)), xs)
    return _from_blocks(h), C, n, m


def _peer(h_in, g_ffn, w_q, subkeys, u_tab, v_tab):
    n_b, n_s, d = h_in.shape
    hq = _mm3(h_in, w_q, g_ffn)
    hn = _rms(h_in, g_ffn)
    hf = hn.reshape(n_b * n_s, d)
    qf = hq.reshape(n_b * n_s, -1)
    n_tok = hf.shape[0]
    blk = min(TOKEN_BLOCK, n_tok)
    nb = n_tok // blk
    half = PEER_QDIM // 2
    sk = subkeys.astype(F32)

    def block(args):
        hb, qb = args
        q = qb.reshape(blk, PEER_HEADS, 2, half).astype(F32)
        s = jnp.einsum('nhpd,hpkd->nhpk', q, sk)
        sv, si = lax.top_k(s, PEER_TOPK)
        cand = (sv[:, :, 0, :, None] + sv[:, :, 1, None, :]).reshape(blk, PEER_HEADS, PEER_TOPK * PEER_TOPK)
        cid = (si[:, :, 0, :, None] * PEER_NKEYS + si[:, :, 1, None, :]).reshape(blk, PEER_HEADS, PEER_TOPK * PEER_TOPK)
        tv, ti = lax.top_k(cand, PEER_TOPK)
        eid = jnp.take_along_axis(cid, ti, axis=-1)
        g = jax.nn.softmax(tv, axis=-1)
        u = u_tab[eid].astype(F32)
        a = jax.nn.gelu(jnp.einsum('nd,nhkd->nhk', hb.astype(F32), u))
        out = jnp.einsum('nhk,nhkd->nd', g * a, v_tab[eid].astype(F32))
        return out.astype(h_in.dtype)

    out = lax.map(block, (hf.reshape(nb, blk, d), qf.reshape(nb, blk, -1))).reshape(nb * blk, d)
    return out.reshape(n_b, n_s, d)


def _heads_t(a):
    return jnp.transpose(a.astype(BF16), (0, 2, 3, 1))


def _values_t(v, tk):
    n_b, n_s, n_kv, d = v.shape
    vt = jnp.transpose(v.astype(BF16), (0, 2, 3, 1)).reshape(n_b, n_kv, d, n_s // tk, tk)
    return jnp.transpose(vt, (0, 1, 3, 2, 4))


def _dsa_prompt_glue(aq, aiq, aiw, ak, av, aik):
    n_b, n_s = aq.shape[:2]
    scale = HEAD_DIM ** -0.5
    qt = _heads_t(aq.reshape(n_b, n_s, A_HEADS, HEAD_DIM) * scale)
    ot = dsa_prompt(_heads_t(aiq), jnp.transpose(aiw, (0, 2, 1)), aik.astype(BF16), qt,
                    jnp.transpose(ak.astype(BF16), (0, 2, 1, 3)), _values_t(av, ATT_TK),
                    top=min(DSA_TOPK, n_s // 4))
    return jnp.transpose(ot, (0, 3, 1, 2)).reshape(n_b, n_s, A_WIDTH)


def _moba_prompt_glue(bq, bk, bv):
    n_b, n_s = bq.shape[:2]
    scale = HEAD_DIM ** -0.5
    qt = _heads_t(bq.reshape(n_b, n_s, B_HEADS, HEAD_DIM) * scale)
    ot = moba_prompt(qt, jnp.transpose(bk.astype(BF16), (0, 2, 1, 3)), _values_t(bv, MOBA_BLOCK))
    return jnp.transpose(ot, (0, 3, 1, 2)).reshape(n_b, n_s, B_WIDTH)


def _layer(x, p_l, pos, lw, past):
    n_b, n_s, _ = x.shape
    z = _mm3(x, lw['w_in'], lw['g_mix'])
    (aq, ak, av, aiq, aiw, aik, bq, bk, bv, cq, ck, cv, co, ci, cf, gate) = jnp.split(z, IN_SPLITS, axis=-1)
    aq = _rope(_rms(aq.reshape(n_b, n_s, A_HEADS, HEAD_DIM), lw['g_qa']), pos).reshape(n_b, n_s, A_KV, A_GROUP, HEAD_DIM)
    ak = _rope(_rms(ak.reshape(n_b, n_s, A_KV, HEAD_DIM), lw['g_ka']), pos)
    av = av.reshape(n_b, n_s, A_KV, HEAD_DIM)
    aiq = _rope(aiq.reshape(n_b, n_s, IDX_HEADS, IDX_DIM), pos)
    aik = _rope(aik.reshape(n_b, n_s, 1, IDX_DIM), pos)[:, :, 0]
    aiw = aiw * IDX_W_SCALE
    bq = _rope(_rms(bq.reshape(n_b, n_s, B_HEADS, HEAD_DIM), lw['g_qb']), pos).reshape(n_b, n_s, B_KV, B_GROUP, HEAD_DIM)
    bk = _rope(_rms(bk.reshape(n_b, n_s, B_KV, HEAD_DIM), lw['g_kb']), pos)
    bv = bv.reshape(n_b, n_s, B_KV, HEAD_DIM)
    cq = cq.reshape(n_b, n_s, C_HEADS, C_HD)
    ck = ck.reshape(n_b, n_s, C_HEADS, C_HD)
    cv = cv.reshape(n_b, n_s, C_HEADS, C_HD)
    ci = ci + lw['b_if'][:C_HEADS]
    cf = cf + lw['b_if'][C_HEADS:]
    if past is None:
        a_k_all, a_v_all, a_i_all, b_k_all, b_v_all = ak, av, aik, bk, bv
        C0 = jnp.zeros((n_b, C_HEADS, C_HD, C_HD), F32)
        n0 = jnp.zeros((n_b, C_HEADS, C_HD), F32)
        m0 = jnp.zeros((n_b, C_HEADS), F32)
    else:
        a_k_all = jnp.concatenate([past['a_k'], ak], axis=1)
        a_v_all = jnp.concatenate([past['a_v'], av], axis=1)
        a_i_all = jnp.concatenate([past['a_idx'], aik], axis=1)
        b_k_all = jnp.concatenate([past['b_k'], bk], axis=1)
        b_v_all = jnp.concatenate([past['b_v'], bv], axis=1)
        C0, n0, m0 = past['c_C'], past['c_n'], past['c_m']
    if past is None:
        oa = _dsa_prompt_glue(aq, aiq, aiw, ak, av, aik)
        ob = _moba_prompt_glue(bq, bk, bv)
    else:
        oa = _dsa_attend(aq, aiq, aiw, a_k_all, a_v_all, a_i_all, pos).reshape(n_b, n_s, A_WIDTH)
        ob = _moba_attend(bq, b_k_all, b_v_all, pos).reshape(n_b, n_s, B_WIDTH)
    hc, C1, n1, m1 = _mlstm(cq, ck, cv, ci, cf, C0, n0, m0)
    oc = (_rms(hc.astype(x.dtype), lw['g_c_out'].reshape(C_HEADS, C_HD))
          * jax.nn.sigmoid(co.reshape(n_b, n_s, C_HEADS, C_HD))).reshape(n_b, n_s, C_WIDTH)
    g_a, g_b, g_c = jnp.split(gate, N_BRANCH, axis=-1)
    mix = (jax.nn.sigmoid(g_a) * _mm3(oa, lw['w_br_a'])
           + jax.nn.sigmoid(g_b) * _mm3(ob, lw['w_br_b'])
           + jax.nn.sigmoid(g_c) * _mm3(oc, lw['w_br_c']))
    x = x + _mm3(mix, lw['w_out'])
    x = x + peer(x.reshape(n_b * n_s, -1), lw['g_ffn'], lw['w_peer_q'], lw['peer_subkeys'],
                 lw['peer_u'], lw['peer_v']).reshape(x.shape)
    x = x + jax.nn.sigmoid(_mm3(x, lw['w_ple_gate'])) * _mm3(p_l, lw['w_ple'])
    return x, (ak, av, aik, bk, bv, C1, n1, m1)


def kernel(x_prompt, x_sample, p_prompt, p_sample, cache_a_k, cache_a_v, cache_a_idx, cache_b_k, cache_b_v,
           state_c_C, state_c_n, state_c_m, page_table, g_mix, w_in, b_if, g_qa, g_ka, g_qb, g_kb, g_c_out,
           w_br_a, w_br_b, w_br_c, w_out, g_ffn, w_peer_q, peer_subkeys, peer_u, peer_v, w_ple, w_ple_gate):
    n_dec = x_sample.shape[0]
    past_len = page_table.shape[1] * PAGE_SIZE
    pos_p = jnp.arange(x_prompt.shape[1], dtype=jnp.int32)
    pos_s = past_len + jnp.arange(x_sample.shape[1], dtype=jnp.int32)

    def paged(cache_l):
        rows = cache_l[page_table]
        return rows.reshape(n_dec, past_len, *cache_l.shape[2:])

    yp, ys = x_prompt, x_sample
    states_p, states_s = [], []
    for l in range(DEPTH):
        lw = {'g_mix': g_mix[l], 'w_in': w_in[l], 'b_if': b_if[l], 'g_qa': g_qa[l], 'g_ka': g_ka[l],
              'g_qb': g_qb[l], 'g_kb': g_kb[l], 'g_c_out': g_c_out[l], 'w_br_a': w_br_a[l], 'w_br_b': w_br_b[l],
              'w_br_c': w_br_c[l], 'w_out': w_out[l], 'g_ffn': g_ffn[l], 'w_peer_q': w_peer_q[l],
              'peer_subkeys': peer_subkeys[l], 'peer_u': peer_u[l], 'peer_v': peer_v[l], 'w_ple': w_ple[l],
              'w_ple_gate': w_ple_gate[l]}
        yp, st_p = _layer(yp, p_prompt[l], pos_p, lw, None)
        past = {'a_k': paged(cache_a_k[l]), 'a_v': paged(cache_a_v[l]), 'a_idx': paged(cache_a_idx[l]),
                'b_k': paged(cache_b_k[l]), 'b_v': paged(cache_b_v[l]),
                'c_C': state_c_C[l], 'c_n': state_c_n[l], 'c_m': state_c_m[l]}
        ys, st_s = _layer(ys, p_sample[l], pos_s, lw, past)
        states_p.append(st_p)
        states_s.append(st_s)
    (pa_k, pa_v, pa_i, pb_k, pb_v, pc_C, pc_n, pc_m) = [jnp.stack(a) for a in zip(*states_p)]
    (sa_k, sa_v, sa_i, sb_k, sb_v, sc_C, sc_n, sc_m) = [jnp.stack(a) for a in zip(*states_s)]
    return (yp, ys, pa_k, pa_v, pa_i, pb_k, pb_v, pc_C, pc_n, pc_m,
            sa_k, sa_v, sa_i, sb_k, sb_v, sc_C, sc_n, sc_m)
```

```python
import functools
import math

import jax
import jax.numpy as jnp
import numpy as np
from jax import lax
from jax.experimental import pallas as pl
from jax.experimental.pallas import tpu as pltpu

D_MODEL = 1024
DEPTH = 4
PAGE_SIZE = 128
HEAD_DIM = 64
A_HEADS = 8
A_KV = 4
A_GROUP = A_HEADS // A_KV
A_WIDTH = A_HEADS * HEAD_DIM
IDX_HEADS = 8
IDX_DIM = 64
IDX_W_SCALE = (IDX_HEADS ** -0.5) * (IDX_DIM ** -0.5)
DSA_TOPK = 256
B_HEADS = 8
B_KV = 4
B_GROUP = B_HEADS // B_KV
B_WIDTH = B_HEADS * HEAD_DIM
MOBA_BLOCK = 256
MOBA_TOPK = 3
C_HEADS = 4
C_HD = 128
C_WIDTH = C_HEADS * C_HD
C_CHUNK = 64
N_BRANCH = 3
PEER_HEADS = 8
PEER_NKEYS = 128
PEER_QDIM = 256
PEER_TOPK = 16
PLE_DIM = 256
ROPE_THETA = 10000.0
EPS = 1e-6
Q_BLOCK = 128
MOBA_Q_BLOCK = 32
TOKEN_BLOCK = 256

IN_SIZES = (A_WIDTH, A_KV * HEAD_DIM, A_KV * HEAD_DIM, IDX_HEADS * IDX_DIM, IDX_HEADS, IDX_DIM,
            B_WIDTH, B_KV * HEAD_DIM, B_KV * HEAD_DIM,
            C_WIDTH, C_WIDTH, C_WIDTH, C_WIDTH, C_HEADS, C_HEADS,
            N_BRANCH * D_MODEL)
IN_SPLITS = tuple(int(s) for s in np.cumsum(IN_SIZES)[:-1])

F32 = jnp.float32
BF16 = jnp.bfloat16

LANES = 128
SUBLANES = 8
VMEM_LIMIT = 48 * 1024 * 1024


def _mm_kernel(x_ref, g_ref, w_ref, o_ref, h_ref, *, normalize):
    @pl.when(pl.program_id(1) == 0)
    def _():
        x = x_ref[...]
        if normalize:
            x = x * lax.rsqrt(jnp.mean(x * x, axis=-1, keepdims=True) + EPS) * g_ref[...]
        h_ref[...] = x.astype(BF16)

    o_ref[...] = jnp.dot(h_ref[...], w_ref[...], preferred_element_type=F32)


def _pick_tile(n, cands):
    for c in cands:
        if n % c == 0:
            return c
    return n


def matmul(x, w, g=None):
    m, k = x.shape
    n = w.shape[1]
    n_pad = -(-n // LANES) * LANES
    wb = w.astype(BF16)
    if n_pad != n:
        wb = jnp.pad(wb, ((0, 0), (0, n_pad - n)))
    tm = _pick_tile(m, (1024, 512, 256, 128))
    tn = _pick_tile(n_pad, (512, 256, 128))
    gg = (jnp.ones((k,), F32) if g is None else g.astype(F32)).reshape(1, k)
    out = pl.pallas_call(
        functools.partial(_mm_kernel, normalize=g is not None),
        grid=(m // tm, n_pad // tn),
        in_specs=[pl.BlockSpec((tm, k), lambda i, j: (i, 0)),
                  pl.BlockSpec((1, k), lambda i, j: (0, 0)),
                  pl.BlockSpec((k, tn), lambda i, j: (0, j))],
        out_specs=pl.BlockSpec((tm, tn), lambda i, j: (i, j)),
        out_shape=jax.ShapeDtypeStruct((m, n_pad), F32),
        scratch_shapes=[pltpu.VMEM((tm, k), BF16)],
        compiler_params=pltpu.CompilerParams(dimension_semantics=("arbitrary", "arbitrary"),
                                             vmem_limit_bytes=VMEM_LIMIT),
        name="matmul",
    )(x, gg, wb)
    return out[:, :n] if n_pad != n else out


def _mm3(x, w, g=None):
    b, s, k = x.shape
    return matmul(x.reshape(b * s, k), w, g).reshape(b, s, w.shape[1])


PROJ_TM = 512
HALF = HEAD_DIM // 2
T_ROWS = (A_WIDTH, IDX_HEADS * IDX_DIM, B_WIDTH, A_KV * HEAD_DIM, B_KV * HEAD_DIM, C_WIDTH, 2 * SUBLANES)
T_OFFS = tuple(int(v) for v in np.cumsum((0,) + T_ROWS))


def _rms_rows(x, g):
    return x * lax.rsqrt(jnp.mean(x * x, axis=0, keepdims=True) + EPS) * g


def _rope_rows(x, cos, sin):
    x1, x2 = x[:HALF], x[HALF:]
    return jnp.concatenate([x1 * cos - x2 * sin, x1 * sin + x2 * cos], axis=0)


def _proj_t_kernel(x_ref, g_ref, wt_ref, cos_ref, sin_ref, gq_ref, bias_ref,
                   qa_ref, qi_ref, wi_ref, qb_ref, va_ref, vb_ref, kc_ref, if_ref):
    x = x_ref[0]
    hb = (x * lax.rsqrt(jnp.mean(x * x, axis=-1, keepdims=True) + EPS) * g_ref[...]).astype(BF16)
    tm = hb.shape[0]
    cos, sin = cos_ref[...], sin_ref[...]
    scale = HEAD_DIM ** -0.5

    def seg(k):
        return lax.dot_general(wt_ref[T_OFFS[k]:T_OFFS[k + 1], :], hb, (((1,), (1,)), ((), ())),
                               preferred_element_type=F32)

    for k, (out_ref, gain) in enumerate(((qa_ref, 0), (qi_ref, None), (qb_ref, 1))):
        z = seg((0, 1, 2)[k])
        for h in range(A_HEADS):
            xh = z[h * HEAD_DIM:(h + 1) * HEAD_DIM]
            if gain is not None:
                xh = _rms_rows(xh, gq_ref[gain]) * scale
            out_ref[0, h] = _rope_rows(xh, cos, sin).astype(BF16)
    for k, out_ref in ((3, va_ref), (4, vb_ref)):
        z = seg(k).astype(BF16)
        for kv in range(A_KV):
            for j in range(tm // ATT_TK):
                out_ref[0, kv, j] = z[kv * HEAD_DIM:(kv + 1) * HEAD_DIM, j * ATT_TK:(j + 1) * ATT_TK]
    kc_ref[0] = seg(5) * (C_HD ** -0.5)
    small = seg(6)
    wi_ref[0] = small[0:IDX_HEADS] * IDX_W_SCALE
    if_ref[0] = small[IDX_HEADS:] + bias_ref[...]


def _rope_lanes(x, cos, sin_lo, sin_hi):
    return x * cos + pltpu.roll(x, LANES - HALF, 1) * sin_lo + pltpu.roll(x, HALF, 1) * sin_hi


def _proj_kv_kernel(x_ref, g_ref, w_ref, cos_ref, sinlo_ref, sinhi_ref, gk_ref, bd_ref,
                    ka_ref, va_ref, ki_ref, kb_ref, vb_ref, kah_ref, kih_ref, kbh_ref):
    x = x_ref[0]
    hb = (x * lax.rsqrt(jnp.mean(x * x, axis=-1, keepdims=True) + EPS) * g_ref[...]).astype(BF16)
    cos, sin_lo, sin_hi = cos_ref[...], sinlo_ref[...], sinhi_ref[...]
    width = A_KV * HEAD_DIM

    def seg(c0, n):
        return jnp.dot(hb, w_ref[:, c0:c0 + n], preferred_element_type=F32)

    def rope(z):
        return jnp.concatenate([_rope_lanes(z[:, j * LANES:(j + 1) * LANES], cos, sin_lo, sin_hi)
                                for j in range(z.shape[1] // LANES)], axis=1)

    for k, (f_ref, h_ref) in enumerate(((ka_ref, kah_ref), (kb_ref, kbh_ref))):
        z = seg(k * width, width)
        ms = jnp.dot(z * z, bd_ref[...], precision=HIGHEST, preferred_element_type=F32)
        z = rope(z * lax.rsqrt(ms + EPS) * gk_ref[k:k + 1, :])
        f_ref[0] = z
        zb = z.astype(BF16)
        for kv in range(A_KV):
            h_ref[0, kv] = zb[:, kv * HEAD_DIM:(kv + 1) * HEAD_DIM]
    va_ref[0] = seg(2 * width, width)
    vb_ref[0] = seg(3 * width, width)
    zi = rope(seg(4 * width, LANES))[:, :IDX_DIM]
    ki_ref[0] = zi
    kih_ref[0] = zi.astype(BF16)


def _rope_tables(n_s):
    freqs = jnp.power(jnp.float32(ROPE_THETA), -jnp.arange(HALF, dtype=F32) / HALF)
    ang = jnp.arange(n_s, dtype=F32)[:, None] * freqs[None, :]
    return jnp.cos(ang), jnp.sin(ang)


def prompt_projection(x, lw):
    n_b, n_s, d = x.shape
    tm = PROJ_TM
    n_t = n_s // tm
    w = lw['w_in']
    cols = dict(zip(('aq', 'ak', 'av', 'aiq', 'aiw', 'aik', 'bq', 'bk', 'bv', 'cq', 'ck', 'cv', 'co', 'ci', 'cf',
                     'gate'), jnp.split(w, IN_SPLITS, axis=1)))
    g = lw['g_mix'].astype(F32).reshape(1, d)
    cos, sin = _rope_tables(n_s)
    x_spec = pl.BlockSpec((1, tm, d), lambda b, i: (b, i, 0))
    g_spec = pl.BlockSpec((1, d), lambda b, i: (0, 0))
    params = pltpu.CompilerParams(dimension_semantics=("arbitrary", "arbitrary"), vmem_limit_bytes=VMEM_LIMIT)

    wt = jnp.concatenate([cols[k] for k in ('aq', 'aiq', 'bq', 'av', 'bv', 'ck', 'aiw', 'ci', 'cf')],
                         axis=1).T.astype(BF16)
    gq = jnp.stack([jnp.broadcast_to(lw[k].astype(F32)[:, None], (HEAD_DIM, tm)) for k in ('g_qa', 'g_qb')])
    bias = jnp.broadcast_to(lw['b_if'].astype(F32)[:, None], (2 * C_HEADS, tm))
    heads_t = lambda: pl.BlockSpec((1, A_HEADS, HEAD_DIM, tm), lambda b, i: (b, 0, 0, i))
    vals_t = lambda: pl.BlockSpec((1, A_KV, tm // ATT_TK, HEAD_DIM, ATT_TK), lambda b, i: (b, 0, i, 0, 0))
    rows_t = lambda n: pl.BlockSpec((1, n, tm), lambda b, i: (b, 0, i))
    tab_t = pl.BlockSpec((HALF, tm), lambda b, i: (0, i))
    const = lambda shape: pl.BlockSpec(shape, lambda b, i: (0,) * len(shape))
    heads_shape = jax.ShapeDtypeStruct((n_b, A_HEADS, HEAD_DIM, n_s), BF16)
    vals_shape = jax.ShapeDtypeStruct((n_b, A_KV, n_s // ATT_TK, HEAD_DIM, ATT_TK), BF16)
    qa, qi, wi, qb, va_t, vb_t, kc_t, gates = pl.pallas_call(
        _proj_t_kernel,
        grid=(n_b, n_t),
        in_specs=[x_spec, g_spec, const(wt.shape), tab_t, tab_t, const(gq.shape), const(bias.shape)],
        out_specs=[heads_t(), heads_t(), rows_t(IDX_HEADS), heads_t(), vals_t(), vals_t(),
                   rows_t(C_WIDTH), rows_t(2 * C_HEADS)],
        out_shape=[heads_shape, heads_shape, jax.ShapeDtypeStruct((n_b, IDX_HEADS, n_s), F32), heads_shape,
                   vals_shape, vals_shape, jax.ShapeDtypeStruct((n_b, C_WIDTH, n_s), F32),
                   jax.ShapeDtypeStruct((n_b, 2 * C_HEADS, n_s), F32)],
        compiler_params=params,
        name="proj_t",
    )(x, g, wt, cos.T, sin.T, gq, bias)

    width = A_KV * HEAD_DIM
    wkv = jnp.concatenate([cols['ak'], cols['bk'], cols['av'], cols['bv'], cols['aik'],
                           jnp.zeros((d, LANES - IDX_DIM), w.dtype)], axis=1).astype(BF16)
    lane = jnp.arange(LANES)
    cos_l = jnp.tile(cos, (1, LANES // HALF))
    sin_l = jnp.tile(sin, (1, LANES // HALF))
    sin_lo = jnp.where((lane % HEAD_DIM) < HALF, -sin_l, 0.0)
    sin_hi = jnp.where((lane % HEAD_DIM) >= HALF, sin_l, 0.0)
    gk = jnp.stack([jnp.tile(lw[k].astype(F32), A_KV) for k in ('g_ka', 'g_kb')])
    head_of = jnp.arange(width) // HEAD_DIM
    bd = jnp.where(head_of[:, None] == head_of[None, :], 1.0 / HEAD_DIM, 0.0).astype(F32)
    tab = pl.BlockSpec((tm, LANES), lambda b, i: (i, 0))
    tok = lambda n: pl.BlockSpec((1, tm, n), lambda b, i: (b, i, 0))
    heads = lambda: pl.BlockSpec((1, A_KV, tm, HEAD_DIM), lambda b, i: (b, 0, i, 0))
    tok_shape = lambda n, dt=F32: jax.ShapeDtypeStruct((n_b, n_s, n), dt)
    heads_kv = jax.ShapeDtypeStruct((n_b, A_KV, n_s, HEAD_DIM), BF16)
    ka, va, ki, kb, vb, ka_h, ki_h, kb_h = pl.pallas_call(
        _proj_kv_kernel,
        grid=(n_b, n_t),
        in_specs=[x_spec, g_spec, const(wkv.shape), tab, tab, tab, const(gk.shape), const(bd.shape)],
        out_specs=[tok(width), tok(width), tok(IDX_DIM), tok(width), tok(width), heads(), tok(IDX_DIM), heads()],
        out_shape=[tok_shape(width), tok_shape(width), tok_shape(IDX_DIM), tok_shape(width), tok_shape(width),
                   heads_kv, tok_shape(IDX_DIM, BF16), heads_kv],
        compiler_params=params,
        name="proj_kv",
    )(x, g, wkv, cos_l, sin_lo, sin_hi, gk, bd)

    wc = jnp.concatenate([cols['cq'], cols['cv'], cols['co'], cols['gate']], axis=1)
    zc = matmul(x.reshape(n_b * n_s, d), wc, lw['g_mix']).reshape(n_b, n_s, -1)
    return dict(qa=qa, qi=qi, wi=wi, qb=qb, va_t=va_t, vb_t=vb_t, kc_t=kc_t, gates=gates,
                ka=ka, va=va, ki=ki, kb=kb, vb=vb, ka_h=ka_h, ki_h=ki_h, kb_h=kb_h, zc=zc)


INT_MIN = -2 ** 31
INT_MAX = 2 ** 31 - 1
NEG_BIG = -1e30
ATT_TQ = 128
ATT_TK = 256
BISECT_VALUE_STEPS = 12
BISECT_INT_STEPS = 33


def _col_sum_i32(mask):
    rows, tq = mask.shape
    return jnp.sum(mask.astype(jnp.int32).reshape(rows // SUBLANES, SUBLANES, tq), axis=0)


def _dsa_prompt_kernel(qit_ref, wt_ref, ki_ref, qt_ref, k_ref, vt_ref, o_ref,
                       key_ref, j_ref, m_ref, l_ref, acc_ref, *, top):
    tq, tk = ATT_TQ, ATT_TK
    i = pl.program_id(1)
    n_chunks = ((i + 1) * tq + tk - 1) // tk
    qpos = i * tq + lax.broadcasted_iota(jnp.int32, (1, tq), 1)
    row = lax.broadcasted_iota(jnp.int32, (tk, tq), 0)

    def score_chunk(c, carry):
        start = pl.multiple_of(c * tk, tk)
        kic = ki_ref[0, pl.ds(start, tk), :]
        s = jnp.zeros((tk, tq), F32)
        for h in range(IDX_HEADS):
            d = jnp.dot(kic, qit_ref[0, h], preferred_element_type=F32)
            s = s + wt_ref[0, h:h + 1, :] * jnp.maximum(d, 0.0)
        s = jnp.where(s == 0.0, 0.0, s)
        bits = pltpu.bitcast(s, jnp.int32)
        key = jnp.where(bits < 0, bits ^ INT_MAX, bits)
        key = jnp.where(start + row <= qpos, key, INT_MIN)
        key_ref[pl.ds(start, tk), :] = key
        return carry

    lax.fori_loop(0, n_chunks, score_chunk, 0)

    def count(pred):
        def body(c, acc):
            start = pl.multiple_of(c * tk, tk)
            return acc + _col_sum_i32(pred(key_ref[pl.ds(start, tk), :], start + row))
        acc = lax.fori_loop(0, n_chunks, body, jnp.zeros((SUBLANES, tq), jnp.int32))
        return jnp.sum(acc, axis=0, keepdims=True)

    def bis_cond(st):
        it, lo, hi, _ = st
        return ((it < BISECT_VALUE_STEPS + BISECT_INT_STEPS)
                & (jnp.max(jnp.where(hi != lo + 1, 1, 0)) > 0))

    def key_value(k):
        return pltpu.bitcast(jnp.where(k < 0, k ^ INT_MAX, k), F32)

    def bis_body(st):
        it, lo, hi, cnt_lo = st
        mid = (lo >> 1) + (hi >> 1) + (lo & hi & 1)
        vbits = pltpu.bitcast(0.5 * (key_value(lo) + key_value(hi)), jnp.int32)
        vmid = jnp.where(vbits < 0, vbits ^ INT_MAX, vbits)
        vmid_cap = jnp.where(it < BISECT_VALUE_STEPS, INT_MAX, INT_MIN)
        mid = jnp.where((vmid > lo) & (vmid < jnp.minimum(hi, vmid_cap)), vmid, mid)
        c = count(lambda kc, _: kc >= mid)
        ge = c >= top
        lo2 = jnp.where(ge, mid, lo)
        hi2 = jnp.where(c == top, mid + 1, jnp.where(ge, hi, mid))
        return it + 1, lo2, hi2, jnp.where(ge, c, cnt_lo)

    def max_body(c, acc):
        kc = key_ref[pl.ds(pl.multiple_of(c * tk, tk), tk), :]
        return jnp.maximum(acc, jnp.max(kc.reshape(tk // SUBLANES, SUBLANES, tq), axis=0))

    kmax = jnp.max(lax.fori_loop(0, n_chunks, max_body, jnp.full((SUBLANES, tq), INT_MIN, jnp.int32)),
                   axis=0, keepdims=True)
    _, t, _, cnt_t = lax.while_loop(
        bis_cond, bis_body,
        (jnp.int32(0), jnp.full((1, tq), INT_MIN, jnp.int32), kmax + 1,
         jnp.zeros((1, tq), jnp.int32) + n_chunks * tk))

    need = (cnt_t > top) & (t > INT_MIN)
    j_ref[...] = jnp.full((1, tq), INT_MAX, jnp.int32)

    @pl.when(jnp.max(need.astype(jnp.int32)) > 0)
    def _():
        r = top - count(lambda kc, _: kc > t)

        def jb(_, st):
            lo, hi = st
            mid = (lo + hi) >> 1
            ok = count(lambda kc, kp: (kc == t) & (kp <= mid)) >= r
            return jnp.where(ok, lo, mid), jnp.where(ok, mid, hi)

        n_bits = int(math.ceil(math.log2(key_ref.shape[0]))) + 1
        _, hi = lax.fori_loop(0, n_bits, jb, (jnp.full((1, tq), -1, jnp.int32),
                                              jnp.zeros((1, tq), jnp.int32) + (n_chunks * tk - 1)))
        j_ref[...] = jnp.where(need, hi, INT_MAX)

    m_ref[...] = jnp.full(m_ref.shape, NEG_BIG, F32)
    l_ref[...] = jnp.zeros(l_ref.shape, F32)
    acc_ref[...] = jnp.zeros(acc_ref.shape, F32)
    jcut = j_ref[...]

    def att_chunk(c, carry):
        start = pl.multiple_of(c * tk, tk)
        kc = key_ref[pl.ds(start, tk), :]
        sel = (kc > INT_MIN) & ((kc > t) | ((kc == t) & (start + row <= jcut)))
        for h in range(A_HEADS):
            kv = h // A_GROUP
            s = jnp.dot(k_ref[0, kv, pl.ds(start, tk), :], qt_ref[0, h], preferred_element_type=F32)
            m_old = m_ref[h]
            m_new = jnp.maximum(m_old, jnp.max(jnp.where(sel, s, NEG_BIG), axis=0, keepdims=True))
            p = jnp.where(sel, jnp.exp(s - m_new), 0.0)
            alpha = jnp.exp(m_old - m_new)
            l_ref[h] = alpha * l_ref[h] + jnp.sum(p, axis=0, keepdims=True)
            acc_ref[h] = alpha * acc_ref[h] + jnp.dot(vt_ref[0, kv, c], p.astype(BF16),
                                                      preferred_element_type=F32)
            m_ref[h] = m_new
        return carry

    lax.fori_loop(0, n_chunks, att_chunk, 0)
    for h in range(A_HEADS):
        o_ref[0, h] = acc_ref[h] / l_ref[h]


def dsa_prompt(qit, wt, ki, qt, k, vt, top):
    n_b, _, _, n_s = qt.shape
    tq, tk = ATT_TQ, ATT_TK
    return pl.pallas_call(
        functools.partial(_dsa_prompt_kernel, top=top),
        grid=(n_b, n_s // tq),
        in_specs=[pl.BlockSpec((1, IDX_HEADS, IDX_DIM, tq), lambda b, i: (b, 0, 0, i)),
                  pl.BlockSpec((1, IDX_HEADS, tq), lambda b, i: (b, 0, i)),
                  pl.BlockSpec((1, n_s, IDX_DIM), lambda b, i: (b, 0, 0)),
                  pl.BlockSpec((1, A_HEADS, HEAD_DIM, tq), lambda b, i: (b, 0, 0, i)),
                  pl.BlockSpec((1, A_KV, n_s, HEAD_DIM), lambda b, i: (b, 0, 0, 0)),
                  pl.BlockSpec((1, A_KV, n_s // tk, HEAD_DIM, tk), lambda b, i: (b, 0, 0, 0, 0))],
        out_specs=pl.BlockSpec((1, A_HEADS, HEAD_DIM, tq), lambda b, i: (b, 0, 0, i)),
        out_shape=jax.ShapeDtypeStruct((n_b, A_HEADS, HEAD_DIM, n_s), F32),
        scratch_shapes=[pltpu.VMEM((n_s, tq), jnp.int32),
                        pltpu.VMEM((1, tq), jnp.int32),
                        pltpu.VMEM((A_HEADS, 1, tq), F32),
                        pltpu.VMEM((A_HEADS, 1, tq), F32),
                        pltpu.VMEM((A_HEADS, HEAD_DIM, tq), F32)],
        compiler_params=pltpu.CompilerParams(dimension_semantics=("arbitrary", "arbitrary"),
                                             vmem_limit_bytes=VMEM_LIMIT),
        name="dsa_prompt",
    )(qit, wt, ki, qt, k, vt)


def _moba_prompt_kernel(qt_ref, k_ref, vt_ref, avg_ref, o_ref, kmean_ref, selm_ref, m_ref, l_ref, acc_ref,
                        *, top):
    tq, tk = ATT_TQ, MOBA_BLOCK
    n_blk = avg_ref.shape[0]
    i = pl.program_id(1)
    n_own = (i * tq) // tk
    qpos = i * tq + lax.broadcasted_iota(jnp.int32, (1, tq), 1)
    row = lax.broadcasted_iota(jnp.int32, (tk, tq), 0)
    nrow = lax.broadcasted_iota(jnp.int32, (n_blk, tq), 0)

    @pl.when(i == 0)
    def _():
        for kv in range(B_KV):
            kmean_ref[kv] = jnp.dot(avg_ref[...], k_ref[0, kv], preferred_element_type=F32).astype(BF16)

    for h in range(B_HEADS):
        g = jnp.dot(kmean_ref[h // B_GROUP], qt_ref[0, h], preferred_element_type=F32)
        g = jnp.where(nrow < n_own, g, -jnp.inf)
        picked = jnp.zeros((n_blk, tq), jnp.bool_)
        for _ in range(top):
            mx = jnp.max(g, axis=0, keepdims=True)
            first = jnp.min(jnp.where(g == mx, nrow, n_blk), axis=0, keepdims=True)
            hit = nrow == first
            picked = picked | hit
            g = jnp.where(hit, -jnp.inf, g)
        selm_ref[h] = jnp.where(picked & (nrow < n_own), 1.0, 0.0)

    m_ref[...] = jnp.full(m_ref.shape, NEG_BIG, F32)
    l_ref[...] = jnp.zeros(l_ref.shape, F32)
    acc_ref[...] = jnp.zeros(acc_ref.shape, F32)

    def att_block(n, carry):
        start = pl.multiple_of(n * tk, tk)
        causal = start + row <= qpos
        own = (jnp.zeros((1, tq), jnp.int32) + n) == n_own
        for h in range(B_HEADS):
            kv = h // B_GROUP
            picked_n = jnp.max(jnp.where(nrow == n, selm_ref[h], 0.0), axis=0, keepdims=True) > 0.0
            sel = (own & causal) | picked_n
            s = jnp.dot(k_ref[0, kv, pl.ds(start, tk), :], qt_ref[0, h], preferred_element_type=F32)
            m_old = m_ref[h]
            m_new = jnp.maximum(m_old, jnp.max(jnp.where(sel, s, NEG_BIG), axis=0, keepdims=True))
            p = jnp.where(sel, jnp.exp(s - m_new), 0.0)
            alpha = jnp.exp(m_old - m_new)
            l_ref[h] = alpha * l_ref[h] + jnp.sum(p, axis=0, keepdims=True)
            acc_ref[h] = alpha * acc_ref[h] + jnp.dot(vt_ref[0, kv, n], p.astype(BF16),
                                                      preferred_element_type=F32)
            m_ref[h] = m_new
        return carry

    lax.fori_loop(0, n_own + 1, att_block, 0)
    for h in range(B_HEADS):
        o_ref[0, h] = acc_ref[h] / l_ref[h]


def moba_prompt(qt, k, vt):
    n_b, _, _, n_s = qt.shape
    tq, tk = ATT_TQ, MOBA_BLOCK
    n_blk = n_s // tk
    avg = (jnp.repeat(jnp.eye(n_blk, dtype=F32), tk, axis=1) / tk).astype(BF16)
    return pl.pallas_call(
        functools.partial(_moba_prompt_kernel, top=min(MOBA_TOPK, n_blk)),
        grid=(n_b, n_s // tq),
        in_specs=[pl.BlockSpec((1, B_HEADS, HEAD_DIM, tq), lambda b, i: (b, 0, 0, i)),
                  pl.BlockSpec((1, B_KV, n_s, HEAD_DIM), lambda b, i: (b, 0, 0, 0)),
                  pl.BlockSpec((1, B_KV, n_blk, HEAD_DIM, tk), lambda b, i: (b, 0, 0, 0, 0)),
                  pl.BlockSpec((n_blk, n_s), lambda b, i: (0, 0))],
        out_specs=pl.BlockSpec((1, B_HEADS, HEAD_DIM, tq), lambda b, i: (b, 0, 0, i)),
        out_shape=jax.ShapeDtypeStruct((n_b, B_HEADS, HEAD_DIM, n_s), F32),
        scratch_shapes=[pltpu.VMEM((B_KV, n_blk, HEAD_DIM), BF16),
                        pltpu.VMEM((B_HEADS, n_blk, tq), F32),
                        pltpu.VMEM((B_HEADS, 1, tq), F32),
                        pltpu.VMEM((B_HEADS, 1, tq), F32),
                        pltpu.VMEM((B_HEADS, HEAD_DIM, tq), F32)],
        compiler_params=pltpu.CompilerParams(dimension_semantics=("arbitrary", "arbitrary"),
                                             vmem_limit_bytes=VMEM_LIMIT),
        name="moba_prompt",
    )(qt, k, vt, avg)


PEER_TT = 256
PEER_TE = 1024
PEER_NSORT = PEER_TOPK + 1
PEER_SVROWS = 24


def _extract_sorted(s, n, emit):
    rows, tt = s.shape
    rowi = lax.broadcasted_iota(jnp.int32, (rows, tt), 0)
    for r in range(n):
        mx = jnp.max(s, axis=0, keepdims=True)
        emit(r, mx)
        if r + 1 < n:
            first = jnp.min(jnp.where(s == mx, rowi, rows), axis=0, keepdims=True)
            s = jnp.where(rowi == first, -jnp.inf, s)


def _peer_select_kernel(x_ref, g_ref, wqt_ref, sk_ref, hb_ref, s1_ref, e1_ref, thr_ref, e0_ref,
                        q_ref, s_ref, sv_ref, top_ref):
    half = PEER_QDIM // 2
    x = x_ref[...]
    hb = (x * lax.rsqrt(jnp.mean(x * x, axis=-1, keepdims=True) + EPS) * g_ref[...]).astype(BF16)
    hb_ref[...] = hb
    q_ref[...] = lax.dot_general(wqt_ref[...], hb, (((1,), (1,)), ((), ())),
                                 preferred_element_type=F32).astype(BF16)
    sv_ref[...] = jnp.full(sv_ref.shape, -jnp.inf, F32)

    def head(h, carry):
        for p in range(2):
            hp = h * 2 + p
            s = jnp.dot(sk_ref[hp], q_ref[pl.ds(pl.multiple_of(hp * half, half), half), :],
                        preferred_element_type=F32)
            s_ref[p] = s

            def emit(r, v, p=p):
                sv_ref[p, r:r + 1, :] = v
            _extract_sorted(s, PEER_NSORT, emit)
        sv0, sv1 = sv_ref[0], sv_ref[1]
        cand = jnp.concatenate(
            [sv0[0:1] + sv1] + [sv0[a:a + 1] + sv1[0:8] for a in range(1, 8)] + [sv0[8:24] + sv1[0:1]], axis=0)

        def emit_top(r, v):
            top_ref[r:r + 1, :] = v
        _extract_sorted(cand, PEER_NSORT, emit_top)
        tau = 0.5 * (top_ref[PEER_TOPK - 1:PEER_TOPK, :] + top_ref[PEER_TOPK:PEER_TOPK + 1, :])
        cmax = sv0[0:1] + sv1[0:1]
        z = jnp.sum(jnp.where(cand >= tau, jnp.exp(cand - cmax), 0.0), axis=0, keepdims=True)
        s0, s1 = s_ref[0], s_ref[1]
        s1_ref[h] = s1
        e1_ref[h] = jnp.exp(s1 - sv1[0:1]) / z
        thr_ref[h] = tau - s0
        e0_ref[h] = jnp.exp(s0 - sv0[0:1])
        return carry

    lax.fori_loop(0, PEER_HEADS, head, 0)


def _peer_apply_kernel(hb_ref, u_ref, vt_ref, s1_ref, e1_ref, thr_ref, e0_ref, o_ref, g_ref):
    e = pl.program_id(1)
    for il in range(PEER_TE // PEER_NKEYS):
        acc = jnp.zeros((PEER_NKEYS, hb_ref.shape[0]), F32)
        for h in range(PEER_HEADS):
            thr_row = thr_ref[h, 0, il:il + 1, :]
            e0_row = e0_ref[h, 0, il:il + 1, :]
            acc = acc + jnp.where(s1_ref[h] >= thr_row, e1_ref[h] * e0_row, 0.0)
        g_ref[il * PEER_NKEYS:(il + 1) * PEER_NKEYS, :] = acc
    a = lax.dot_general(u_ref[...], hb_ref[...], (((1,), (1,)), ((), ())), preferred_element_type=F32)
    ga = (g_ref[...] * jax.nn.gelu(a)).astype(BF16)
    contrib = jnp.dot(vt_ref[...], ga, preferred_element_type=F32)

    @pl.when(e == 0)
    def _():
        o_ref[...] = contrib

    @pl.when(e != 0)
    def _():
        o_ref[...] += contrib


def peer(x, g_ffn, w_q, subkeys, u_tab, v_tab):
    n_tok, d = x.shape
    n_exp = u_tab.shape[0]
    tt = _pick_tile(n_tok, (PEER_TT, 128))
    half = PEER_QDIM // 2
    wqt = w_q.T.astype(BF16)
    sk = subkeys.reshape(PEER_HEADS * 2, PEER_NKEYS, half).astype(BF16)
    stat = jax.ShapeDtypeStruct((PEER_HEADS, PEER_NKEYS, n_tok), F32)
    stat_spec = pl.BlockSpec((PEER_HEADS, PEER_NKEYS, tt), lambda t: (0, 0, t))
    hb, s1, e1, thr, e0 = pl.pallas_call(
        _peer_select_kernel,
        grid=(n_tok // tt,),
        in_specs=[pl.BlockSpec((tt, d), lambda t: (t, 0)),
                  pl.BlockSpec((1, d), lambda t: (0, 0)),
                  pl.BlockSpec((PEER_HEADS * PEER_QDIM, d), lambda t: (0, 0)),
                  pl.BlockSpec((PEER_HEADS * 2, PEER_NKEYS, half), lambda t: (0, 0, 0))],
        out_specs=[pl.BlockSpec((tt, d), lambda t: (t, 0)), stat_spec, stat_spec, stat_spec, stat_spec],
        out_shape=[jax.ShapeDtypeStruct((n_tok, d), BF16), stat, stat, stat, stat],
        scratch_shapes=[pltpu.VMEM((PEER_HEADS * PEER_QDIM, tt), BF16),
                        pltpu.VMEM((2, PEER_NKEYS, tt), F32),
                        pltpu.VMEM((2, PEER_SVROWS, tt), F32),
                        pltpu.VMEM((PEER_SVROWS, tt), F32)],
        compiler_params=pltpu.CompilerParams(dimension_semantics=("arbitrary",),
                                             vmem_limit_bytes=VMEM_LIMIT),
        name="peer_select",
    )(x, g_ffn.astype(F32).reshape(1, d), wqt, sk)

    n_il = PEER_TE // PEER_NKEYS
    thr4 = thr.reshape(PEER_HEADS, PEER_NKEYS // n_il, n_il, n_tok)
    e04 = e0.reshape(PEER_HEADS, PEER_NKEYS // n_il, n_il, n_tok)
    stat2 = pl.BlockSpec((PEER_HEADS, PEER_NKEYS, tt), lambda t, e: (0, 0, t))
    stat4 = pl.BlockSpec((PEER_HEADS, 1, n_il, tt), lambda t, e: (0, e, 0, t))
    out_t = pl.pallas_call(
        _peer_apply_kernel,
        grid=(n_tok // tt, n_exp // PEER_TE),
        in_specs=[pl.BlockSpec((tt, d), lambda t, e: (t, 0)),
                  pl.BlockSpec((PEER_TE, d), lambda t, e: (e, 0)),
                  pl.BlockSpec((d, PEER_TE), lambda t, e: (0, e)),
                  stat2, stat2, stat4, stat4],
        out_specs=pl.BlockSpec((d, tt), lambda t, e: (0, t)),
        out_shape=jax.ShapeDtypeStruct((d, n_tok), F32),
        scratch_shapes=[pltpu.VMEM((PEER_TE, tt), F32)],
        compiler_params=pltpu.CompilerParams(dimension_semantics=("arbitrary", "arbitrary"),
                                             vmem_limit_bytes=VMEM_LIMIT),
        name="peer_apply",
    )(hb, u_tab.astype(BF16), v_tab.T.astype(BF16), s1, e1, thr4, e04)
    return out_t.T


MLSTM_CHUNK = 128
HIGHEST = lax.Precision.HIGHEST


def _mlstm_kernel(q_ref, kt_ref, v_ref, co_ref, if_ref, gc_ref, s0_ref, m0_ref,
                  oc_ref, s_out_ref, m_out_ref, s_ref, m_ref, *, valid):
    L, D = MLSTM_CHUNK, C_HD
    c = pl.program_id(1)

    @pl.when(c == 0)
    def _():
        s_ref[...] = s0_ref[0]
        m_ref[...] = m0_ref[0]

    li = lax.broadcasted_iota(jnp.int32, (L, L), 0)
    si = lax.broadcasted_iota(jnp.int32, (L, L), 1)
    tri = si <= li
    upper = jnp.where(li <= si, 1.0, 0.0)
    ones_ll = jnp.ones((L, L), F32)
    live = c * L + lax.broadcasted_iota(jnp.int32, (1, L), 1) < valid
    ones_col = jnp.where(lax.broadcasted_iota(jnp.int32, (L, D), 1) == 0, 1.0, 0.0).astype(BF16)

    for h in range(C_HEADS):
        lanes = slice(h * D, (h + 1) * D)
        logf = jnp.where(live, jax.nn.log_sigmoid(if_ref[0, C_HEADS + h:C_HEADS + h + 1, :]), 0.0)
        i_row = jnp.where(live, if_ref[0, h:h + 1, :], -jnp.inf)
        f_b = jnp.broadcast_to(logf, (L, L))
        b_col_b = jnp.dot(jnp.where(tri, f_b, 0.0), ones_ll, precision=HIGHEST,
                          preferred_element_type=F32)
        b_row = jnp.dot(f_b[0:SUBLANES], upper, precision=HIGHEST,
                        preferred_element_type=F32)[0:1]
        b_col = b_col_b[:, 0:1]
        dmat = jnp.where(tri, b_col_b - b_row + i_row, -jnp.inf)
        m_prev = m_ref[h][0:1, 0:1]
        m_t = jnp.maximum(m_prev + b_col, jnp.max(dmat, axis=1, keepdims=True))
        inter = jnp.exp(m_prev + b_col - m_t)
        qb = q_ref[0, :, lanes].astype(BF16)
        kt = kt_ref[0, lanes, :]
        vaug = jnp.concatenate([v_ref[0, :, lanes].astype(BF16), ones_col], axis=1)
        s = jnp.dot(qb, kt.astype(BF16), preferred_element_type=F32) * jnp.exp(dmat - m_t)
        state = s_ref[h]
        tot = (jnp.dot(s.astype(BF16), vaug, preferred_element_type=F32)
               + inter * jnp.dot(qb, state.astype(BF16), preferred_element_type=F32))
        hh = tot[:, :D] / jnp.maximum(jnp.abs(tot[:, D:D + 1]), jnp.exp(-m_t))
        y = hh * lax.rsqrt(jnp.mean(hh * hh, axis=1, keepdims=True) + EPS) * gc_ref[0:1, lanes]
        oc_ref[0, :, lanes] = y * jax.nn.sigmoid(co_ref[0, :, lanes])
        m_new = m_t[L - 1:L, :]
        b_last = b_col[L - 1:L, :]
        wk_row = jnp.exp(b_last - b_row + i_row - m_new)
        s_ref[h] = (jnp.exp(m_prev + b_last - m_new) * state
                    + jnp.dot((kt * wk_row).astype(BF16), vaug, preferred_element_type=F32))
        m_ref[h] = jnp.broadcast_to(m_new, (SUBLANES, LANES))

    @pl.when(c == pl.num_programs(1) - 1)
    def _():
        s_out_ref[0] = s_ref[...]
        m_out_ref[0] = m_ref[...]


def mlstm(cq, ck, cv, co, ci, cf, g_c_out, C0, n0, m0):
    n_b, n_s, width = cq.shape
    L, D = MLSTM_CHUNK, C_HD
    n_c = -(-n_s // L)
    pad = n_c * L - n_s

    def padded(a):
        return jnp.pad(a, ((0, 0), (0, pad), (0, 0))) if pad else a

    kt = jnp.swapaxes(padded(ck) * (D ** -0.5), 1, 2)
    gates = jnp.swapaxes(padded(jnp.concatenate([ci, cf], axis=-1)), 1, 2)
    oc, C1, n1, m1 = mlstm_call((padded(cq), 0), kt, (padded(cv), 0), (padded(co), 0), gates,
                                g_c_out, C0, n0, m0, n_s)
    return oc[:, :n_s], C1, n1, m1


def mlstm_call(q_src, kt, v_src, co_src, gates, g_c_out, C0, n0, m0, n_s):
    n_b, width, s_pad = kt.shape
    L, D = MLSTM_CHUNK, C_HD
    n_c = s_pad // L
    s0 = jnp.concatenate([C0, n0[..., None], jnp.zeros((n_b, C_HEADS, D, D - 1), F32)], axis=-1)
    m0b = jnp.broadcast_to(m0[:, :, None, None], (n_b, C_HEADS, SUBLANES, LANES))

    def tok_spec(off):
        return pl.BlockSpec((1, L, width), lambda b, c: (b, c, off))

    tok = tok_spec(0)
    oc, s1, m1 = pl.pallas_call(
        functools.partial(_mlstm_kernel, valid=n_s),
        grid=(n_b, n_c),
        in_specs=[tok_spec(q_src[1]), pl.BlockSpec((1, width, L), lambda b, c: (b, 0, c)),
                  tok_spec(v_src[1]), tok_spec(co_src[1]),
                  pl.BlockSpec((1, 2 * C_HEADS, L), lambda b, c: (b, 0, c)),
                  pl.BlockSpec((1, width), lambda b, c: (0, 0)),
                  pl.BlockSpec((1, C_HEADS, D, 2 * D), lambda b, c: (b, 0, 0, 0)),
                  pl.BlockSpec((1, C_HEADS, SUBLANES, LANES), lambda b, c: (b, 0, 0, 0))],
        out_specs=[tok,
                   pl.BlockSpec((1, C_HEADS, D, 2 * D), lambda b, c: (b, 0, 0, 0)),
                   pl.BlockSpec((1, C_HEADS, SUBLANES, LANES), lambda b, c: (b, 0, 0, 0))],
        out_shape=[jax.ShapeDtypeStruct((n_b, n_c * L, width), F32),
                   jax.ShapeDtypeStruct((n_b, C_HEADS, D, 2 * D), F32),
                   jax.ShapeDtypeStruct((n_b, C_HEADS, SUBLANES, LANES), F32)],
        scratch_shapes=[pltpu.VMEM((C_HEADS, D, 2 * D), F32),
                        pltpu.VMEM((C_HEADS, SUBLANES, LANES), F32)],
        compiler_params=pltpu.CompilerParams(dimension_semantics=("arbitrary", "arbitrary"),
                                             vmem_limit_bytes=VMEM_LIMIT),
        name="mlstm",
    )(q_src[0], kt, v_src[0], co_src[0], gates, g_c_out.astype(F32).reshape(1, width), s0, m0b)
    return oc, s1[..., :D], s1[..., D], m1[:, :, 0, 0]


def _row_sum_i32(mask):
    return jnp.sum(mask.astype(jnp.int32), axis=1, keepdims=True)


def _masked_softmax_pv(logits, sel, v_pages):
    m = jnp.max(jnp.where(sel, logits, NEG_BIG), axis=1, keepdims=True)
    p = jnp.where(sel, jnp.exp(logits - m), 0.0)
    l = jnp.sum(p, axis=1, keepdims=True)
    pb = p.astype(BF16)
    acc = None
    for j, vp in enumerate(v_pages):
        part = jnp.dot(pb[:, j * PAGE_SIZE:(j + 1) * PAGE_SIZE], vp, preferred_element_type=F32)
        acc = part if acc is None else acc + part
    return acc / l


def _dsa_sample_kernel(pt_ref, *refs, n_pages, top, past_len):
    del pt_ref
    n_all = n_pages + 1
    idx_refs = refs[0:n_all]
    k_refs = refs[n_all:2 * n_all]
    v_refs = refs[2 * n_all:3 * n_all]
    qi_ref, w_ref, qbd_ref, o_ref = refs[3 * n_all:3 * n_all + 4]
    n_q = w_ref.shape[2]
    n_keys = n_all * PAGE_SIZE
    rows = IDX_HEADS * n_q

    w_b = jnp.broadcast_to(w_ref[0], (IDX_HEADS, n_q, PAGE_SIZE))
    qi = qi_ref[0]
    pieces = []
    for j in range(n_all):
        kip = idx_refs[j][...].reshape(PAGE_SIZE, IDX_DIM).astype(BF16)
        d = lax.dot_general(qi, kip, (((1,), (1,)), ((), ())), preferred_element_type=F32)
        pieces.append(jnp.sum(w_b * jnp.maximum(d, 0.0).reshape(IDX_HEADS, n_q, PAGE_SIZE), axis=0))
    s = jnp.concatenate(pieces, axis=1)
    s = jnp.where(s == 0.0, 0.0, s)
    bits = pltpu.bitcast(s, jnp.int32)
    kpos = lax.broadcasted_iota(jnp.int32, (n_q, n_keys), 1)
    qpos = past_len + lax.broadcasted_iota(jnp.int32, (n_q, n_keys), 0)
    key = jnp.where(kpos <= qpos, jnp.where(bits < 0, bits ^ INT_MAX, bits), INT_MIN)

    def bis_cond(st):
        it, lo, hi, _ = st
        return (it < BISECT_INT_STEPS) & (jnp.max(jnp.where(hi != lo + 1, 1, 0)) > 0)

    def bis_body(st):
        it, lo, hi, cnt_lo = st
        mid = (lo >> 1) + (hi >> 1) + (lo & hi & 1)
        c = _row_sum_i32(key >= mid)
        ge = c >= top
        return (it + 1, jnp.where(ge, mid, lo), jnp.where(c == top, mid + 1, jnp.where(ge, hi, mid)),
                jnp.where(ge, c, cnt_lo))

    _, t, _, cnt_t = lax.while_loop(
        bis_cond, bis_body,
        (jnp.int32(0), jnp.full((n_q, 1), INT_MIN, jnp.int32), jnp.full((n_q, 1), INT_MAX, jnp.int32),
         jnp.full((n_q, 1), n_keys, jnp.int32)))

    need = (cnt_t > top) & (t > INT_MIN)
    r = top - _row_sum_i32(key > t)

    def jb(_, st):
        lo, hi = st
        mid = (lo + hi) >> 1
        ok = _row_sum_i32((key == t) & (kpos <= mid)) >= r
        return jnp.where(ok, lo, mid), jnp.where(ok, mid, hi)

    n_bits = int(math.ceil(math.log2(n_keys))) + 1
    _, jhi = lax.fori_loop(0, n_bits, jb, (jnp.full((n_q, 1), -1, jnp.int32),
                                           jnp.full((n_q, 1), n_keys - 1, jnp.int32)))
    jcut = jnp.where(need, jhi, INT_MAX)
    sel = (key > INT_MIN) & ((key > t) | ((key == t) & (kpos <= jcut)))

    qbd = qbd_ref[0]
    logits = jnp.concatenate(
        [lax.dot_general(qbd, k_refs[j][...].reshape(PAGE_SIZE, A_KV * HEAD_DIM).astype(BF16),
                         (((1,), (1,)), ((), ())), preferred_element_type=F32) for j in range(n_all)], axis=1)
    sel_rows = jnp.broadcast_to(sel[None], (A_HEADS, n_q, n_keys)).reshape(rows, n_keys)
    v_pages = [v_refs[j][...].reshape(PAGE_SIZE, A_KV * HEAD_DIM).astype(BF16) for j in range(n_all)]
    o_ref[0] = _masked_softmax_pv(logits, sel_rows, v_pages)


def _moba_sample_kernel(pt_ref, *refs, n_pages):
    del pt_ref
    n_all = n_pages + 1
    k_refs = refs[0:n_all]
    v_refs = refs[n_all:2 * n_all]
    qbd_ref, o_ref = refs[2 * n_all:2 * n_all + 2]
    rows = qbd_ref.shape[1]
    n_q = rows // B_HEADS
    width = B_KV * HEAD_DIM
    pages_per_blk = MOBA_BLOCK // PAGE_SIZE
    n_blk = n_pages // pages_per_blk
    top = min(MOBA_TOPK, n_blk + 1)
    qbd = qbd_ref[0]

    k_pages = [k_refs[j][...].reshape(PAGE_SIZE, width) for j in range(n_all)]
    means = []
    for n in range(n_blk):
        tot = k_pages[n * pages_per_blk]
        for j in range(1, pages_per_blk):
            tot = tot + k_pages[n * pages_per_blk + j]
        means.append(jnp.sum(tot, axis=0, keepdims=True) * (1.0 / MOBA_BLOCK))
    kmean = jnp.concatenate(means + [jnp.zeros((LANES - n_blk, width), F32)], axis=0).astype(BF16)
    gate = lax.dot_general(qbd, kmean, (((1,), (1,)), ((), ())), preferred_element_type=F32)
    lane = lax.broadcasted_iota(jnp.int32, (rows, LANES), 1)
    gate = jnp.where(lane < n_blk, gate, -jnp.inf)
    picked = jnp.zeros((rows, LANES), jnp.bool_)
    for _ in range(top):
        mx = jnp.max(gate, axis=1, keepdims=True)
        first = jnp.min(jnp.where(gate == mx, lane, LANES), axis=1, keepdims=True)
        hit = lane == first
        picked = picked | hit
        gate = jnp.where(hit, -jnp.inf, gate)
    picked = jnp.where(picked & (lane < n_blk), 1.0, 0.0)

    kb = [kp.astype(BF16) for kp in k_pages]
    logits = jnp.concatenate(
        [lax.dot_general(qbd, kb[j], (((1,), (1,)), ((), ())), preferred_element_type=F32)
         for j in range(n_all)], axis=1)
    q_of_row = lax.broadcasted_iota(jnp.int32, (rows, PAGE_SIZE), 0) % n_q
    own = lax.broadcasted_iota(jnp.int32, (rows, PAGE_SIZE), 1) <= q_of_row
    sel = jnp.concatenate(
        [jnp.broadcast_to(picked[:, j // pages_per_blk:j // pages_per_blk + 1] > 0.0, (rows, PAGE_SIZE))
         for j in range(n_pages)] + [own], axis=1)
    v_pages = [v_refs[j][...].reshape(PAGE_SIZE, width).astype(BF16) for j in range(n_all)]
    o_ref[0] = _masked_softmax_pv(logits, sel, v_pages)


def _block_diag_queries(q):
    n_b, n_q, n_kv, n_g, hd = q.shape
    qh = jnp.transpose(q, (0, 2, 3, 1, 4))
    eye = jnp.eye(n_kv, dtype=q.dtype)
    bd = qh[:, :, :, :, None, :] * eye[None, :, None, None, :, None]
    return bd.reshape(n_b, n_kv * n_g * n_q, n_kv * hd).astype(BF16)


def _take_diag_heads(o, n_q, n_kv, n_g):
    n_b = o.shape[0]
    o6 = o.reshape(n_b, n_kv, n_g, n_q, n_kv, HEAD_DIM)
    d = jnp.stack([o6[:, kv, :, :, kv, :] for kv in range(n_kv)], axis=1)
    return jnp.transpose(d, (0, 3, 1, 2, 4)).reshape(n_b, n_q, n_kv * n_g * HEAD_DIM)


def _page_specs(layer, n_pages, tail_shape):
    def spec(p):
        return pl.BlockSpec((1, 1, PAGE_SIZE) + tail_shape,
                            lambda b, pt, p=p: (layer, pt[b, p]) + (0,) * (1 + len(tail_shape)))
    return [spec(p) for p in range(n_pages)]


def _new_page(a):
    return jnp.pad(a, ((0, 0), (0, PAGE_SIZE - a.shape[1]), (0, 0)))


def dsa_sample(layer, page_table, cache_idx, cache_k, cache_v, aq, aiq, aiw, ak, av, aik):
    n_b, n_q = aq.shape[:2]
    n_pages = page_table.shape[1]
    past_len = n_pages * PAGE_SIZE
    width = A_KV * HEAD_DIM
    ck = cache_k.reshape(cache_k.shape[:3] + (width,))
    cv = cache_v.reshape(cache_v.shape[:3] + (width,))
    qbd = _block_diag_queries(aq * (HEAD_DIM ** -0.5))
    qi = jnp.transpose(aiq, (0, 2, 1, 3)).reshape(n_b, IDX_HEADS * n_q, IDX_DIM).astype(BF16)
    w = jnp.transpose(aiw, (0, 2, 1))[..., None]
    new = lambda tail: pl.BlockSpec((1, PAGE_SIZE) + tail, lambda b, pt: (b, 0) + (0,) * len(tail))
    full = lambda shape: pl.BlockSpec((1,) + shape, lambda b, pt: (b,) + (0,) * len(shape))
    rows = A_HEADS * n_q
    out = pl.pallas_call(
        functools.partial(_dsa_sample_kernel, n_pages=n_pages, top=min(DSA_TOPK, (past_len + n_q) // 4),
                          past_len=past_len),
        grid_spec=pltpu.PrefetchScalarGridSpec(
            num_scalar_prefetch=1, grid=(n_b,),
            in_specs=(_page_specs(layer, n_pages, (IDX_DIM,)) + [new((IDX_DIM,))]
                      + _page_specs(layer, n_pages, (width,)) + [new((width,))]
                      + _page_specs(layer, n_pages, (width,)) + [new((width,))]
                      + [full((IDX_HEADS * n_q, IDX_DIM)), full((IDX_HEADS, n_q, 1)), full((rows, width))]),
            out_specs=pl.BlockSpec((1, rows, width), lambda b, pt: (b, 0, 0))),
        out_shape=jax.ShapeDtypeStruct((n_b, rows, width), F32),
        compiler_params=pltpu.CompilerParams(dimension_semantics=("arbitrary",),
                                             vmem_limit_bytes=VMEM_LIMIT),
        name="dsa_sample",
    )(page_table, *([cache_idx] * n_pages), _new_page(aik),
      *([ck] * n_pages), _new_page(ak.reshape(n_b, n_q, width)),
      *([cv] * n_pages), _new_page(av.reshape(n_b, n_q, width)), qi, w, qbd)
    return _take_diag_heads(out, n_q, A_KV, A_GROUP)


def moba_sample(layer, page_table, cache_k, cache_v, bq, bk, bv):
    n_b, n_q = bq.shape[:2]
    n_pages = page_table.shape[1]
    width = B_KV * HEAD_DIM
    ck = cache_k.reshape(cache_k.shape[:3] + (width,))
    cv = cache_v.reshape(cache_v.shape[:3] + (width,))
    qbd = _block_diag_queries(bq * (HEAD_DIM ** -0.5))
    new = pl.BlockSpec((1, PAGE_SIZE, width), lambda b, pt: (b, 0, 0))
    rows = B_HEADS * n_q
    out = pl.pallas_call(
        functools.partial(_moba_sample_kernel, n_pages=n_pages),
        grid_spec=pltpu.PrefetchScalarGridSpec(
            num_scalar_prefetch=1, grid=(n_b,),
            in_specs=(_page_specs(layer, n_pages, (width,)) + [new]
                      + _page_specs(layer, n_pages, (width,)) + [new]
                      + [pl.BlockSpec((1, rows, width), lambda b, pt: (b, 0, 0))]),
            out_specs=pl.BlockSpec((1, rows, width), lambda b, pt: (b, 0, 0))),
        out_shape=jax.ShapeDtypeStruct((n_b, rows, width), F32),
        compiler_params=pltpu.CompilerParams(dimension_semantics=("arbitrary",),
                                             vmem_limit_bytes=VMEM_LIMIT),
        name="moba_sample",
    )(page_table, *([ck] * n_pages), _new_page(bk.reshape(n_b, n_q, width)),
      *([cv] * n_pages), _new_page(bv.reshape(n_b, n_q, width)), qbd)
    return _take_diag_heads(out, n_q, B_KV, B_GROUP)


def _rms(x, g):
    xf = x.astype(F32)
    y = xf * lax.rsqrt(jnp.mean(xf * xf, axis=-1, keepdims=True) + EPS)
    return (y * g.astype(F32)).astype(x.dtype)


def _rope(x, pos):
    half = x.shape[-1] // 2
    freqs = jnp.power(jnp.float32(ROPE_THETA), -jnp.arange(half, dtype=F32) / half)
    ang = pos.astype(F32)[:, None] * freqs[None, :]
    cos = jnp.cos(ang)[None, :, None, :]
    sin = jnp.sin(ang)[None, :, None, :]
    xf = x.astype(F32)
    x1, x2 = xf[..., :half], xf[..., half:]
    return jnp.concatenate([x1 * cos - x2 * sin, x1 * sin + x2 * cos], axis=-1).astype(x.dtype)


def _to_blocks(a, nb):
    return jnp.moveaxis(a.reshape(a.shape[0], nb, a.shape[1] // nb, *a.shape[2:]), 1, 0)


def _from_blocks(a):
    a = jnp.moveaxis(a, 0, 1)
    return a.reshape(a.shape[0], a.shape[1] * a.shape[2], *a.shape[3:])


def _dsa_attend(q, qi, wi, k_all, v_all, ki_all, q_pos):
    n_q = q.shape[1]
    n_k = k_all.shape[1]
    top = min(DSA_TOPK, n_k // 4)
    blk = Q_BLOCK if n_q % Q_BLOCK == 0 else n_q
    nb = n_q // blk
    k_pos = jnp.arange(n_k, dtype=jnp.int32)
    scale = HEAD_DIM ** -0.5
    ki_f = ki_all.astype(F32)

    def block(args):
        qb, qib, wib, pb = args
        s = jax.nn.relu(jnp.einsum('bqhd,bld->bqhl', qib.astype(F32), ki_f))
        score = jnp.einsum('bqhl,bqh->bql', s, wib.astype(F32))
        score = jnp.where(k_pos[None, None, :] <= pb[None, :, None], score, -jnp.inf)
        _, sel = lax.top_k(score, top)
        valid = sel <= pb[None, :, None]
        kg = jax.vmap(lambda kk, ii: kk[ii])(k_all, sel)
        vg = jax.vmap(lambda vv, ii: vv[ii])(v_all, sel)
        logits = jnp.einsum('bqkgd,bqskd->bqkgs', qb.astype(F32), kg.astype(F32)) * scale
        logits = jnp.where(valid[:, :, None, None, :], logits, -jnp.inf)
        p = jax.nn.softmax(logits, axis=-1)
        return jnp.einsum('bqkgs,bqskd->bqkgd', p, vg.astype(F32)).astype(q.dtype)

    out = lax.map(block, (_to_blocks(q, nb), _to_blocks(qi, nb), _to_blocks(wi, nb), q_pos.reshape(nb, blk)))
    return _from_blocks(out)


def _moba_attend(q, k_all, v_all, q_pos):
    n_b, n_q = q.shape[:2]
    n_k = k_all.shape[1]
    n_blk = -(-n_k // MOBA_BLOCK)
    pad = n_blk * MOBA_BLOCK - n_k
    kb = jnp.pad(k_all, ((0, 0), (0, pad), (0, 0), (0, 0))).reshape(n_b, n_blk, MOBA_BLOCK, B_KV, HEAD_DIM)
    vb = jnp.pad(v_all, ((0, 0), (0, pad), (0, 0), (0, 0))).reshape(n_b, n_blk, MOBA_BLOCK, B_KV, HEAD_DIM)
    k_mean = jnp.mean(kb.astype(F32), axis=2)
    kb_h = jnp.moveaxis(kb, 3, 1)
    vb_h = jnp.moveaxis(vb, 3, 1)
    top = min(MOBA_TOPK, n_blk)
    blk = MOBA_Q_BLOCK if n_q % MOBA_Q_BLOCK == 0 else n_q
    nb = n_q // blk
    blk_ids = jnp.arange(n_blk, dtype=jnp.int32)
    in_blk = jnp.arange(MOBA_BLOCK, dtype=jnp.int32)
    b_ix = jnp.arange(n_b)
    kv_ix = jnp.arange(B_KV)
    scale = HEAD_DIM ** -0.5

    def block(args):
        qb, pb = args
        qf = qb.astype(F32)
        own = pb // MOBA_BLOCK
        gate = jnp.einsum('bqkgd,bnkd->bqkgn', qf, k_mean)
        past = blk_ids[None, :] < own[:, None]
        gate = jnp.where(past[None, :, None, None, :], gate, -jnp.inf)
        _, sel = lax.top_k(gate, top)
        sel_ok = sel < own[None, :, None, None, None]
        bi = b_ix[:, None, None, None, None]
        ki = kv_ix[None, None, :, None, None]
        kg = kb_h[bi, ki, sel]
        vg = vb_h[bi, ki, sel]
        l_sel = jnp.einsum('bqkgd,bqkgnsd->bqkgns', qf, kg.astype(F32)) * scale
        l_sel = jnp.where(sel_ok[..., None], l_sel, -jnp.inf).reshape(n_b, blk, B_KV, B_GROUP, top * MOBA_BLOCK)
        ko = kb[b_ix[:, None], own[None, :]]
        vo = vb[b_ix[:, None], own[None, :]]
        l_own = jnp.einsum('bqkgd,bqskd->bqkgs', qf, ko.astype(F32)) * scale
        own_pos = own[:, None] * MOBA_BLOCK + in_blk[None, :]
        l_own = jnp.where((own_pos <= pb[:, None])[None, :, None, None, :], l_own, -jnp.inf)
        p = jax.nn.softmax(jnp.concatenate([l_sel, l_own], axis=-1), axis=-1)
        p_sel = p[..., :top * MOBA_BLOCK].reshape(n_b, blk, B_KV, B_GROUP, top, MOBA_BLOCK)
        p_own = p[..., top * MOBA_BLOCK:]
        out = (jnp.einsum('bqkgns,bqkgnsd->bqkgd', p_sel, vg.astype(F32))
               + jnp.einsum('bqkgs,bqskd->bqkgd', p_own, vo.astype(F32)))
        return out.astype(q.dtype)

    out = lax.map(block, (_to_blocks(q, nb), q_pos.reshape(nb, blk)))
    return _from_blocks(out)


def _mlstm(q, k, v, i_pre, f_pre, C0, n0, m0):
    n_s = q.shape[1]
    L = C_CHUNK if n_s % C_CHUNK == 0 else n_s
    nc = n_s // L
    q = q.astype(F32)
    k = k.astype(F32) * (C_HD ** -0.5)
    v = v.astype(F32)
    i_pre = i_pre.astype(F32)
    logf = jax.nn.log_sigmoid(f_pre.astype(F32))
    tri = jnp.tril(jnp.ones((L, L), dtype=bool))

    def step(carry, xs):
        C, n, m = carry
        qc, kc, vc, ic, fc = xs
        b = jnp.moveaxis(jnp.cumsum(fc, axis=1), 1, 2)
        it = jnp.moveaxis(ic, 1, 2)
        dmat = jnp.where(tri, b[..., :, None] - b[..., None, :] + it[..., None, :], -jnp.inf)
        m_t = jnp.maximum(m[..., None] + b, jnp.max(dmat, axis=-1))
        inter = jnp.exp(m[..., None] + b - m_t)
        s = jnp.einsum('blhd,bshd->bhls', qc, kc) * jnp.exp(dmat - m_t[..., None])
        num = jnp.einsum('bhls,bshd->bhld', s, vc) + inter[..., None] * jnp.einsum('blhd,bhde->bhle', qc, C)
        den = jnp.sum(s, axis=-1) + inter * jnp.einsum('blhd,bhd->bhl', qc, n)
        h = num / jnp.maximum(jnp.abs(den), jnp.exp(-m_t))[..., None]
        m_new = m_t[..., -1]
        decay = jnp.exp(m + b[..., -1] - m_new)
        wk = jnp.exp(b[..., -1:] - b + it - m_new[..., None])
        C_new = decay[..., None, None] * C + jnp.einsum('bhs,bshd,bshe->bhde', wk, kc, vc)
        n_new = decay[..., None] * n + jnp.einsum('bhs,bshd->bhd', wk, kc)
        return (C_new, n_new, m_new), jnp.moveaxis(h, 1, 2)

    xs = (_to_blocks(q, nc), _to_blocks(k, nc), _to_blocks(v, nc), _to_blocks(i_pre, nc), _to_blocks(logf, nc))
    (C, n, m), h = lax.scan(step, (C0.astype(F32), n0.astype(F32), m0.astype(F32)), xs)
    return _from_blocks(h), C, n, m


def _peer(h_in, g_ffn, w_q, subkeys, u_tab, v_tab):
    n_b, n_s, d = h_in.shape
    hq = _mm3(h_in, w_q, g_ffn)
    hn = _rms(h_in, g_ffn)
    hf = hn.reshape(n_b * n_s, d)
    qf = hq.reshape(n_b * n_s, -1)
    n_tok = hf.shape[0]
    blk = min(TOKEN_BLOCK, n_tok)
    nb = n_tok // blk
    half = PEER_QDIM // 2
    sk = subkeys.astype(F32)

    def block(args):
        hb, qb = args
        q = qb.reshape(blk, PEER_HEADS, 2, half).astype(F32)
        s = jnp.einsum('nhpd,hpkd->nhpk', q, sk)
        sv, si = lax.top_k(s, PEER_TOPK)
        cand = (sv[:, :, 0, :, None] + sv[:, :, 1, None, :]).reshape(blk, PEER_HEADS, PEER_TOPK * PEER_TOPK)
        cid = (si[:, :, 0, :, None] * PEER_NKEYS + si[:, :, 1, None, :]).reshape(blk, PEER_HEADS, PEER_TOPK * PEER_TOPK)
        tv, ti = lax.top_k(cand, PEER_TOPK)
        eid = jnp.take_along_axis(cid, ti, axis=-1)
        g = jax.nn.softmax(tv, axis=-1)
        u = u_tab[eid].astype(F32)
        a = jax.nn.gelu(jnp.einsum('nd,nhkd->nhk', hb.astype(F32), u))
        out = jnp.einsum('nhk,nhkd->nd', g * a, v_tab[eid].astype(F32))
        return out.astype(h_in.dtype)

    out = lax.map(block, (hf.reshape(nb, blk, d), qf.reshape(nb, blk, -1))).reshape(nb * blk, d)
    return out.reshape(n_b, n_s, d)


def _heads_t(a):
    return jnp.transpose(a.astype(BF16), (0, 2, 3, 1))


def _values_t(v, tk):
    n_b, n_s, n_kv, d = v.shape
    vt = jnp.transpose(v.astype(BF16), (0, 2, 3, 1)).reshape(n_b, n_kv, d, n_s // tk, tk)
    return jnp.transpose(vt, (0, 1, 3, 2, 4))


def _dsa_prompt_glue(aq, aiq, aiw, ak, av, aik):
    n_b, n_s = aq.shape[:2]
    scale = HEAD_DIM ** -0.5
    qt = _heads_t(aq.reshape(n_b, n_s, A_HEADS, HEAD_DIM) * scale)
    ot = dsa_prompt(_heads_t(aiq), jnp.transpose(aiw, (0, 2, 1)), aik.astype(BF16), qt,
                    jnp.transpose(ak.astype(BF16), (0, 2, 1, 3)), _values_t(av, ATT_TK),
                    top=min(DSA_TOPK, n_s // 4))
    return jnp.transpose(ot, (0, 3, 1, 2)).reshape(n_b, n_s, A_WIDTH)


def _moba_prompt_glue(bq, bk, bv):
    n_b, n_s = bq.shape[:2]
    scale = HEAD_DIM ** -0.5
    qt = _heads_t(bq.reshape(n_b, n_s, B_HEADS, HEAD_DIM) * scale)
    ot = moba_prompt(qt, jnp.transpose(bk.astype(BF16), (0, 2, 1, 3)), _values_t(bv, MOBA_BLOCK))
    return jnp.transpose(ot, (0, 3, 1, 2)).reshape(n_b, n_s, B_WIDTH)


def _mix_ffn(x, p_l, lw, oa, ob, oc, g_a, g_b, g_c):
    n_b, n_s, _ = x.shape
    mix = (jax.nn.sigmoid(g_a) * _mm3(oa, lw['w_br_a'])
           + jax.nn.sigmoid(g_b) * _mm3(ob, lw['w_br_b'])
           + jax.nn.sigmoid(g_c) * _mm3(oc, lw['w_br_c']))
    x = x + _mm3(mix, lw['w_out'])
    x = x + peer(x.reshape(n_b * n_s, -1), lw['g_ffn'], lw['w_peer_q'], lw['peer_subkeys'],
                 lw['peer_u'], lw['peer_v']).reshape(x.shape)
    return x + jax.nn.sigmoid(_mm3(x, lw['w_ple_gate'])) * _mm3(p_l, lw['w_ple'])


def _layer_prompt(x, p_l, lw):
    n_b, n_s, _ = x.shape
    pr = prompt_projection(x, lw)
    oa_t = dsa_prompt(pr['qi'], pr['wi'], pr['ki_h'], pr['qa'], pr['ka_h'], pr['va_t'],
                      top=min(DSA_TOPK, n_s // 4))
    ob_t = moba_prompt(pr['qb'], pr['kb_h'], pr['vb_t'])
    oa = jnp.transpose(oa_t, (0, 3, 1, 2)).reshape(n_b, n_s, A_WIDTH)
    ob = jnp.transpose(ob_t, (0, 3, 1, 2)).reshape(n_b, n_s, B_WIDTH)
    zc = pr['zc']
    oc, C1, n1, m1 = mlstm_call((zc, 0), pr['kc_t'], (zc, 1), (zc, 2), pr['gates'], lw['g_c_out'], jnp.zeros((n_b, C_HEADS, C_HD, C_HD), F32),
                                jnp.zeros((n_b, C_HEADS, C_HD), F32), jnp.zeros((n_b, C_HEADS), F32), n_s)
    g0 = 3 * C_WIDTH
    x = _mix_ffn(x, p_l, lw, oa, ob, oc, zc[..., g0:g0 + D_MODEL], zc[..., g0 + D_MODEL:g0 + 2 * D_MODEL],
                 zc[..., g0 + 2 * D_MODEL:])
    kv4 = lambda a: a.reshape(n_b, n_s, A_KV, HEAD_DIM)
    return x, (kv4(pr['ka']), kv4(pr['va']), pr['ki'], kv4(pr['kb']), kv4(pr['vb']), C1, n1, m1)


def _layer_sample(x, p_l, pos, lw, past):
    n_b, n_s, _ = x.shape
    z = _mm3(x, lw['w_in'], lw['g_mix'])
    (aq, ak, av, aiq, aiw, aik, bq, bk, bv, cq, ck, cv, co, ci, cf, gate) = jnp.split(z, IN_SPLITS, axis=-1)
    aq = _rope(_rms(aq.reshape(n_b, n_s, A_HEADS, HEAD_DIM), lw['g_qa']), pos).reshape(n_b, n_s, A_KV, A_GROUP, HEAD_DIM)
    ak = _rope(_rms(ak.reshape(n_b, n_s, A_KV, HEAD_DIM), lw['g_ka']), pos)
    av = av.reshape(n_b, n_s, A_KV, HEAD_DIM)
    aiq = _rope(aiq.reshape(n_b, n_s, IDX_HEADS, IDX_DIM), pos)
    aik = _rope(aik.reshape(n_b, n_s, 1, IDX_DIM), pos)[:, :, 0]
    aiw = aiw * IDX_W_SCALE
    bq = _rope(_rms(bq.reshape(n_b, n_s, B_HEADS, HEAD_DIM), lw['g_qb']), pos).reshape(n_b, n_s, B_KV, B_GROUP, HEAD_DIM)
    bk = _rope(_rms(bk.reshape(n_b, n_s, B_KV, HEAD_DIM), lw['g_kb']), pos)
    bv = bv.reshape(n_b, n_s, B_KV, HEAD_DIM)
    cq = cq.reshape(n_b, n_s, C_HEADS, C_HD)
    ck = ck.reshape(n_b, n_s, C_HEADS, C_HD)
    cv = cv.reshape(n_b, n_s, C_HEADS, C_HD)
    ci = ci + lw['b_if'][:C_HEADS]
    cf = cf + lw['b_if'][C_HEADS:]
    oa = dsa_sample(past['layer'], past['page_table'], past['a_idx'], past['a_k'], past['a_v'],
                    aq, aiq, aiw, ak, av, aik)
    ob = moba_sample(past['layer'], past['page_table'], past['b_k'], past['b_v'], bq, bk, bv)
    oc, C1, n1, m1 = mlstm(cq.reshape(n_b, n_s, C_WIDTH), ck.reshape(n_b, n_s, C_WIDTH),
                           cv.reshape(n_b, n_s, C_WIDTH), co, ci, cf, lw['g_c_out'],
                           past['c_C'], past['c_n'], past['c_m'])
    g_a, g_b, g_c = jnp.split(gate, N_BRANCH, axis=-1)
    return _mix_ffn(x, p_l, lw, oa, ob, oc, g_a, g_b, g_c), (ak, av, aik, bk, bv, C1, n1, m1)


def kernel(x_prompt, x_sample, p_prompt, p_sample, cache_a_k, cache_a_v, cache_a_idx, cache_b_k, cache_b_v,
           state_c_C, state_c_n, state_c_m, page_table, g_mix, w_in, b_if, g_qa, g_ka, g_qb, g_kb, g_c_out,
           w_br_a, w_br_b, w_br_c, w_out, g_ffn, w_peer_q, peer_subkeys, peer_u, peer_v, w_ple, w_ple_gate):
    past_len = page_table.shape[1] * PAGE_SIZE
    pos_s = past_len + jnp.arange(x_sample.shape[1], dtype=jnp.int32)
    yp, ys = x_prompt, x_sample
    states_p, states_s = [], []
    for l in range(DEPTH):
        lw = {'g_mix': g_mix[l], 'w_in': w_in[l], 'b_if': b_if[l], 'g_qa': g_qa[l], 'g_ka': g_ka[l],
              'g_qb': g_qb[l], 'g_kb': g_kb[l], 'g_c_out': g_c_out[l], 'w_br_a': w_br_a[l], 'w_br_b': w_br_b[l],
              'w_br_c': w_br_c[l], 'w_out': w_out[l], 'g_ffn': g_ffn[l], 'w_peer_q': w_peer_q[l],
              'peer_subkeys': peer_subkeys[l], 'peer_u': peer_u[l], 'peer_v': peer_v[l], 'w_ple': w_ple[l],
              'w_ple_gate': w_ple_gate[l]}
        yp, st_p = _layer_prompt(yp, p_prompt[l], lw)
        past = {'layer': l, 'page_table': page_table, 'a_k': cache_a_k, 'a_v': cache_a_v, 'a_idx': cache_a_idx,
                'b_k': cache_b_k, 'b_v': cache_b_v,
                'c_C': state_c_C[l], 'c_n': state_c_n[l], 'c_m': state_c_m[l]}
        ys, st_s = _layer_sample(ys, p_sample[l], pos_s, lw, past)
        states_p.append(st_p)
        states_s.append(st_s)
    (pa_k, pa_v, pa_i, pb_k, pb_v, pc_C, pc_n, pc_m) = [jnp.stack(a) for a in zip(*states_p)]
    (sa_k, sa_v, sa_i, sb_k, sb_v, sc_C, sc_n, sc_m) = [jnp.stack(a) for a in zip(*states_s)]
    return (yp, ys, pa_k, pa_v, pa_i, pb_k, pb_v, pc_C, pc_n, pc_m,
            sa_k, sa_v, sa_i, sb_k, sb_v, sc_C, sc_n, sc_m)
```

```python
import functools
import math

import jax
import jax.numpy as jnp
import numpy as np
from jax import lax
from jax.experimental import pallas as pl
from jax.experimental.pallas import tpu as pltpu

D_MODEL = 1024
DEPTH = 4
PAGE_SIZE = 128
HEAD_DIM = 64
A_HEADS = 8
A_KV = 4
A_GROUP = A_HEADS // A_KV
A_WIDTH = A_HEADS * HEAD_DIM
IDX_HEADS = 8
IDX_DIM = 64
IDX_W_SCALE = (IDX_HEADS ** -0.5) * (IDX_DIM ** -0.5)
DSA_TOPK = 256
B_HEADS = 8
B_KV = 4
B_GROUP = B_HEADS // B_KV
B_WIDTH = B_HEADS * HEAD_DIM
MOBA_BLOCK = 256
MOBA_TOPK = 3
C_HEADS = 4
C_HD = 128
C_WIDTH = C_HEADS * C_HD
C_CHUNK = 64
N_BRANCH = 3
PEER_HEADS = 8
PEER_NKEYS = 128
PEER_QDIM = 256
PEER_TOPK = 16
PLE_DIM = 256
ROPE_THETA = 10000.0
EPS = 1e-6
Q_BLOCK = 128
MOBA_Q_BLOCK = 32
TOKEN_BLOCK = 256

IN_SIZES = (A_WIDTH, A_KV * HEAD_DIM, A_KV * HEAD_DIM, IDX_HEADS * IDX_DIM, IDX_HEADS, IDX_DIM,
            B_WIDTH, B_KV * HEAD_DIM, B_KV * HEAD_DIM,
            C_WIDTH, C_WIDTH, C_WIDTH, C_WIDTH, C_HEADS, C_HEADS,
            N_BRANCH * D_MODEL)
IN_SPLITS = tuple(int(s) for s in np.cumsum(IN_SIZES)[:-1])

F32 = jnp.float32
BF16 = jnp.bfloat16

LANES = 128
SUBLANES = 8
VMEM_LIMIT = 48 * 1024 * 1024


def _mm_kernel(x_ref, g_ref, w_ref, o_ref, h_ref, *, normalize):
    @pl.when(pl.program_id(1) == 0)
    def _():
        x = x_ref[...]
        if normalize:
            x = x * lax.rsqrt(jnp.mean(x * x, axis=-1, keepdims=True) + EPS) * g_ref[...]
        h_ref[...] = x.astype(BF16)

    o_ref[...] = jnp.dot(h_ref[...], w_ref[...], preferred_element_type=F32)


def _pick_tile(n, cands):
    for c in cands:
        if n % c == 0:
            return c
    return n


def matmul(x, w, g=None):
    m, k = x.shape
    n = w.shape[1]
    n_pad = -(-n // LANES) * LANES
    wb = w.astype(BF16)
    if n_pad != n:
        wb = jnp.pad(wb, ((0, 0), (0, n_pad - n)))
    tm = _pick_tile(m, (1024, 512, 256, 128))
    tn = _pick_tile(n_pad, (512, 256, 128))
    gg = (jnp.ones((k,), F32) if g is None else g.astype(F32)).reshape(1, k)
    out = pl.pallas_call(
        functools.partial(_mm_kernel, normalize=g is not None),
        grid=(m // tm, n_pad // tn),
        in_specs=[pl.BlockSpec((tm, k), lambda i, j: (i, 0)),
                  pl.BlockSpec((1, k), lambda i, j: (0, 0)),
                  pl.BlockSpec((k, tn), lambda i, j: (0, j))],
        out_specs=pl.BlockSpec((tm, tn), lambda i, j: (i, j)),
        out_shape=jax.ShapeDtypeStruct((m, n_pad), F32),
        scratch_shapes=[pltpu.VMEM((tm, k), BF16)],
        compiler_params=pltpu.CompilerParams(dimension_semantics=("arbitrary", "arbitrary"),
                                             vmem_limit_bytes=VMEM_LIMIT),
        name="matmul",
    )(x, gg, wb)
    return out[:, :n] if n_pad != n else out


def _mm3(x, w, g=None):
    b, s, k = x.shape
    return matmul(x.reshape(b * s, k), w, g).reshape(b, s, w.shape[1])


PROJ_TM = 512
HALF = HEAD_DIM // 2
T_ROWS = (A_WIDTH, IDX_HEADS * IDX_DIM, B_WIDTH, A_KV * HEAD_DIM, B_KV * HEAD_DIM, C_WIDTH, 2 * SUBLANES)
T_OFFS = tuple(int(v) for v in np.cumsum((0,) + T_ROWS))


def _rms_rows(x, g):
    return x * lax.rsqrt(jnp.mean(x * x, axis=0, keepdims=True) + EPS) * g


def _rope_rows(x, cos, sin):
    x1, x2 = x[:HALF], x[HALF:]
    return jnp.concatenate([x1 * cos - x2 * sin, x1 * sin + x2 * cos], axis=0)


def _proj_t_kernel(x_ref, g_ref, wt_ref, cos_ref, sin_ref, gq_ref, bias_ref,
                   qa_ref, qi_ref, wi_ref, qb_ref, va_ref, vb_ref, kc_ref, if_ref):
    x = x_ref[0]
    hb = (x * lax.rsqrt(jnp.mean(x * x, axis=-1, keepdims=True) + EPS) * g_ref[...]).astype(BF16)
    tm = hb.shape[0]
    cos, sin = cos_ref[...], sin_ref[...]
    scale = HEAD_DIM ** -0.5

    def seg(k):
        return lax.dot_general(wt_ref[T_OFFS[k]:T_OFFS[k + 1], :], hb, (((1,), (1,)), ((), ())),
                               preferred_element_type=F32)

    for k, (out_ref, gain) in enumerate(((qa_ref, 0), (qi_ref, None), (qb_ref, 1))):
        z = seg((0, 1, 2)[k])
        for h in range(A_HEADS):
            xh = z[h * HEAD_DIM:(h + 1) * HEAD_DIM]
            if gain is not None:
                xh = _rms_rows(xh, gq_ref[gain]) * scale
            out_ref[0, h] = _rope_rows(xh, cos, sin).astype(BF16)
    for k, out_ref in ((3, va_ref), (4, vb_ref)):
        z = seg(k).astype(BF16)
        for kv in range(A_KV):
            for j in range(tm // ATT_TK):
                out_ref[0, kv, j] = z[kv * HEAD_DIM:(kv + 1) * HEAD_DIM, j * ATT_TK:(j + 1) * ATT_TK]
    kc_ref[0] = seg(5) * (C_HD ** -0.5)
    small = seg(6)
    wi_ref[0] = small[0:IDX_HEADS] * IDX_W_SCALE
    if_ref[0] = small[IDX_HEADS:] + bias_ref[...]


def _rope_lanes(x, cos, sin_lo, sin_hi):
    return x * cos + pltpu.roll(x, LANES - HALF, 1) * sin_lo + pltpu.roll(x, HALF, 1) * sin_hi


def _proj_kv_kernel(x_ref, g_ref, w_ref, cos_ref, sinlo_ref, sinhi_ref, gk_ref, bd_ref,
                    ka_ref, va_ref, ki_ref, kb_ref, vb_ref, kah_ref, kih_ref, kbh_ref):
    x = x_ref[0]
    hb = (x * lax.rsqrt(jnp.mean(x * x, axis=-1, keepdims=True) + EPS) * g_ref[...]).astype(BF16)
    cos, sin_lo, sin_hi = cos_ref[...], sinlo_ref[...], sinhi_ref[...]
    width = A_KV * HEAD_DIM

    def seg(c0, n):
        return jnp.dot(hb, w_ref[:, c0:c0 + n], preferred_element_type=F32)

    def rope(z):
        return jnp.concatenate([_rope_lanes(z[:, j * LANES:(j + 1) * LANES], cos, sin_lo, sin_hi)
                                for j in range(z.shape[1] // LANES)], axis=1)

    for k, (f_ref, h_ref) in enumerate(((ka_ref, kah_ref), (kb_ref, kbh_ref))):
        z = seg(k * width, width)
        ms = jnp.dot(z * z, bd_ref[...], precision=HIGHEST, preferred_element_type=F32)
        z = rope(z * lax.rsqrt(ms + EPS) * gk_ref[k:k + 1, :])
        f_ref[0] = z
        zb = z.astype(BF16)
        for kv in range(A_KV):
            h_ref[0, kv] = zb[:, kv * HEAD_DIM:(kv + 1) * HEAD_DIM]
    va_ref[0] = seg(2 * width, width)
    vb_ref[0] = seg(3 * width, width)
    zi = rope(seg(4 * width, LANES))[:, :IDX_DIM]
    ki_ref[0] = zi
    kih_ref[0] = zi.astype(BF16)


def _rope_tables(n_s):
    freqs = jnp.power(jnp.float32(ROPE_THETA), -jnp.arange(HALF, dtype=F32) / HALF)
    ang = jnp.arange(n_s, dtype=F32)[:, None] * freqs[None, :]
    return jnp.cos(ang), jnp.sin(ang)


def prompt_projection(x, lw):
    n_b, n_s, d = x.shape
    tm = PROJ_TM
    n_t = n_s // tm
    w = lw['w_in']
    cols = dict(zip(('aq', 'ak', 'av', 'aiq', 'aiw', 'aik', 'bq', 'bk', 'bv', 'cq', 'ck', 'cv', 'co', 'ci', 'cf',
                     'gate'), jnp.split(w, IN_SPLITS, axis=1)))
    g = lw['g_mix'].astype(F32).reshape(1, d)
    cos, sin = _rope_tables(n_s)
    x_spec = pl.BlockSpec((1, tm, d), lambda b, i: (b, i, 0))
    g_spec = pl.BlockSpec((1, d), lambda b, i: (0, 0))
    params = pltpu.CompilerParams(dimension_semantics=("arbitrary", "arbitrary"), vmem_limit_bytes=VMEM_LIMIT)

    wt = jnp.concatenate([cols[k] for k in ('aq', 'aiq', 'bq', 'av', 'bv', 'ck', 'aiw', 'ci', 'cf')],
                         axis=1).T.astype(BF16)
    gq = jnp.stack([jnp.broadcast_to(lw[k].astype(F32)[:, None], (HEAD_DIM, tm)) for k in ('g_qa', 'g_qb')])
    bias = jnp.broadcast_to(lw['b_if'].astype(F32)[:, None], (2 * C_HEADS, tm))
    heads_t = lambda: pl.BlockSpec((1, A_HEADS, HEAD_DIM, tm), lambda b, i: (b, 0, 0, i))
    vals_t = lambda: pl.BlockSpec((1, A_KV, tm // ATT_TK, HEAD_DIM, ATT_TK), lambda b, i: (b, 0, i, 0, 0))
    rows_t = lambda n: pl.BlockSpec((1, n, tm), lambda b, i: (b, 0, i))
    tab_t = pl.BlockSpec((HALF, tm), lambda b, i: (0, i))
    const = lambda shape: pl.BlockSpec(shape, lambda b, i: (0,) * len(shape))
    heads_shape = jax.ShapeDtypeStruct((n_b, A_HEADS, HEAD_DIM, n_s), BF16)
    vals_shape = jax.ShapeDtypeStruct((n_b, A_KV, n_s // ATT_TK, HEAD_DIM, ATT_TK), BF16)
    qa, qi, wi, qb, va_t, vb_t, kc_t, gates = pl.pallas_call(
        _proj_t_kernel,
        grid=(n_b, n_t),
        in_specs=[x_spec, g_spec, const(wt.shape), tab_t, tab_t, const(gq.shape), const(bias.shape)],
        out_specs=[heads_t(), heads_t(), rows_t(IDX_HEADS), heads_t(), vals_t(), vals_t(),
                   rows_t(C_WIDTH), rows_t(2 * C_HEADS)],
        out_shape=[heads_shape, heads_shape, jax.ShapeDtypeStruct((n_b, IDX_HEADS, n_s), F32), heads_shape,
                   vals_shape, vals_shape, jax.ShapeDtypeStruct((n_b, C_WIDTH, n_s), F32),
                   jax.ShapeDtypeStruct((n_b, 2 * C_HEADS, n_s), F32)],
        compiler_params=params,
        name="proj_t",
    )(x, g, wt, cos.T, sin.T, gq, bias)

    width = A_KV * HEAD_DIM
    wkv = jnp.concatenate([cols['ak'], cols['bk'], cols['av'], cols['bv'], cols['aik'],
                           jnp.zeros((d, LANES - IDX_DIM), w.dtype)], axis=1).astype(BF16)
    lane = jnp.arange(LANES)
    cos_l = jnp.tile(cos, (1, LANES // HALF))
    sin_l = jnp.tile(sin, (1, LANES // HALF))
    sin_lo = jnp.where((lane % HEAD_DIM) < HALF, -sin_l, 0.0)
    sin_hi = jnp.where((lane % HEAD_DIM) >= HALF, sin_l, 0.0)
    gk = jnp.stack([jnp.tile(lw[k].astype(F32), A_KV) for k in ('g_ka', 'g_kb')])
    head_of = jnp.arange(width) // HEAD_DIM
    bd = jnp.where(head_of[:, None] == head_of[None, :], 1.0 / HEAD_DIM, 0.0).astype(F32)
    tab = pl.BlockSpec((tm, LANES), lambda b, i: (i, 0))
    tok = lambda n: pl.BlockSpec((1, tm, n), lambda b, i: (b, i, 0))
    heads = lambda: pl.BlockSpec((1, A_KV, tm, HEAD_DIM), lambda b, i: (b, 0, i, 0))
    tok_shape = lambda n, dt=F32: jax.ShapeDtypeStruct((n_b, n_s, n), dt)
    heads_kv = jax.ShapeDtypeStruct((n_b, A_KV, n_s, HEAD_DIM), BF16)
    ka, va, ki, kb, vb, ka_h, ki_h, kb_h = pl.pallas_call(
        _proj_kv_kernel,
        grid=(n_b, n_t),
        in_specs=[x_spec, g_spec, const(wkv.shape), tab, tab, tab, const(gk.shape), const(bd.shape)],
        out_specs=[tok(width), tok(width), tok(IDX_DIM), tok(width), tok(width), heads(), tok(IDX_DIM), heads()],
        out_shape=[tok_shape(width), tok_shape(width), tok_shape(IDX_DIM), tok_shape(width), tok_shape(width),
                   heads_kv, tok_shape(IDX_DIM, BF16), heads_kv],
        compiler_params=params,
        name="proj_kv",
    )(x, g, wkv, cos_l, sin_lo, sin_hi, gk, bd)

    wc = jnp.concatenate([cols['cq'], cols['cv'], cols['co'], cols['gate']], axis=1)
    zc = matmul(x.reshape(n_b * n_s, d), wc, lw['g_mix']).reshape(n_b, n_s, -1)
    return dict(qa=qa, qi=qi, wi=wi, qb=qb, va_t=va_t, vb_t=vb_t, kc_t=kc_t, gates=gates,
                ka=ka, va=va, ki=ki, kb=kb, vb=vb, ka_h=ka_h, ki_h=ki_h, kb_h=kb_h, zc=zc)


INT_MIN = -2 ** 31
INT_MAX = 2 ** 31 - 1
NEG_BIG = -1e30
ATT_TQ = 128
ATT_TK = 256
BISECT_VALUE_STEPS = 12
BISECT_INT_STEPS = 33


def _col_sum_i32(mask):
    rows, tq = mask.shape
    return jnp.sum(mask.astype(jnp.int32).reshape(rows // SUBLANES, SUBLANES, tq), axis=0)


def _attend_init(m_ref, l_ref, acc_ref):
    m_ref[...] = jnp.full(m_ref.shape, NEG_BIG, F32)
    l_ref[...] = jnp.zeros(l_ref.shape, F32)
    acc_ref[...] = jnp.zeros(acc_ref.shape, F32)


def _attend_chunk(qt_ref, k_ref, vt_ref, start, c, sel_of, s_ref, p_ref, a_ref, m_ref, l_ref, acc_ref):
    n_kv, tk = s_ref.shape[0], s_ref.shape[1]
    n_g = qt_ref.shape[1] // n_kv
    for kv in range(n_kv):
        q2 = jnp.concatenate([qt_ref[0, kv * n_g + g] for g in range(n_g)], axis=1)
        s_ref[kv] = jnp.dot(k_ref[0, kv, pl.ds(start, tk), :], q2, preferred_element_type=F32)
    for kv in range(n_kv):
        sel = sel_of(kv)
        s = s_ref[kv]
        m_old = m_ref[kv]
        m_new = jnp.maximum(m_old, jnp.max(jnp.where(sel, s, NEG_BIG), axis=0, keepdims=True))
        p = jnp.where(sel, jnp.exp(s - m_new), 0.0)
        alpha = jnp.exp(m_old - m_new)
        l_ref[kv] = alpha * l_ref[kv] + jnp.sum(p, axis=0, keepdims=True)
        p_ref[kv] = p.astype(BF16)
        a_ref[kv] = alpha
        m_ref[kv] = m_new
    for kv in range(n_kv):
        acc_ref[kv] = a_ref[kv] * acc_ref[kv] + jnp.dot(vt_ref[0, kv, c], p_ref[kv],
                                                        preferred_element_type=F32)


def _attend_finish(o_ref, l_ref, acc_ref):
    n_kv = acc_ref.shape[0]
    n_g = o_ref.shape[1] // n_kv
    tq = o_ref.shape[3]
    for kv in range(n_kv):
        out = acc_ref[kv] / l_ref[kv]
        for g in range(n_g):
            o_ref[0, kv * n_g + g] = out[:, g * tq:(g + 1) * tq]


def _attend_scratch(n_kv, n_g, tq, tk):
    wide = n_g * tq
    return [pltpu.VMEM((n_kv, tk, wide), F32),
            pltpu.VMEM((n_kv, tk, wide), BF16),
            pltpu.VMEM((n_kv, 1, wide), F32),
            pltpu.VMEM((n_kv, 1, wide), F32),
            pltpu.VMEM((n_kv, 1, wide), F32),
            pltpu.VMEM((n_kv, HEAD_DIM, wide), F32)]


def _dsa_prompt_kernel(qit_ref, wt_ref, ki_ref, qt_ref, k_ref, vt_ref, o_ref,
                       key_ref, j_ref, s_ref, p_ref, a_ref, m_ref, l_ref, acc_ref, *, top):
    tq, tk = ATT_TQ, ATT_TK
    i = pl.program_id(1)
    n_chunks = ((i + 1) * tq + tk - 1) // tk
    qpos = i * tq + lax.broadcasted_iota(jnp.int32, (1, tq), 1)
    row = lax.broadcasted_iota(jnp.int32, (tk, tq), 0)

    def score_chunk(c, carry):
        start = pl.multiple_of(c * tk, tk)
        kic = ki_ref[0, pl.ds(start, tk), :]
        s = jnp.zeros((tk, tq), F32)
        for h in range(IDX_HEADS):
            d = jnp.dot(kic, qit_ref[0, h], preferred_element_type=F32)
            s = s + wt_ref[0, h:h + 1, :] * jnp.maximum(d, 0.0)
        s = jnp.where(s == 0.0, 0.0, s)
        bits = pltpu.bitcast(s, jnp.int32)
        key = jnp.where(bits < 0, bits ^ INT_MAX, bits)
        key = jnp.where(start + row <= qpos, key, INT_MIN)
        key_ref[pl.ds(start, tk), :] = key
        return carry

    lax.fori_loop(0, n_chunks, score_chunk, 0)

    def count(pred):
        def body(c, acc):
            start = pl.multiple_of(c * tk, tk)
            return acc + _col_sum_i32(pred(key_ref[pl.ds(start, tk), :], start + row))
        acc = lax.fori_loop(0, n_chunks, body, jnp.zeros((SUBLANES, tq), jnp.int32))
        return jnp.sum(acc, axis=0, keepdims=True)

    def bis_cond(st):
        it, lo, hi, _ = st
        return ((it < BISECT_VALUE_STEPS + BISECT_INT_STEPS)
                & (jnp.max(jnp.where(hi != lo + 1, 1, 0)) > 0))

    def key_value(k):
        return pltpu.bitcast(jnp.where(k < 0, k ^ INT_MAX, k), F32)

    def bis_body(st):
        it, lo, hi, cnt_lo = st
        mid = (lo >> 1) + (hi >> 1) + (lo & hi & 1)
        vbits = pltpu.bitcast(0.5 * (key_value(lo) + key_value(hi)), jnp.int32)
        vmid = jnp.where(vbits < 0, vbits ^ INT_MAX, vbits)
        vmid_cap = jnp.where(it < BISECT_VALUE_STEPS, INT_MAX, INT_MIN)
        mid = jnp.where((vmid > lo) & (vmid < jnp.minimum(hi, vmid_cap)), vmid, mid)
        c = count(lambda kc, _: kc >= mid)
        ge = c >= top
        lo2 = jnp.where(ge, mid, lo)
        hi2 = jnp.where(c == top, mid + 1, jnp.where(ge, hi, mid))
        return it + 1, lo2, hi2, jnp.where(ge, c, cnt_lo)

    def max_body(c, acc):
        kc = key_ref[pl.ds(pl.multiple_of(c * tk, tk), tk), :]
        return jnp.maximum(acc, jnp.max(kc.reshape(tk // SUBLANES, SUBLANES, tq), axis=0))

    kmax = jnp.max(lax.fori_loop(0, n_chunks, max_body, jnp.full((SUBLANES, tq), INT_MIN, jnp.int32)),
                   axis=0, keepdims=True)
    _, t, _, cnt_t = lax.while_loop(
        bis_cond, lambda st: bis_body(bis_body(st)),
        (jnp.int32(0), jnp.full((1, tq), INT_MIN, jnp.int32), kmax + 1,
         jnp.zeros((1, tq), jnp.int32) + n_chunks * tk))

    need = (cnt_t > top) & (t > INT_MIN)
    j_ref[...] = jnp.full((1, tq), INT_MAX, jnp.int32)

    @pl.when(jnp.max(need.astype(jnp.int32)) > 0)
    def _():
        r = top - count(lambda kc, _: kc > t)

        def jb(_, st):
            lo, hi = st
            mid = (lo + hi) >> 1
            ok = count(lambda kc, kp: (kc == t) & (kp <= mid)) >= r
            return jnp.where(ok, lo, mid), jnp.where(ok, mid, hi)

        n_bits = int(math.ceil(math.log2(key_ref.shape[0]))) + 1
        _, hi = lax.fori_loop(0, n_bits, jb, (jnp.full((1, tq), -1, jnp.int32),
                                              jnp.zeros((1, tq), jnp.int32) + (n_chunks * tk - 1)))
        j_ref[...] = jnp.where(need, hi, INT_MAX)

    _attend_init(m_ref, l_ref, acc_ref)
    jcut = j_ref[...]

    def att_chunk(c, carry):
        start = pl.multiple_of(c * tk, tk)
        kc = key_ref[pl.ds(start, tk), :]
        sel = (kc > INT_MIN) & ((kc > t) | ((kc == t) & (start + row <= jcut)))
        sel2 = jnp.concatenate([sel] * A_GROUP, axis=1)
        _attend_chunk(qt_ref, k_ref, vt_ref, start, c, lambda kv: sel2, s_ref, p_ref, a_ref, m_ref, l_ref, acc_ref)
        return carry

    lax.fori_loop(0, n_chunks, att_chunk, 0)
    _attend_finish(o_ref, l_ref, acc_ref)


def dsa_prompt(qit, wt, ki, qt, k, vt, top):
    n_b, _, _, n_s = qt.shape
    tq, tk = ATT_TQ, ATT_TK
    return pl.pallas_call(
        functools.partial(_dsa_prompt_kernel, top=top),
        grid=(n_b, n_s // tq),
        in_specs=[pl.BlockSpec((1, IDX_HEADS, IDX_DIM, tq), lambda b, i: (b, 0, 0, i)),
                  pl.BlockSpec((1, IDX_HEADS, tq), lambda b, i: (b, 0, i)),
                  pl.BlockSpec((1, n_s, IDX_DIM), lambda b, i: (b, 0, 0)),
                  pl.BlockSpec((1, A_HEADS, HEAD_DIM, tq), lambda b, i: (b, 0, 0, i)),
                  pl.BlockSpec((1, A_KV, n_s, HEAD_DIM), lambda b, i: (b, 0, 0, 0)),
                  pl.BlockSpec((1, A_KV, n_s // tk, HEAD_DIM, tk), lambda b, i: (b, 0, 0, 0, 0))],
        out_specs=pl.BlockSpec((1, A_HEADS, HEAD_DIM, tq), lambda b, i: (b, 0, 0, i)),
        out_shape=jax.ShapeDtypeStruct((n_b, A_HEADS, HEAD_DIM, n_s), F32),
        scratch_shapes=[pltpu.VMEM((n_s, tq), jnp.int32),
                        pltpu.VMEM((1, tq), jnp.int32)] + _attend_scratch(A_KV, A_GROUP, tq, tk),
        compiler_params=pltpu.CompilerParams(dimension_semantics=("arbitrary", "arbitrary"),
                                             vmem_limit_bytes=VMEM_LIMIT),
        name="dsa_prompt",
    )(qit, wt, ki, qt, k, vt)


def _moba_prompt_kernel(qt_ref, k_ref, vt_ref, avg_ref, o_ref, kmean_ref, selm_ref,
                        s_ref, p_ref, a_ref, m_ref, l_ref, acc_ref, *, top):
    tq, tk = ATT_TQ, MOBA_BLOCK
    wide = B_GROUP * tq
    n_blk = avg_ref.shape[0]
    i = pl.program_id(1)
    n_own = (i * tq) // tk
    qpos = i * tq + (lax.broadcasted_iota(jnp.int32, (1, wide), 1) & (tq - 1))
    row = lax.broadcasted_iota(jnp.int32, (tk, wide), 0)
    nrow = lax.broadcasted_iota(jnp.int32, (n_blk, wide), 0)

    @pl.when(i == 0)
    def _():
        for kv in range(B_KV):
            kmean_ref[kv] = jnp.dot(avg_ref[...], k_ref[0, kv], preferred_element_type=F32).astype(BF16)

    for kv in range(B_KV):
        q2 = jnp.concatenate([qt_ref[0, kv * B_GROUP + g] for g in range(B_GROUP)], axis=1)
        g = jnp.dot(kmean_ref[kv], q2, preferred_element_type=F32)
        g = jnp.where(nrow < n_own, g, -jnp.inf)
        picked = jnp.zeros((n_blk, wide), jnp.bool_)
        for _ in range(top):
            mx = jnp.max(g, axis=0, keepdims=True)
            first = jnp.min(jnp.where(g == mx, nrow, n_blk), axis=0, keepdims=True)
            hit = nrow == first
            picked = picked | hit
            g = jnp.where(hit, -jnp.inf, g)
        selm_ref[kv] = jnp.where(picked & (nrow < n_own), 1.0, 0.0)

    _attend_init(m_ref, l_ref, acc_ref)

    def att_block(n, carry):
        start = pl.multiple_of(n * tk, tk)
        own_causal = ((jnp.zeros((1, wide), jnp.int32) + n) == n_own) & (start + row <= qpos)

        def sel_of(kv):
            picked_n = jnp.max(jnp.where(nrow == n, selm_ref[kv], 0.0), axis=0, keepdims=True) > 0.0
            return own_causal | picked_n

        _attend_chunk(qt_ref, k_ref, vt_ref, start, n, sel_of, s_ref, p_ref, a_ref, m_ref, l_ref, acc_ref)
        return carry

    lax.fori_loop(0, n_own + 1, att_block, 0)
    _attend_finish(o_ref, l_ref, acc_ref)


def moba_prompt(qt, k, vt):
    n_b, _, _, n_s = qt.shape
    tq, tk = ATT_TQ, MOBA_BLOCK
    n_blk = n_s // tk
    avg = (jnp.repeat(jnp.eye(n_blk, dtype=F32), tk, axis=1) / tk).astype(BF16)
    return pl.pallas_call(
        functools.partial(_moba_prompt_kernel, top=min(MOBA_TOPK, n_blk)),
        grid=(n_b, n_s // tq),
        in_specs=[pl.BlockSpec((1, B_HEADS, HEAD_DIM, tq), lambda b, i: (b, 0, 0, i)),
                  pl.BlockSpec((1, B_KV, n_s, HEAD_DIM), lambda b, i: (b, 0, 0, 0)),
                  pl.BlockSpec((1, B_KV, n_blk, HEAD_DIM, tk), lambda b, i: (b, 0, 0, 0, 0)),
                  pl.BlockSpec((n_blk, n_s), lambda b, i: (0, 0))],
        out_specs=pl.BlockSpec((1, B_HEADS, HEAD_DIM, tq), lambda b, i: (b, 0, 0, i)),
        out_shape=jax.ShapeDtypeStruct((n_b, B_HEADS, HEAD_DIM, n_s), F32),
        scratch_shapes=[pltpu.VMEM((B_KV, n_blk, HEAD_DIM), BF16),
                        pltpu.VMEM((B_KV, n_blk, B_GROUP * tq), F32)] + _attend_scratch(B_KV, B_GROUP, tq, tk),
        compiler_params=pltpu.CompilerParams(dimension_semantics=("arbitrary", "arbitrary"),
                                             vmem_limit_bytes=VMEM_LIMIT),
        name="moba_prompt",
    )(qt, k, vt, avg)


PEER_TT = 256
PEER_TE = 1024
PEER_NSORT = PEER_TOPK + 1
PEER_SVROWS = 24


def _extract_sorted(s, n, emit):
    rows, tt = s.shape
    rowi = lax.broadcasted_iota(jnp.int32, (rows, tt), 0)
    for r in range(n):
        mx = jnp.max(s, axis=0, keepdims=True)
        emit(r, mx)
        if r + 1 < n:
            first = jnp.min(jnp.where(s == mx, rowi, rows), axis=0, keepdims=True)
            s = jnp.where(rowi == first, -jnp.inf, s)


def _peer_select_kernel(x_ref, g_ref, wqt_ref, sk_ref, hb_ref, s1_ref, e1_ref, thr_ref, e0_ref,
                        q_ref, s_ref, sv_ref, top_ref):
    half = PEER_QDIM // 2
    x = x_ref[...]
    hb = (x * lax.rsqrt(jnp.mean(x * x, axis=-1, keepdims=True) + EPS) * g_ref[...]).astype(BF16)
    hb_ref[...] = hb
    q_ref[...] = lax.dot_general(wqt_ref[...], hb, (((1,), (1,)), ((), ())),
                                 preferred_element_type=F32).astype(BF16)
    sv_ref[...] = jnp.full(sv_ref.shape, -jnp.inf, F32)

    def head(h, carry):
        for p in range(2):
            hp = h * 2 + p
            s = jnp.dot(sk_ref[hp], q_ref[pl.ds(pl.multiple_of(hp * half, half), half), :],
                        preferred_element_type=F32)
            s_ref[p] = s

            def emit(r, v, p=p):
                sv_ref[p, r:r + 1, :] = v
            _extract_sorted(s, PEER_NSORT, emit)
        sv0, sv1 = sv_ref[0], sv_ref[1]
        cand = jnp.concatenate(
            [sv0[0:1] + sv1] + [sv0[a:a + 1] + sv1[0:8] for a in range(1, 8)] + [sv0[8:24] + sv1[0:1]], axis=0)

        def emit_top(r, v):
            top_ref[r:r + 1, :] = v
        _extract_sorted(cand, PEER_NSORT, emit_top)
        tau = 0.5 * (top_ref[PEER_TOPK - 1:PEER_TOPK, :] + top_ref[PEER_TOPK:PEER_TOPK + 1, :])
        cmax = sv0[0:1] + sv1[0:1]
        z = jnp.sum(jnp.where(cand >= tau, jnp.exp(cand - cmax), 0.0), axis=0, keepdims=True)
        s0, s1 = s_ref[0], s_ref[1]
        s1_ref[h] = s1
        e1_ref[h] = jnp.exp(s1 - sv1[0:1]) / z
        thr_ref[h] = tau - s0
        e0_ref[h] = jnp.exp(s0 - sv0[0:1])
        return carry

    lax.fori_loop(0, PEER_HEADS, head, 0)


def _peer_apply_kernel(hb_ref, u_ref, vt_ref, s1_ref, e1_ref, thr_ref, e0_ref, o_ref, g_ref):
    e = pl.program_id(1)
    tt = hb_ref.shape[0]
    hb = hb_ref[...]
    contrib = None
    for il in range(0, PEER_TE // PEER_NKEYS, 2):
        rows = slice(il * PEER_NKEYS, (il + 2) * PEER_NKEYS)
        a = lax.dot_general(u_ref[rows, :], hb, (((1,), (1,)), ((), ())), preferred_element_type=F32)
        for lt in range(tt // LANES):
            ls = slice(lt * LANES, (lt + 1) * LANES)
            acc = [jnp.zeros((PEER_NKEYS, LANES), F32) for _ in range(2)]
            for h in range(PEER_HEADS):
                s1 = s1_ref[h, :, ls]
                e1 = e1_ref[h, :, ls]
                for j in range(2):
                    thr_row = thr_ref[h, 0, il + j:il + j + 1, ls]
                    e0_row = e0_ref[h, 0, il + j:il + j + 1, ls]
                    acc[j] = acc[j] + jnp.where(s1 >= thr_row, e1 * e0_row, 0.0)
            for j in range(2):
                g_ref[j * PEER_NKEYS:(j + 1) * PEER_NKEYS, ls] = acc[j]
        ga = (g_ref[...] * jax.nn.gelu(a)).astype(BF16)
        part = jnp.dot(vt_ref[:, rows], ga, preferred_element_type=F32)
        contrib = part if contrib is None else contrib + part

    @pl.when(e == 0)
    def _():
        o_ref[...] = contrib

    @pl.when(e != 0)
    def _():
        o_ref[...] += contrib


def _transpose_cast_kernel(x_ref, o_ref):
    o_ref[...] = x_ref[...].T.astype(o_ref.dtype)


def transpose_cast(x, dtype, tile=512):
    n_r, n_c = x.shape
    return pl.pallas_call(
        _transpose_cast_kernel,
        grid=(n_r // tile, n_c // tile),
        in_specs=[pl.BlockSpec((tile, tile), lambda i, j: (i, j))],
        out_specs=pl.BlockSpec((tile, tile), lambda i, j: (j, i)),
        out_shape=jax.ShapeDtypeStruct((n_c, n_r), dtype),
        compiler_params=pltpu.CompilerParams(dimension_semantics=("arbitrary", "arbitrary")),
        name="transpose_cast",
    )(x)


def peer_tables(u_tab, v_tab):
    return u_tab.astype(BF16), transpose_cast(v_tab, BF16)


def peer(x, g_ffn, w_q, subkeys, u_b, vt_b):
    n_tok, d = x.shape
    n_exp = u_b.shape[0]
    tt = _pick_tile(n_tok, (PEER_TT, 128))
    half = PEER_QDIM // 2
    wqt = w_q.T.astype(BF16)
    sk = subkeys.reshape(PEER_HEADS * 2, PEER_NKEYS, half).astype(BF16)
    stat = jax.ShapeDtypeStruct((PEER_HEADS, PEER_NKEYS, n_tok), F32)
    stat_spec = pl.BlockSpec((PEER_HEADS, PEER_NKEYS, tt), lambda t: (0, 0, t))
    hb, s1, e1, thr, e0 = pl.pallas_call(
        _peer_select_kernel,
        grid=(n_tok // tt,),
        in_specs=[pl.BlockSpec((tt, d), lambda t: (t, 0)),
                  pl.BlockSpec((1, d), lambda t: (0, 0)),
                  pl.BlockSpec((PEER_HEADS * PEER_QDIM, d), lambda t: (0, 0)),
                  pl.BlockSpec((PEER_HEADS * 2, PEER_NKEYS, half), lambda t: (0, 0, 0))],
        out_specs=[pl.BlockSpec((tt, d), lambda t: (t, 0)), stat_spec, stat_spec, stat_spec, stat_spec],
        out_shape=[jax.ShapeDtypeStruct((n_tok, d), BF16), stat, stat, stat, stat],
        scratch_shapes=[pltpu.VMEM((PEER_HEADS * PEER_QDIM, tt), BF16),
                        pltpu.VMEM((2, PEER_NKEYS, tt), F32),
                        pltpu.VMEM((2, PEER_SVROWS, tt), F32),
                        pltpu.VMEM((PEER_SVROWS, tt), F32)],
        compiler_params=pltpu.CompilerParams(dimension_semantics=("arbitrary",),
                                             vmem_limit_bytes=VMEM_LIMIT),
        name="peer_select",
    )(x, g_ffn.astype(F32).reshape(1, d), wqt, sk)

    n_il = PEER_TE // PEER_NKEYS
    thr4 = thr.reshape(PEER_HEADS, PEER_NKEYS // n_il, n_il, n_tok)
    e04 = e0.reshape(PEER_HEADS, PEER_NKEYS // n_il, n_il, n_tok)
    stat2 = pl.BlockSpec((PEER_HEADS, PEER_NKEYS, tt), lambda t, e: (0, 0, t))
    stat4 = pl.BlockSpec((PEER_HEADS, 1, n_il, tt), lambda t, e: (0, e, 0, t))
    out_t = pl.pallas_call(
        _peer_apply_kernel,
        grid=(n_tok // tt, n_exp // PEER_TE),
        in_specs=[pl.BlockSpec((tt, d), lambda t, e: (t, 0)),
                  pl.BlockSpec((PEER_TE, d), lambda t, e: (e, 0)),
                  pl.BlockSpec((d, PEER_TE), lambda t, e: (0, e)),
                  stat2, stat2, stat4, stat4],
        out_specs=pl.BlockSpec((d, tt), lambda t, e: (0, t)),
        out_shape=jax.ShapeDtypeStruct((d, n_tok), F32),
        scratch_shapes=[pltpu.VMEM((2 * PEER_NKEYS, tt), F32)],
        compiler_params=pltpu.CompilerParams(dimension_semantics=("arbitrary", "arbitrary"),
                                             vmem_limit_bytes=VMEM_LIMIT),
        name="peer_apply",
    )(hb, u_b, vt_b, s1, e1, thr4, e04)
    return out_t.T


MLSTM_CHUNK = 128
HIGHEST = lax.Precision.HIGHEST


def _mlstm_kernel(q_ref, kt_ref, v_ref, co_ref, if_ref, gc_ref, s0_ref, m0_ref,
                  oc_ref, s_out_ref, m_out_ref, s_ref, m_ref, *, valid):
    L, D = MLSTM_CHUNK, C_HD
    c = pl.program_id(1)

    @pl.when(c == 0)
    def _():
        s_ref[...] = s0_ref[0]
        m_ref[...] = m0_ref[0]

    li = lax.broadcasted_iota(jnp.int32, (L, L), 0)
    si = lax.broadcasted_iota(jnp.int32, (L, L), 1)
    tri = si <= li
    upper = jnp.where(li <= si, 1.0, 0.0)
    ones_ll = jnp.ones((L, L), F32)
    live = c * L + lax.broadcasted_iota(jnp.int32, (1, L), 1) < valid
    ones_col = jnp.where(lax.broadcasted_iota(jnp.int32, (L, D), 1) == 0, 1.0, 0.0).astype(BF16)

    for h in range(C_HEADS):
        lanes = slice(h * D, (h + 1) * D)
        logf = jnp.where(live, jax.nn.log_sigmoid(if_ref[0, C_HEADS + h:C_HEADS + h + 1, :]), 0.0)
        i_row = jnp.where(live, if_ref[0, h:h + 1, :], -jnp.inf)
        f_b = jnp.broadcast_to(logf, (L, L))
        b_col_b = jnp.dot(jnp.where(tri, f_b, 0.0), ones_ll, precision=HIGHEST,
                          preferred_element_type=F32)
        b_row = jnp.dot(f_b[0:SUBLANES], upper, precision=HIGHEST,
                        preferred_element_type=F32)[0:1]
        b_col = b_col_b[:, 0:1]
        dmat = jnp.where(tri, b_col_b - b_row + i_row, -jnp.inf)
        m_prev = m_ref[h][0:1, 0:1]
        m_t = jnp.maximum(m_prev + b_col, jnp.max(dmat, axis=1, keepdims=True))
        inter = jnp.exp(m_prev + b_col - m_t)
        qb = q_ref[0, :, lanes].astype(BF16)
        kt = kt_ref[0, lanes, :]
        vaug = jnp.concatenate([v_ref[0, :, lanes].astype(BF16), ones_col], axis=1)
        s = jnp.dot(qb, kt.astype(BF16), preferred_element_type=F32) * jnp.exp(dmat - m_t)
        state = s_ref[h]
        tot = (jnp.dot(s.astype(BF16), vaug, preferred_element_type=F32)
               + inter * jnp.dot(qb, state.astype(BF16), preferred_element_type=F32))
        hh = tot[:, :D] / jnp.maximum(jnp.abs(tot[:, D:D + 1]), jnp.exp(-m_t))
        y = hh * lax.rsqrt(jnp.mean(hh * hh, axis=1, keepdims=True) + EPS) * gc_ref[0:1, lanes]
        oc_ref[0, :, lanes] = y * jax.nn.sigmoid(co_ref[0, :, lanes])
        m_new = m_t[L - 1:L, :]
        b_last = b_col[L - 1:L, :]
        wk_row = jnp.exp(b_last - b_row + i_row - m_new)
        s_ref[h] = (jnp.exp(m_prev + b_last - m_new) * state
                    + jnp.dot((kt * wk_row).astype(BF16), vaug, preferred_element_type=F32))
        m_ref[h] = jnp.broadcast_to(m_new, (SUBLANES, LANES))

    @pl.when(c == pl.num_programs(1) - 1)
    def _():
        s_out_ref[0] = s_ref[...]
        m_out_ref[0] = m_ref[...]


def mlstm(cq, ck, cv, co, ci, cf, g_c_out, C0, n0, m0):
    n_b, n_s, width = cq.shape
    L, D = MLSTM_CHUNK, C_HD
    n_c = -(-n_s // L)
    pad = n_c * L - n_s

    def padded(a):
        return jnp.pad(a, ((0, 0), (0, pad), (0, 0))) if pad else a

    kt = jnp.swapaxes(padded(ck) * (D ** -0.5), 1, 2)
    gates = jnp.swapaxes(padded(jnp.concatenate([ci, cf], axis=-1)), 1, 2)
    oc, C1, n1, m1 = mlstm_call((padded(cq), 0), kt, (padded(cv), 0), (padded(co), 0), gates,
                                g_c_out, C0, n0, m0, n_s)
    return oc[:, :n_s], C1, n1, m1


def mlstm_call(q_src, kt, v_src, co_src, gates, g_c_out, C0, n0, m0, n_s):
    n_b, width, s_pad = kt.shape
    L, D = MLSTM_CHUNK, C_HD
    n_c = s_pad // L
    s0 = jnp.concatenate([C0, n0[..., None], jnp.zeros((n_b, C_HEADS, D, D - 1), F32)], axis=-1)
    m0b = jnp.broadcast_to(m0[:, :, None, None], (n_b, C_HEADS, SUBLANES, LANES))

    def tok_spec(off):
        return pl.BlockSpec((1, L, width), lambda b, c: (b, c, off))

    tok = tok_spec(0)
    oc, s1, m1 = pl.pallas_call(
        functools.partial(_mlstm_kernel, valid=n_s),
        grid=(n_b, n_c),
        in_specs=[tok_spec(q_src[1]), pl.BlockSpec((1, width, L), lambda b, c: (b, 0, c)),
                  tok_spec(v_src[1]), tok_spec(co_src[1]),
                  pl.BlockSpec((1, 2 * C_HEADS, L), lambda b, c: (b, 0, c)),
                  pl.BlockSpec((1, width), lambda b, c: (0, 0)),
                  pl.BlockSpec((1, C_HEADS, D, 2 * D), lambda b, c: (b, 0, 0, 0)),
                  pl.BlockSpec((1, C_HEADS, SUBLANES, LANES), lambda b, c: (b, 0, 0, 0))],
        out_specs=[tok,
                   pl.BlockSpec((1, C_HEADS, D, 2 * D), lambda b, c: (b, 0, 0, 0)),
                   pl.BlockSpec((1, C_HEADS, SUBLANES, LANES), lambda b, c: (b, 0, 0, 0))],
        out_shape=[jax.ShapeDtypeStruct((n_b, n_c * L, width), F32),
                   jax.ShapeDtypeStruct((n_b, C_HEADS, D, 2 * D), F32),
                   jax.ShapeDtypeStruct((n_b, C_HEADS, SUBLANES, LANES), F32)],
        scratch_shapes=[pltpu.VMEM((C_HEADS, D, 2 * D), F32),
                        pltpu.VMEM((C_HEADS, SUBLANES, LANES), F32)],
        compiler_params=pltpu.CompilerParams(dimension_semantics=("arbitrary", "arbitrary"),
                                             vmem_limit_bytes=VMEM_LIMIT),
        name="mlstm",
    )(q_src[0], kt, v_src[0], co_src[0], gates, g_c_out.astype(F32).reshape(1, width), s0, m0b)
    return oc, s1[..., :D], s1[..., D], m1[:, :, 0, 0]


def _row_sum_i32(mask):
    return jnp.sum(mask.astype(jnp.int32), axis=1, keepdims=True)


def _masked_softmax_pv(logits, sel, v_pages):
    m = jnp.max(jnp.where(sel, logits, NEG_BIG), axis=1, keepdims=True)
    p = jnp.where(sel, jnp.exp(logits - m), 0.0)
    l = jnp.sum(p, axis=1, keepdims=True)
    pb = p.astype(BF16)
    acc = None
    for j, vp in enumerate(v_pages):
        part = jnp.dot(pb[:, j * PAGE_SIZE:(j + 1) * PAGE_SIZE], vp, preferred_element_type=F32)
        acc = part if acc is None else acc + part
    return acc / l


def _dsa_sample_kernel(pt_ref, *refs, n_pages, top, past_len):
    del pt_ref
    n_all = n_pages + 1
    idx_refs = refs[0:n_all]
    k_refs = refs[n_all:2 * n_all]
    v_refs = refs[2 * n_all:3 * n_all]
    qi_ref, w_ref, qbd_ref, o_ref = refs[3 * n_all:3 * n_all + 4]
    n_q = w_ref.shape[2]
    n_keys = n_all * PAGE_SIZE
    rows = IDX_HEADS * n_q

    w_b = jnp.broadcast_to(w_ref[0], (IDX_HEADS, n_q, PAGE_SIZE))
    qi = qi_ref[0]
    pieces = []
    for j in range(n_all):
        kip = idx_refs[j][...].reshape(PAGE_SIZE, IDX_DIM).astype(BF16)
        d = lax.dot_general(qi, kip, (((1,), (1,)), ((), ())), preferred_element_type=F32)
        pieces.append(jnp.sum(w_b * jnp.maximum(d, 0.0).reshape(IDX_HEADS, n_q, PAGE_SIZE), axis=0))
    s = jnp.concatenate(pieces, axis=1)
    s = jnp.where(s == 0.0, 0.0, s)
    bits = pltpu.bitcast(s, jnp.int32)
    kpos = lax.broadcasted_iota(jnp.int32, (n_q, n_keys), 1)
    qpos = past_len + lax.broadcasted_iota(jnp.int32, (n_q, n_keys), 0)
    key = jnp.where(kpos <= qpos, jnp.where(bits < 0, bits ^ INT_MAX, bits), INT_MIN)

    def bis_cond(st):
        it, lo, hi, _ = st
        return (it < BISECT_INT_STEPS) & (jnp.max(jnp.where(hi != lo + 1, 1, 0)) > 0)

    def bis_body(st):
        it, lo, hi, cnt_lo = st
        mid = (lo >> 1) + (hi >> 1) + (lo & hi & 1)
        c = _row_sum_i32(key >= mid)
        ge = c >= top
        return (it + 1, jnp.where(ge, mid, lo), jnp.where(c == top, mid + 1, jnp.where(ge, hi, mid)),
                jnp.where(ge, c, cnt_lo))

    _, t, _, cnt_t = lax.while_loop(
        bis_cond, bis_body,
        (jnp.int32(0), jnp.full((n_q, 1), INT_MIN, jnp.int32), jnp.full((n_q, 1), INT_MAX, jnp.int32),
         jnp.full((n_q, 1), n_keys, jnp.int32)))

    need = (cnt_t > top) & (t > INT_MIN)
    r = top - _row_sum_i32(key > t)

    def jb(_, st):
        lo, hi = st
        mid = (lo + hi) >> 1
        ok = _row_sum_i32((key == t) & (kpos <= mid)) >= r
        return jnp.where(ok, lo, mid), jnp.where(ok, mid, hi)

    n_bits = int(math.ceil(math.log2(n_keys))) + 1
    _, jhi = lax.fori_loop(0, n_bits, jb, (jnp.full((n_q, 1), -1, jnp.int32),
                                           jnp.full((n_q, 1), n_keys - 1, jnp.int32)))
    jcut = jnp.where(need, jhi, INT_MAX)
    sel = (key > INT_MIN) & ((key > t) | ((key == t) & (kpos <= jcut)))

    qbd = qbd_ref[0]
    logits = jnp.concatenate(
        [lax.dot_general(qbd, k_refs[j][...].reshape(PAGE_SIZE, A_KV * HEAD_DIM).astype(BF16),
                         (((1,), (1,)), ((), ())), preferred_element_type=F32) for j in range(n_all)], axis=1)
    sel_rows = jnp.broadcast_to(sel[None], (A_HEADS, n_q, n_keys)).reshape(rows, n_keys)
    v_pages = [v_refs[j][...].reshape(PAGE_SIZE, A_KV * HEAD_DIM).astype(BF16) for j in range(n_all)]
    o_ref[0] = _masked_softmax_pv(logits, sel_rows, v_pages)


def _moba_sample_kernel(pt_ref, *refs, n_pages):
    del pt_ref
    n_all = n_pages + 1
    k_refs = refs[0:n_all]
    v_refs = refs[n_all:2 * n_all]
    qbd_ref, o_ref = refs[2 * n_all:2 * n_all + 2]
    rows = qbd_ref.shape[1]
    n_q = rows // B_HEADS
    width = B_KV * HEAD_DIM
    pages_per_blk = MOBA_BLOCK // PAGE_SIZE
    n_blk = n_pages // pages_per_blk
    top = min(MOBA_TOPK, n_blk + 1)
    qbd = qbd_ref[0]

    k_pages = [k_refs[j][...].reshape(PAGE_SIZE, width) for j in range(n_all)]
    means = []
    for n in range(n_blk):
        tot = k_pages[n * pages_per_blk]
        for j in range(1, pages_per_blk):
            tot = tot + k_pages[n * pages_per_blk + j]
        means.append(jnp.sum(tot, axis=0, keepdims=True) * (1.0 / MOBA_BLOCK))
    kmean = jnp.concatenate(means + [jnp.zeros((LANES - n_blk, width), F32)], axis=0).astype(BF16)
    gate = lax.dot_general(qbd, kmean, (((1,), (1,)), ((), ())), preferred_element_type=F32)
    lane = lax.broadcasted_iota(jnp.int32, (rows, LANES), 1)
    gate = jnp.where(lane < n_blk, gate, -jnp.inf)
    picked = jnp.zeros((rows, LANES), jnp.bool_)
    for _ in range(top):
        mx = jnp.max(gate, axis=1, keepdims=True)
        first = jnp.min(jnp.where(gate == mx, lane, LANES), axis=1, keepdims=True)
        hit = lane == first
        picked = picked | hit
        gate = jnp.where(hit, -jnp.inf, gate)
    picked = jnp.where(picked & (lane < n_blk), 1.0, 0.0)

    kb = [kp.astype(BF16) for kp in k_pages]
    logits = jnp.concatenate(
        [lax.dot_general(qbd, kb[j], (((1,), (1,)), ((), ())), preferred_element_type=F32)
         for j in range(n_all)], axis=1)
    q_of_row = lax.broadcasted_iota(jnp.int32, (rows, PAGE_SIZE), 0) % n_q
    own = lax.broadcasted_iota(jnp.int32, (rows, PAGE_SIZE), 1) <= q_of_row
    sel = jnp.concatenate(
        [jnp.broadcast_to(picked[:, j // pages_per_blk:j // pages_per_blk + 1] > 0.0, (rows, PAGE_SIZE))
         for j in range(n_pages)] + [own], axis=1)
    v_pages = [v_refs[j][...].reshape(PAGE_SIZE, width).astype(BF16) for j in range(n_all)]
    o_ref[0] = _masked_softmax_pv(logits, sel, v_pages)


def _block_diag_queries(q):
    n_b, n_q, n_kv, n_g, hd = q.shape
    qh = jnp.transpose(q, (0, 2, 3, 1, 4))
    eye = jnp.eye(n_kv, dtype=q.dtype)
    bd = qh[:, :, :, :, None, :] * eye[None, :, None, None, :, None]
    return bd.reshape(n_b, n_kv * n_g * n_q, n_kv * hd).astype(BF16)


def _take_diag_heads(o, n_q, n_kv, n_g):
    n_b = o.shape[0]
    o6 = o.reshape(n_b, n_kv, n_g, n_q, n_kv, HEAD_DIM)
    d = jnp.stack([o6[:, kv, :, :, kv, :] for kv in range(n_kv)], axis=1)
    return jnp.transpose(d, (0, 3, 1, 2, 4)).reshape(n_b, n_q, n_kv * n_g * HEAD_DIM)


def _page_specs(layer, n_pages, tail_shape):
    def spec(p):
        return pl.BlockSpec((1, 1, PAGE_SIZE) + tail_shape,
                            lambda b, pt, p=p: (layer, pt[b, p]) + (0,) * (1 + len(tail_shape)))
    return [spec(p) for p in range(n_pages)]


def _new_page(a):
    return jnp.pad(a, ((0, 0), (0, PAGE_SIZE - a.shape[1]), (0, 0)))


def dsa_sample(layer, page_table, cache_idx, cache_k, cache_v, aq, aiq, aiw, ak, av, aik):
    n_b, n_q = aq.shape[:2]
    n_pages = page_table.shape[1]
    past_len = n_pages * PAGE_SIZE
    width = A_KV * HEAD_DIM
    ck = cache_k.reshape(cache_k.shape[:3] + (width,))
    cv = cache_v.reshape(cache_v.shape[:3] + (width,))
    qbd = _block_diag_queries(aq * (HEAD_DIM ** -0.5))
    qi = jnp.transpose(aiq, (0, 2, 1, 3)).reshape(n_b, IDX_HEADS * n_q, IDX_DIM).astype(BF16)
    w = jnp.transpose(aiw, (0, 2, 1))[..., None]
    new = lambda tail: pl.BlockSpec((1, PAGE_SIZE) + tail, lambda b, pt: (b, 0) + (0,) * len(tail))
    full = lambda shape: pl.BlockSpec((1,) + shape, lambda b, pt: (b,) + (0,) * len(shape))
    rows = A_HEADS * n_q
    out = pl.pallas_call(
        functools.partial(_dsa_sample_kernel, n_pages=n_pages, top=min(DSA_TOPK, (past_len + n_q) // 4),
                          past_len=past_len),
        grid_spec=pltpu.PrefetchScalarGridSpec(
            num_scalar_prefetch=1, grid=(n_b,),
            in_specs=(_page_specs(layer, n_pages, (IDX_DIM,)) + [new((IDX_DIM,))]
                      + _page_specs(layer, n_pages, (width,)) + [new((width,))]
                      + _page_specs(layer, n_pages, (width,)) + [new((width,))]
                      + [full((IDX_HEADS * n_q, IDX_DIM)), full((IDX_HEADS, n_q, 1)), full((rows, width))]),
            out_specs=pl.BlockSpec((1, rows, width), lambda b, pt: (b, 0, 0))),
        out_shape=jax.ShapeDtypeStruct((n_b, rows, width), F32),
        compiler_params=pltpu.CompilerParams(dimension_semantics=("arbitrary",),
                                             vmem_limit_bytes=VMEM_LIMIT),
        name="dsa_sample",
    )(page_table, *([cache_idx] * n_pages), _new_page(aik),
      *([ck] * n_pages), _new_page(ak.reshape(n_b, n_q, width)),
      *([cv] * n_pages), _new_page(av.reshape(n_b, n_q, width)), qi, w, qbd)
    return _take_diag_heads(out, n_q, A_KV, A_GROUP)


def moba_sample(layer, page_table, cache_k, cache_v, bq, bk, bv):
    n_b, n_q = bq.shape[:2]
    n_pages = page_table.shape[1]
    width = B_KV * HEAD_DIM
    ck = cache_k.reshape(cache_k.shape[:3] + (width,))
    cv = cache_v.reshape(cache_v.shape[:3] + (width,))
    qbd = _block_diag_queries(bq * (HEAD_DIM ** -0.5))
    new = pl.BlockSpec((1, PAGE_SIZE, width), lambda b, pt: (b, 0, 0))
    rows = B_HEADS * n_q
    out = pl.pallas_call(
        functools.partial(_moba_sample_kernel, n_pages=n_pages),
        grid_spec=pltpu.PrefetchScalarGridSpec(
            num_scalar_prefetch=1, grid=(n_b,),
            in_specs=(_page_specs(layer, n_pages, (width,)) + [new]
                      + _page_specs(layer, n_pages, (width,)) + [new]
                      + [pl.BlockSpec((1, rows, width), lambda b, pt: (b, 0, 0))]),
            out_specs=pl.BlockSpec((1, rows, width), lambda b, pt: (b, 0, 0))),
        out_shape=jax.ShapeDtypeStruct((n_b, rows, width), F32),
        compiler_params=pltpu.CompilerParams(dimension_semantics=("arbitrary",),
                                             vmem_limit_bytes=VMEM_LIMIT),
        name="moba_sample",
    )(page_table, *([ck] * n_pages), _new_page(bk.reshape(n_b, n_q, width)),
      *([cv] * n_pages), _new_page(bv.reshape(n_b, n_q, width)), qbd)
    return _take_diag_heads(out, n_q, B_KV, B_GROUP)


def _rms(x, g):
    xf = x.astype(F32)
    y = xf * lax.rsqrt(jnp.mean(xf * xf, axis=-1, keepdims=True) + EPS)
    return (y * g.astype(F32)).astype(x.dtype)


def _rope(x, pos):
    half = x.shape[-1] // 2
    freqs = jnp.power(jnp.float32(ROPE_THETA), -jnp.arange(half, dtype=F32) / half)
    ang = pos.astype(F32)[:, None] * freqs[None, :]
    cos = jnp.cos(ang)[None, :, None, :]
    sin = jnp.sin(ang)[None, :, None, :]
    xf = x.astype(F32)
    x1, x2 = xf[..., :half], xf[..., half:]
    return jnp.concatenate([x1 * cos - x2 * sin, x1 * sin + x2 * cos], axis=-1).astype(x.dtype)


def _to_blocks(a, nb):
    return jnp.moveaxis(a.reshape(a.shape[0], nb, a.shape[1] // nb, *a.shape[2:]), 1, 0)


def _from_blocks(a):
    a = jnp.moveaxis(a, 0, 1)
    return a.reshape(a.shape[0], a.shape[1] * a.shape[2], *a.shape[3:])


def _dsa_attend(q, qi, wi, k_all, v_all, ki_all, q_pos):
    n_q = q.shape[1]
    n_k = k_all.shape[1]
    top = min(DSA_TOPK, n_k // 4)
    blk = Q_BLOCK if n_q % Q_BLOCK == 0 else n_q
    nb = n_q // blk
    k_pos = jnp.arange(n_k, dtype=jnp.int32)
    scale = HEAD_DIM ** -0.5
    ki_f = ki_all.astype(F32)

    def block(args):
        qb, qib, wib, pb = args
        s = jax.nn.relu(jnp.einsum('bqhd,bld->bqhl', qib.astype(F32), ki_f))
        score = jnp.einsum('bqhl,bqh->bql', s, wib.astype(F32))
        score = jnp.where(k_pos[None, None, :] <= pb[None, :, None], score, -jnp.inf)
        _, sel = lax.top_k(score, top)
        valid = sel <= pb[None, :, None]
        kg = jax.vmap(lambda kk, ii: kk[ii])(k_all, sel)
        vg = jax.vmap(lambda vv, ii: vv[ii])(v_all, sel)
        logits = jnp.einsum('bqkgd,bqskd->bqkgs', qb.astype(F32), kg.astype(F32)) * scale
        logits = jnp.where(valid[:, :, None, None, :], logits, -jnp.inf)
        p = jax.nn.softmax(logits, axis=-1)
        return jnp.einsum('bqkgs,bqskd->bqkgd', p, vg.astype(F32)).astype(q.dtype)

    out = lax.map(block, (_to_blocks(q, nb), _to_blocks(qi, nb), _to_blocks(wi, nb), q_pos.reshape(nb, blk)))
    return _from_blocks(out)


def _moba_attend(q, k_all, v_all, q_pos):
    n_b, n_q = q.shape[:2]
    n_k = k_all.shape[1]
    n_blk = -(-n_k // MOBA_BLOCK)
    pad = n_blk * MOBA_BLOCK - n_k
    kb = jnp.pad(k_all, ((0, 0), (0, pad), (0, 0), (0, 0))).reshape(n_b, n_blk, MOBA_BLOCK, B_KV, HEAD_DIM)
    vb = jnp.pad(v_all, ((0, 0), (0, pad), (0, 0), (0, 0))).reshape(n_b, n_blk, MOBA_BLOCK, B_KV, HEAD_DIM)
    k_mean = jnp.mean(kb.astype(F32), axis=2)
    kb_h = jnp.moveaxis(kb, 3, 1)
    vb_h = jnp.moveaxis(vb, 3, 1)
    top = min(MOBA_TOPK, n_blk)
    blk = MOBA_Q_BLOCK if n_q % MOBA_Q_BLOCK == 0 else n_q
    nb = n_q // blk
    blk_ids = jnp.arange(n_blk, dtype=jnp.int32)
    in_blk = jnp.arange(MOBA_BLOCK, dtype=jnp.int32)
    b_ix = jnp.arange(n_b)
    kv_ix = jnp.arange(B_KV)
    scale = HEAD_DIM ** -0.5

    def block(args):
        qb, pb = args
        qf = qb.astype(F32)
        own = pb // MOBA_BLOCK
        gate = jnp.einsum('bqkgd,bnkd->bqkgn', qf, k_mean)
        past = blk_ids[None, :] < own[:, None]
        gate = jnp.where(past[None, :, None, None, :], gate, -jnp.inf)
        _, sel = lax.top_k(gate, top)
        sel_ok = sel < own[None, :, None, None, None]
        bi = b_ix[:, None, None, None, None]
        ki = kv_ix[None, None, :, None, None]
        kg = kb_h[bi, ki, sel]
        vg = vb_h[bi, ki, sel]
        l_sel = jnp.einsum('bqkgd,bqkgnsd->bqkgns', qf, kg.astype(F32)) * scale
        l_sel = jnp.where(sel_ok[..., None], l_sel, -jnp.inf).reshape(n_b, blk, B_KV, B_GROUP, top * MOBA_BLOCK)
        ko = kb[b_ix[:, None], own[None, :]]
        vo = vb[b_ix[:, None], own[None, :]]
        l_own = jnp.einsum('bqkgd,bqskd->bqkgs', qf, ko.astype(F32)) * scale
        own_pos = own[:, None] * MOBA_BLOCK + in_blk[None, :]
        l_own = jnp.where((own_pos <= pb[:, None])[None, :, None, None, :], l_own, -jnp.inf)
        p = jax.nn.softmax(jnp.concatenate([l_sel, l_own], axis=-1), axis=-1)
        p_sel = p[..., :top * MOBA_BLOCK].reshape(n_b, blk, B_KV, B_GROUP, top, MOBA_BLOCK)
        p_own = p[..., top * MOBA_BLOCK:]
        out = (jnp.einsum('bqkgns,bqkgnsd->bqkgd', p_sel, vg.astype(F32))
               + jnp.einsum('bqkgs,bqskd->bqkgd', p_own, vo.astype(F32)))
        return out.astype(q.dtype)

    out = lax.map(block, (_to_blocks(q, nb), q_pos.reshape(nb, blk)))
    return _from_blocks(out)


def _mlstm(q, k, v, i_pre, f_pre, C0, n0, m0):
    n_s = q.shape[1]
    L = C_CHUNK if n_s % C_CHUNK == 0 else n_s
    nc = n_s // L
    q = q.astype(F32)
    k = k.astype(F32) * (C_HD ** -0.5)
    v = v.astype(F32)
    i_pre = i_pre.astype(F32)
    logf = jax.nn.log_sigmoid(f_pre.astype(F32))
    tri = jnp.tril(jnp.ones((L, L), dtype=bool))

    def step(carry, xs):
        C, n, m = carry
        qc, kc, vc, ic, fc = xs
        b = jnp.moveaxis(jnp.cumsum(fc, axis=1), 1, 2)
        it = jnp.moveaxis(ic, 1, 2)
        dmat = jnp.where(tri, b[..., :, None] - b[..., None, :] + it[..., None, :], -jnp.inf)
        m_t = jnp.maximum(m[..., None] + b, jnp.max(dmat, axis=-1))
        inter = jnp.exp(m[..., None] + b - m_t)
        s = jnp.einsum('blhd,bshd->bhls', qc, kc) * jnp.exp(dmat - m_t[..., None])
        num = jnp.einsum('bhls,bshd->bhld', s, vc) + inter[..., None] * jnp.einsum('blhd,bhde->bhle', qc, C)
        den = jnp.sum(s, axis=-1) + inter * jnp.einsum('blhd,bhd->bhl', qc, n)
        h = num / jnp.maximum(jnp.abs(den), jnp.exp(-m_t))[..., None]
        m_new = m_t[..., -1]
        decay = jnp.exp(m + b[..., -1] - m_new)
        wk = jnp.exp(b[..., -1:] - b + it - m_new[..., None])
        C_new = decay[..., None, None] * C + jnp.einsum('bhs,bshd,bshe->bhde', wk, kc, vc)
        n_new = decay[..., None] * n + jnp.einsum('bhs,bshd->bhd', wk, kc)
        return (C_new, n_new, m_new), jnp.moveaxis(h, 1, 2)

    xs = (_to_blocks(q, nc), _to_blocks(k, nc), _to_blocks(v, nc), _to_blocks(i_pre, nc), _to_blocks(logf, nc))
    (C, n, m), h = lax.scan(step, (C0.astype(F32), n0.astype(F32), m0.astype(F32)), xs)
    return _from_blocks(h), C, n, m


def _peer(h_in, g_ffn, w_q, subkeys, u_tab, v_tab):
    n_b, n_s, d = h_in.shape
    hq = _mm3(h_in, w_q, g_ffn)
    hn = _rms(h_in, g_ffn)
    hf = hn.reshape(n_b * n_s, d)
    qf = hq.reshape(n_b * n_s, -1)
    n_tok = hf.shape[0]
    blk = min(TOKEN_BLOCK, n_tok)
    nb = n_tok // blk
    half = PEER_QDIM // 2
    sk = subkeys.astype(F32)

    def block(args):
        hb, qb = args
        q = qb.reshape(blk, PEER_HEADS, 2, half).astype(F32)
        s = jnp.einsum('nhpd,hpkd->nhpk', q, sk)
        sv, si = lax.top_k(s, PEER_TOPK)
        cand = (sv[:, :, 0, :, None] + sv[:, :, 1, None, :]).reshape(blk, PEER_HEADS, PEER_TOPK * PEER_TOPK)
        cid = (si[:, :, 0, :, None] * PEER_NKEYS + si[:, :, 1, None, :]).reshape(blk, PEER_HEADS, PEER_TOPK * PEER_TOPK)
        tv, ti = lax.top_k(cand, PEER_TOPK)
        eid = jnp.take_along_axis(cid, ti, axis=-1)
        g = jax.nn.softmax(tv, axis=-1)
        u = u_tab[eid].astype(F32)
        a = jax.nn.gelu(jnp.einsum('nd,nhkd->nhk', hb.astype(F32), u))
        out = jnp.einsum('nhk,nhkd->nd', g * a, v_tab[eid].astype(F32))
        return out.astype(h_in.dtype)

    out = lax.map(block, (hf.reshape(nb, blk, d), qf.reshape(nb, blk, -1))).reshape(nb * blk, d)
    return out.reshape(n_b, n_s, d)


def _heads_t(a):
    return jnp.transpose(a.astype(BF16), (0, 2, 3, 1))


def _values_t(v, tk):
    n_b, n_s, n_kv, d = v.shape
    vt = jnp.transpose(v.astype(BF16), (0, 2, 3, 1)).reshape(n_b, n_kv, d, n_s // tk, tk)
    return jnp.transpose(vt, (0, 1, 3, 2, 4))


def _dsa_prompt_glue(aq, aiq, aiw, ak, av, aik):
    n_b, n_s = aq.shape[:2]
    scale = HEAD_DIM ** -0.5
    qt = _heads_t(aq.reshape(n_b, n_s, A_HEADS, HEAD_DIM) * scale)
    ot = dsa_prompt(_heads_t(aiq), jnp.transpose(aiw, (0, 2, 1)), aik.astype(BF16), qt,
                    jnp.transpose(ak.astype(BF16), (0, 2, 1, 3)), _values_t(av, ATT_TK),
                    top=min(DSA_TOPK, n_s // 4))
    return jnp.transpose(ot, (0, 3, 1, 2)).reshape(n_b, n_s, A_WIDTH)


def _moba_prompt_glue(bq, bk, bv):
    n_b, n_s = bq.shape[:2]
    scale = HEAD_DIM ** -0.5
    qt = _heads_t(bq.reshape(n_b, n_s, B_HEADS, HEAD_DIM) * scale)
    ot = moba_prompt(qt, jnp.transpose(bk.astype(BF16), (0, 2, 1, 3)), _values_t(bv, MOBA_BLOCK))
    return jnp.transpose(ot, (0, 3, 1, 2)).reshape(n_b, n_s, B_WIDTH)


def _mix_ffn(x, p_l, lw, oa, ob, oc, g_a, g_b, g_c):
    n_b, n_s, _ = x.shape
    mix = (jax.nn.sigmoid(g_a) * _mm3(oa, lw['w_br_a'])
           + jax.nn.sigmoid(g_b) * _mm3(ob, lw['w_br_b'])
           + jax.nn.sigmoid(g_c) * _mm3(oc, lw['w_br_c']))
    x = x + _mm3(mix, lw['w_out'])
    x = x + peer(x.reshape(n_b * n_s, -1), lw['g_ffn'], lw['w_peer_q'], lw['peer_subkeys'],
                 lw['peer_u_b'], lw['peer_vt_b']).reshape(x.shape)
    return x + jax.nn.sigmoid(_mm3(x, lw['w_ple_gate'])) * _mm3(p_l, lw['w_ple'])


def _layer_prompt(x, p_l, lw):
    n_b, n_s, _ = x.shape
    pr = prompt_projection(x, lw)
    oa_t = dsa_prompt(pr['qi'], pr['wi'], pr['ki_h'], pr['qa'], pr['ka_h'], pr['va_t'],
                      top=min(DSA_TOPK, n_s // 4))
    ob_t = moba_prompt(pr['qb'], pr['kb_h'], pr['vb_t'])
    oa = jnp.transpose(oa_t, (0, 3, 1, 2)).reshape(n_b, n_s, A_WIDTH)
    ob = jnp.transpose(ob_t, (0, 3, 1, 2)).reshape(n_b, n_s, B_WIDTH)
    zc = pr['zc']
    oc, C1, n1, m1 = mlstm_call((zc, 0), pr['kc_t'], (zc, 1), (zc, 2), pr['gates'], lw['g_c_out'], jnp.zeros((n_b, C_HEADS, C_HD, C_HD), F32),
                                jnp.zeros((n_b, C_HEADS, C_HD), F32), jnp.zeros((n_b, C_HEADS), F32), n_s)
    g0 = 3 * C_WIDTH
    x = _mix_ffn(x, p_l, lw, oa, ob, oc, zc[..., g0:g0 + D_MODEL], zc[..., g0 + D_MODEL:g0 + 2 * D_MODEL],
                 zc[..., g0 + 2 * D_MODEL:])
    kv4 = lambda a: a.reshape(n_b, n_s, A_KV, HEAD_DIM)
    return x, (kv4(pr['ka']), kv4(pr['va']), pr['ki'], kv4(pr['kb']), kv4(pr['vb']), C1, n1, m1)


def _layer_sample(x, p_l, pos, lw, past):
    n_b, n_s, _ = x.shape
    z = _mm3(x, lw['w_in'], lw['g_mix'])
    (aq, ak, av, aiq, aiw, aik, bq, bk, bv, cq, ck, cv, co, ci, cf, gate) = jnp.split(z, IN_SPLITS, axis=-1)
    aq = _rope(_rms(aq.reshape(n_b, n_s, A_HEADS, HEAD_DIM), lw['g_qa']), pos).reshape(n_b, n_s, A_KV, A_GROUP, HEAD_DIM)
    ak = _rope(_rms(ak.reshape(n_b, n_s, A_KV, HEAD_DIM), lw['g_ka']), pos)
    av = av.reshape(n_b, n_s, A_KV, HEAD_DIM)
    aiq = _rope(aiq.reshape(n_b, n_s, IDX_HEADS, IDX_DIM), pos)
    aik = _rope(aik.reshape(n_b, n_s, 1, IDX_DIM), pos)[:, :, 0]
    aiw = aiw * IDX_W_SCALE
    bq = _rope(_rms(bq.reshape(n_b, n_s, B_HEADS, HEAD_DIM), lw['g_qb']), pos).reshape(n_b, n_s, B_KV, B_GROUP, HEAD_DIM)
    bk = _rope(_rms(bk.reshape(n_b, n_s, B_KV, HEAD_DIM), lw['g_kb']), pos)
    bv = bv.reshape(n_b, n_s, B_KV, HEAD_DIM)
    cq = cq.reshape(n_b, n_s, C_HEADS, C_HD)
    ck = ck.reshape(n_b, n_s, C_HEADS, C_HD)
    cv = cv.reshape(n_b, n_s, C_HEADS, C_HD)
    ci = ci + lw['b_if'][:C_HEADS]
    cf = cf + lw['b_if'][C_HEADS:]
    oa = dsa_sample(past['layer'], past['page_table'], past['a_idx'], past['a_k'], past['a_v'],
                    aq, aiq, aiw, ak, av, aik)
    ob = moba_sample(past['layer'], past['page_table'], past['b_k'], past['b_v'], bq, bk, bv)
    oc, C1, n1, m1 = mlstm(cq.reshape(n_b, n_s, C_WIDTH), ck.reshape(n_b, n_s, C_WIDTH),
                           cv.reshape(n_b, n_s, C_WIDTH), co, ci, cf, lw['g_c_out'],
                           past['c_C'], past['c_n'], past['c_m'])
    g_a, g_b, g_c = jnp.split(gate, N_BRANCH, axis=-1)
    return _mix_ffn(x, p_l, lw, oa, ob, oc, g_a, g_b, g_c), (ak, av, aik, bk, bv, C1, n1, m1)


def kernel(x_prompt, x_sample, p_prompt, p_sample, cache_a_k, cache_a_v, cache_a_idx, cache_b_k, cache_b_v,
           state_c_C, state_c_n, state_c_m, page_table, g_mix, w_in, b_if, g_qa, g_ka, g_qb, g_kb, g_c_out,
           w_br_a, w_br_b, w_br_c, w_out, g_ffn, w_peer_q, peer_subkeys, peer_u, peer_v, w_ple, w_ple_gate):
    past_len = page_table.shape[1] * PAGE_SIZE
    pos_s = past_len + jnp.arange(x_sample.shape[1], dtype=jnp.int32)
    yp, ys = x_prompt, x_sample
    states_p, states_s = [], []
    for l in range(DEPTH):
        lw = {'g_mix': g_mix[l], 'w_in': w_in[l], 'b_if': b_if[l], 'g_qa': g_qa[l], 'g_ka': g_ka[l],
              'g_qb': g_qb[l], 'g_kb': g_kb[l], 'g_c_out': g_c_out[l], 'w_br_a': w_br_a[l], 'w_br_b': w_br_b[l],
              'w_br_c': w_br_c[l], 'w_out': w_out[l], 'g_ffn': g_ffn[l], 'w_peer_q': w_peer_q[l],
              'peer_subkeys': peer_subkeys[l], 'peer_u': peer_u[l], 'peer_v': peer_v[l], 'w_ple': w_ple[l],
              'w_ple_gate': w_ple_gate[l]}
        lw['peer_u_b'], lw['peer_vt_b'] = peer_tables(peer_u[l], peer_v[l])
        yp, st_p = _layer_prompt(yp, p_prompt[l], lw)
        past = {'layer': l, 'page_table': page_table, 'a_k': cache_a_k, 'a_v': cache_a_v, 'a_idx': cache_a_idx,
                'b_k': cache_b_k, 'b_v': cache_b_v,
                'c_C': state_c_C[l], 'c_n': state_c_n[l], 'c_m': state_c_m[l]}
        ys, st_s = _layer_sample(ys, p_sample[l], pos_s, lw, past)
        states_p.append(st_p)
        states_s.append(st_s)
    (pa_k, pa_v, pa_i, pb_k, pb_v, pc_C, pc_n, pc_m) = [jnp.stack(a) for a in zip(*states_p)]
    (sa_k, sa_v, sa_i, sb_k, sb_v, sc_C, sc_n, sc_m) = [jnp.stack(a) for a in zip(*states_s)]
    return (yp, ys, pa_k, pa_v, pa_i, pb_k, pb_v, pc_C, pc_n, pc_m,
            sa_k, sa_v, sa_i, sb_k, sb_v, sc_C, sc_n, sc_m)
```

```python
import functools
import math

import jax
import jax.numpy as jnp
import numpy as np
from jax import lax
from jax.experimental import pallas as pl
from jax.experimental.pallas import tpu as pltpu

D_MODEL = 1024
DEPTH = 4
PAGE_SIZE = 128
HEAD_DIM = 64
A_HEADS = 8
A_KV = 4
A_GROUP = A_HEADS // A_KV
A_WIDTH = A_HEADS * HEAD_DIM
IDX_HEADS = 8
IDX_DIM = 64
IDX_W_SCALE = (IDX_HEADS ** -0.5) * (IDX_DIM ** -0.5)
DSA_TOPK = 256
B_HEADS = 8
B_KV = 4
B_GROUP = B_HEADS // B_KV
B_WIDTH = B_HEADS * HEAD_DIM
MOBA_BLOCK = 256
MOBA_TOPK = 3
C_HEADS = 4
C_HD = 128
C_WIDTH = C_HEADS * C_HD
C_CHUNK = 64
N_BRANCH = 3
PEER_HEADS = 8
PEER_NKEYS = 128
PEER_QDIM = 256
PEER_TOPK = 16
PLE_DIM = 256
ROPE_THETA = 10000.0
EPS = 1e-6
Q_BLOCK = 128
MOBA_Q_BLOCK = 32
TOKEN_BLOCK = 256

IN_SIZES = (A_WIDTH, A_KV * HEAD_DIM, A_KV * HEAD_DIM, IDX_HEADS * IDX_DIM, IDX_HEADS, IDX_DIM,
            B_WIDTH, B_KV * HEAD_DIM, B_KV * HEAD_DIM,
            C_WIDTH, C_WIDTH, C_WIDTH, C_WIDTH, C_HEADS, C_HEADS,
            N_BRANCH * D_MODEL)
IN_SPLITS = tuple(int(s) for s in np.cumsum(IN_SIZES)[:-1])

F32 = jnp.float32
BF16 = jnp.bfloat16

LANES = 128
SUBLANES = 8
VMEM_LIMIT = 48 * 1024 * 1024


def _mm_kernel(x_ref, g_ref, w_ref, o_ref, h_ref, *, normalize):
    @pl.when(pl.program_id(1) == 0)
    def _():
        x = x_ref[...]
        if normalize:
            x = x * lax.rsqrt(jnp.mean(x * x, axis=-1, keepdims=True) + EPS) * g_ref[...]
        h_ref[...] = x.astype(BF16)

    o_ref[...] = jnp.dot(h_ref[...], w_ref[...], preferred_element_type=F32)


def _pick_tile(n, cands):
    for c in cands:
        if n % c == 0:
            return c
    return n


def matmul(x, w, g=None):
    m, k = x.shape
    n = w.shape[1]
    n_pad = -(-n // LANES) * LANES
    wb = w.astype(BF16)
    if n_pad != n:
        wb = jnp.pad(wb, ((0, 0), (0, n_pad - n)))
    tm = _pick_tile(m, (1024, 512, 256, 128))
    tn = _pick_tile(n_pad, (512, 256, 128))
    gg = (jnp.ones((k,), F32) if g is None else g.astype(F32)).reshape(1, k)
    out = pl.pallas_call(
        functools.partial(_mm_kernel, normalize=g is not None),
        grid=(m // tm, n_pad // tn),
        in_specs=[pl.BlockSpec((tm, k), lambda i, j: (i, 0)),
                  pl.BlockSpec((1, k), lambda i, j: (0, 0)),
                  pl.BlockSpec((k, tn), lambda i, j: (0, j))],
        out_specs=pl.BlockSpec((tm, tn), lambda i, j: (i, j)),
        out_shape=jax.ShapeDtypeStruct((m, n_pad), F32),
        scratch_shapes=[pltpu.VMEM((tm, k), BF16)],
        compiler_params=pltpu.CompilerParams(dimension_semantics=("arbitrary", "arbitrary"),
                                             vmem_limit_bytes=VMEM_LIMIT),
        name="matmul",
    )(x, gg, wb)
    return out[:, :n] if n_pad != n else out


def _mix_out_kernel(x_ref, oa_ref, ob_ref, oc_ref, ga_ref, gb_ref, gc_ref, wa_ref, wb_ref, wc_ref, wo_ref, o_ref):
    mix = None
    for o_b, g_b, w_b in ((oa_ref, ga_ref, wa_ref), (ob_ref, gb_ref, wb_ref), (oc_ref, gc_ref, wc_ref)):
        term = jax.nn.sigmoid(g_b[...]) * jnp.dot(o_b[...].astype(BF16), w_b[...], preferred_element_type=F32)
        mix = term if mix is None else mix + term
    o_ref[...] = x_ref[...] + jnp.dot(mix.astype(BF16), wo_ref[...], preferred_element_type=F32)


def mix_out(x, oa, ob, oc, gate, gate_block, lw):
    n_tok, d = x.shape
    tm = _pick_tile(n_tok, (512, 256, 128))
    tok = lambda n: pl.BlockSpec((tm, n), lambda i: (i, 0))
    gate_spec = lambda j: pl.BlockSpec((tm, d), lambda i: (i, gate_block + j))
    const = lambda a: pl.BlockSpec(a.shape, lambda i: (0, 0))
    ws = [lw[k].astype(BF16) for k in ('w_br_a', 'w_br_b', 'w_br_c', 'w_out')]
    return pl.pallas_call(
        _mix_out_kernel,
        grid=(n_tok // tm,),
        in_specs=[tok(d), tok(oa.shape[1]), tok(ob.shape[1]), tok(oc.shape[1]),
                  gate_spec(0), gate_spec(1), gate_spec(2)] + [const(w) for w in ws],
        out_specs=tok(d),
        out_shape=jax.ShapeDtypeStruct((n_tok, d), F32),
        compiler_params=pltpu.CompilerParams(dimension_semantics=("arbitrary",), vmem_limit_bytes=VMEM_LIMIT),
        name="mix_out",
    )(x, oa, ob, oc, gate, gate, gate, *ws)


def _ple_kernel(x_ref, p_ref, wg_ref, wp_ref, o_ref):
    x = x_ref[...]
    gate = jax.nn.sigmoid(jnp.dot(x.astype(BF16), wg_ref[...], preferred_element_type=F32))
    o_ref[...] = x + gate * jnp.dot(p_ref[...].astype(BF16), wp_ref[...], preferred_element_type=F32)


def ple(x, p, lw):
    n_tok, d = x.shape
    tm = _pick_tile(n_tok, (512, 256, 128))
    wg, wp = lw['w_ple_gate'].astype(BF16), lw['w_ple'].astype(BF16)
    return pl.pallas_call(
        _ple_kernel,
        grid=(n_tok // tm,),
        in_specs=[pl.BlockSpec((tm, d), lambda i: (i, 0)), pl.BlockSpec((tm, p.shape[1]), lambda i: (i, 0)),
                  pl.BlockSpec(wg.shape, lambda i: (0, 0)), pl.BlockSpec(wp.shape, lambda i: (0, 0))],
        out_specs=pl.BlockSpec((tm, d), lambda i: (i, 0)),
        out_shape=jax.ShapeDtypeStruct((n_tok, d), F32),
        compiler_params=pltpu.CompilerParams(dimension_semantics=("arbitrary",), vmem_limit_bytes=VMEM_LIMIT),
        name="ple",
    )(x, p, wg, wp)


def _mm3(x, w, g=None):
    b, s, k = x.shape
    return matmul(x.reshape(b * s, k), w, g).reshape(b, s, w.shape[1])


PROJ_TM = 512
HALF = HEAD_DIM // 2
T_ROWS = (A_WIDTH, IDX_HEADS * IDX_DIM, B_WIDTH, A_KV * HEAD_DIM, B_KV * HEAD_DIM, C_WIDTH, 2 * SUBLANES)
T_OFFS = tuple(int(v) for v in np.cumsum((0,) + T_ROWS))


def _rms_rows(x, g):
    return x * lax.rsqrt(jnp.mean(x * x, axis=0, keepdims=True) + EPS) * g


def _rope_rows(x, cos, sin):
    x1, x2 = x[:HALF], x[HALF:]
    return jnp.concatenate([x1 * cos - x2 * sin, x1 * sin + x2 * cos], axis=0)


def _proj_t_kernel(x_ref, g_ref, wt_ref, cos_ref, sin_ref, gq_ref, bias_ref,
                   qa_ref, qi_ref, wi_ref, qb_ref, va_ref, vb_ref, kc_ref, if_ref):
    x = x_ref[0]
    hb = (x * lax.rsqrt(jnp.mean(x * x, axis=-1, keepdims=True) + EPS) * g_ref[...]).astype(BF16)
    tm = hb.shape[0]
    cos, sin = cos_ref[...], sin_ref[...]
    scale = HEAD_DIM ** -0.5

    def seg(k):
        return lax.dot_general(wt_ref[T_OFFS[k]:T_OFFS[k + 1], :], hb, (((1,), (1,)), ((), ())),
                               preferred_element_type=F32)

    for k, (out_ref, gain) in enumerate(((qa_ref, 0), (qi_ref, None), (qb_ref, 1))):
        z = seg((0, 1, 2)[k])
        for h in range(A_HEADS):
            xh = z[h * HEAD_DIM:(h + 1) * HEAD_DIM]
            if gain is not None:
                xh = _rms_rows(xh, gq_ref[gain]) * scale
            out_ref[0, h] = _rope_rows(xh, cos, sin).astype(BF16)
    for k, out_ref in ((3, va_ref), (4, vb_ref)):
        z = seg(k).astype(BF16)
        for kv in range(A_KV):
            for j in range(tm // ATT_TK):
                out_ref[0, kv, j] = z[kv * HEAD_DIM:(kv + 1) * HEAD_DIM, j * ATT_TK:(j + 1) * ATT_TK]
    kc_ref[0] = seg(5) * (C_HD ** -0.5)
    small = seg(6)
    wi_ref[0] = small[0:IDX_HEADS] * IDX_W_SCALE
    if_ref[0] = small[IDX_HEADS:] + bias_ref[...]


def _rope_lanes(x, cos, sin_lo, sin_hi):
    return x * cos + pltpu.roll(x, LANES - HALF, 1) * sin_lo + pltpu.roll(x, HALF, 1) * sin_hi


def _proj_kv_kernel(x_ref, g_ref, w_ref, cos_ref, sinlo_ref, sinhi_ref, gk_ref, bd_ref,
                    ka_ref, va_ref, ki_ref, kb_ref, vb_ref, kah_ref, kih_ref, kbh_ref):
    x = x_ref[0]
    hb = (x * lax.rsqrt(jnp.mean(x * x, axis=-1, keepdims=True) + EPS) * g_ref[...]).astype(BF16)
    cos, sin_lo, sin_hi = cos_ref[...], sinlo_ref[...], sinhi_ref[...]
    width = A_KV * HEAD_DIM

    def seg(c0, n):
        return jnp.dot(hb, w_ref[:, c0:c0 + n], preferred_element_type=F32)

    def rope(z):
        return jnp.concatenate([_rope_lanes(z[:, j * LANES:(j + 1) * LANES], cos, sin_lo, sin_hi)
                                for j in range(z.shape[1] // LANES)], axis=1)

    for k, (f_ref, h_ref) in enumerate(((ka_ref, kah_ref), (kb_ref, kbh_ref))):
        z = seg(k * width, width)
        ms = jnp.dot(z * z, bd_ref[...], precision=HIGHEST, preferred_element_type=F32)
        z = rope(z * lax.rsqrt(ms + EPS) * gk_ref[k:k + 1, :])
        f_ref[0] = z
        zb = z.astype(BF16)
        for kv in range(A_KV):
            h_ref[0, kv] = zb[:, kv * HEAD_DIM:(kv + 1) * HEAD_DIM]
    va_ref[0] = seg(2 * width, width)
    vb_ref[0] = seg(3 * width, width)
    zi = rope(seg(4 * width, LANES))[:, :IDX_DIM]
    ki_ref[0] = zi
    kih_ref[0] = zi.astype(BF16)


def _rope_tables(n_s):
    freqs = jnp.power(jnp.float32(ROPE_THETA), -jnp.arange(HALF, dtype=F32) / HALF)
    ang = jnp.arange(n_s, dtype=F32)[:, None] * freqs[None, :]
    return jnp.cos(ang), jnp.sin(ang)


def prompt_projection(x, lw):
    n_b, n_s, d = x.shape
    tm = PROJ_TM
    n_t = n_s // tm
    w = lw['w_in']
    cols = dict(zip(('aq', 'ak', 'av', 'aiq', 'aiw', 'aik', 'bq', 'bk', 'bv', 'cq', 'ck', 'cv', 'co', 'ci', 'cf',
                     'gate'), jnp.split(w, IN_SPLITS, axis=1)))
    g = lw['g_mix'].astype(F32).reshape(1, d)
    cos, sin = _rope_tables(n_s)
    x_spec = pl.BlockSpec((1, tm, d), lambda b, i: (b, i, 0))
    g_spec = pl.BlockSpec((1, d), lambda b, i: (0, 0))
    params = pltpu.CompilerParams(dimension_semantics=("arbitrary", "arbitrary"), vmem_limit_bytes=VMEM_LIMIT)

    wt = jnp.concatenate([cols[k] for k in ('aq', 'aiq', 'bq', 'av', 'bv', 'ck', 'aiw', 'ci', 'cf')],
                         axis=1).T.astype(BF16)
    gq = jnp.stack([jnp.broadcast_to(lw[k].astype(F32)[:, None], (HEAD_DIM, tm)) for k in ('g_qa', 'g_qb')])
    bias = jnp.broadcast_to(lw['b_if'].astype(F32)[:, None], (2 * C_HEADS, tm))
    heads_t = lambda: pl.BlockSpec((1, A_HEADS, HEAD_DIM, tm), lambda b, i: (b, 0, 0, i))
    vals_t = lambda: pl.BlockSpec((1, A_KV, tm // ATT_TK, HEAD_DIM, ATT_TK), lambda b, i: (b, 0, i, 0, 0))
    rows_t = lambda n: pl.BlockSpec((1, n, tm), lambda b, i: (b, 0, i))
    tab_t = pl.BlockSpec((HALF, tm), lambda b, i: (0, i))
    const = lambda shape: pl.BlockSpec(shape, lambda b, i: (0,) * len(shape))
    heads_shape = jax.ShapeDtypeStruct((n_b, A_HEADS, HEAD_DIM, n_s), BF16)
    vals_shape = jax.ShapeDtypeStruct((n_b, A_KV, n_s // ATT_TK, HEAD_DIM, ATT_TK), BF16)
    qa, qi, wi, qb, va_t, vb_t, kc_t, gates = pl.pallas_call(
        _proj_t_kernel,
        grid=(n_b, n_t),
        in_specs=[x_spec, g_spec, const(wt.shape), tab_t, tab_t, const(gq.shape), const(bias.shape)],
        out_specs=[heads_t(), heads_t(), rows_t(IDX_HEADS), heads_t(), vals_t(), vals_t(),
                   rows_t(C_WIDTH), rows_t(2 * C_HEADS)],
        out_shape=[heads_shape, heads_shape, jax.ShapeDtypeStruct((n_b, IDX_HEADS, n_s), F32), heads_shape,
                   vals_shape, vals_shape, jax.ShapeDtypeStruct((n_b, C_WIDTH, n_s), F32),
                   jax.ShapeDtypeStruct((n_b, 2 * C_HEADS, n_s), F32)],
        compiler_params=params,
        name="proj_t",
    )(x, g, wt, cos.T, sin.T, gq, bias)

    width = A_KV * HEAD_DIM
    wkv = jnp.concatenate([cols['ak'], cols['bk'], cols['av'], cols['bv'], cols['aik'],
                           jnp.zeros((d, LANES - IDX_DIM), w.dtype)], axis=1).astype(BF16)
    lane = jnp.arange(LANES)
    cos_l = jnp.tile(cos, (1, LANES // HALF))
    sin_l = jnp.tile(sin, (1, LANES // HALF))
    sin_lo = jnp.where((lane % HEAD_DIM) < HALF, -sin_l, 0.0)
    sin_hi = jnp.where((lane % HEAD_DIM) >= HALF, sin_l, 0.0)
    gk = jnp.stack([jnp.tile(lw[k].astype(F32), A_KV) for k in ('g_ka', 'g_kb')])
    head_of = jnp.arange(width) // HEAD_DIM
    bd = jnp.where(head_of[:, None] == head_of[None, :], 1.0 / HEAD_DIM, 0.0).astype(F32)
    tab = pl.BlockSpec((tm, LANES), lambda b, i: (i, 0))
    tok = lambda n: pl.BlockSpec((1, tm, n), lambda b, i: (b, i, 0))
    heads = lambda: pl.BlockSpec((1, A_KV, tm, HEAD_DIM), lambda b, i: (b, 0, i, 0))
    tok_shape = lambda n, dt=F32: jax.ShapeDtypeStruct((n_b, n_s, n), dt)
    heads_kv = jax.ShapeDtypeStruct((n_b, A_KV, n_s, HEAD_DIM), BF16)
    ka, va, ki, kb, vb, ka_h, ki_h, kb_h = pl.pallas_call(
        _proj_kv_kernel,
        grid=(n_b, n_t),
        in_specs=[x_spec, g_spec, const(wkv.shape), tab, tab, tab, const(gk.shape), const(bd.shape)],
        out_specs=[tok(width), tok(width), tok(IDX_DIM), tok(width), tok(width), heads(), tok(IDX_DIM), heads()],
        out_shape=[tok_shape(width), tok_shape(width), tok_shape(IDX_DIM), tok_shape(width), tok_shape(width),
                   heads_kv, tok_shape(IDX_DIM, BF16), heads_kv],
        compiler_params=params,
        name="proj_kv",
    )(x, g, wkv, cos_l, sin_lo, sin_hi, gk, bd)

    wc = jnp.concatenate([cols['gate'], cols['cq'], cols['cv'], cols['co']], axis=1)
    zc = matmul(x.reshape(n_b * n_s, d), wc, lw['g_mix']).reshape(n_b, n_s, -1)
    return dict(qa=qa, qi=qi, wi=wi, qb=qb, va_t=va_t, vb_t=vb_t, kc_t=kc_t, gates=gates,
                ka=ka, va=va, ki=ki, kb=kb, vb=vb, ka_h=ka_h, ki_h=ki_h, kb_h=kb_h, zc=zc)


INT_MIN = -2 ** 31
INT_MAX = 2 ** 31 - 1
NEG_BIG = -1e30
ATT_TQ = 128
ATT_TK = 256
BISECT_VALUE_STEPS = 12
BISECT_INT_STEPS = 33


def _key_value(k):
    return pltpu.bitcast(jnp.where(k < 0, k ^ INT_MAX, k), F32)


def _bisect_mid(it, lo, hi):
    mid = (lo >> 1) + (hi >> 1) + (lo & hi & 1)
    vbits = pltpu.bitcast(0.5 * (_key_value(lo) + _key_value(hi)), jnp.int32)
    vmid = jnp.where(vbits < 0, vbits ^ INT_MAX, vbits)
    vmid_cap = jnp.where(it < BISECT_VALUE_STEPS, INT_MAX, INT_MIN)
    return jnp.where((vmid > lo) & (vmid < jnp.minimum(hi, vmid_cap)), vmid, mid)


def _col_sum_i32(mask):
    rows, tq = mask.shape
    return jnp.sum(mask.astype(jnp.int32).reshape(rows // SUBLANES, SUBLANES, tq), axis=0)


def _attend_init(m_ref, l_ref, acc_ref):
    m_ref[...] = jnp.full(m_ref.shape, NEG_BIG, F32)
    l_ref[...] = jnp.zeros(l_ref.shape, F32)
    acc_ref[...] = jnp.zeros(acc_ref.shape, F32)


def _attend_chunk(qt_ref, k_ref, vt_ref, start, c, sel_of, s_ref, p_ref, a_ref, m_ref, l_ref, acc_ref):
    n_kv, tk = s_ref.shape[0], s_ref.shape[1]
    n_g = qt_ref.shape[1] // n_kv
    for kv in range(n_kv):
        q2 = jnp.concatenate([qt_ref[0, kv * n_g + g] for g in range(n_g)], axis=1)
        s_ref[kv] = jnp.dot(k_ref[0, kv, pl.ds(start, tk), :], q2, preferred_element_type=F32)
    for kv in range(n_kv):
        sel = sel_of(kv)
        s = s_ref[kv]
        m_old = m_ref[kv]
        m_new = jnp.maximum(m_old, jnp.max(jnp.where(sel, s, NEG_BIG), axis=0, keepdims=True))
        p = jnp.where(sel, jnp.exp(s - m_new), 0.0)
        alpha = jnp.exp(m_old - m_new)
        l_ref[kv] = alpha * l_ref[kv] + jnp.sum(p, axis=0, keepdims=True)
        p_ref[kv] = p.astype(BF16)
        a_ref[kv] = alpha
        m_ref[kv] = m_new
    for kv in range(n_kv):
        acc_ref[kv] = a_ref[kv] * acc_ref[kv] + jnp.dot(vt_ref[0, kv, c], p_ref[kv],
                                                        preferred_element_type=F32)


def _attend_finish(o_ref, l_ref, acc_ref):
    n_kv, hd, wide = acc_ref.shape
    tq = o_ref.shape[1]
    n_g = wide // tq
    for kv in range(n_kv):
        out = acc_ref[kv] / l_ref[kv]
        rows = jnp.concatenate([out[:, g * tq:(g + 1) * tq] for g in range(n_g)], axis=0)
        o_ref[0, :, kv * n_g * hd:(kv + 1) * n_g * hd] = rows.T


def _attend_scratch(n_kv, n_g, tq, tk):
    wide = n_g * tq
    return [pltpu.VMEM((n_kv, tk, wide), F32),
            pltpu.VMEM((n_kv, tk, wide), BF16),
            pltpu.VMEM((n_kv, 1, wide), F32),
            pltpu.VMEM((n_kv, 1, wide), F32),
            pltpu.VMEM((n_kv, 1, wide), F32),
            pltpu.VMEM((n_kv, HEAD_DIM, wide), F32)]


def _dsa_prompt_kernel(qit_ref, wt_ref, ki_ref, qt_ref, k_ref, vt_ref, o_ref,
                       key_ref, j_ref, s_ref, p_ref, a_ref, m_ref, l_ref, acc_ref, *, top):
    tq, tk = ATT_TQ, ATT_TK
    i = pl.program_id(1)
    n_chunks = ((i + 1) * tq + tk - 1) // tk
    qpos = i * tq + lax.broadcasted_iota(jnp.int32, (1, tq), 1)
    row = lax.broadcasted_iota(jnp.int32, (tk, tq), 0)

    def score_chunk(c, carry):
        start = pl.multiple_of(c * tk, tk)
        kic = ki_ref[0, pl.ds(start, tk), :]
        s = jnp.zeros((tk, tq), F32)
        for h in range(IDX_HEADS):
            d = jnp.dot(kic, qit_ref[0, h], preferred_element_type=F32)
            s = s + wt_ref[0, h:h + 1, :] * jnp.maximum(d, 0.0)
        s = jnp.where(s == 0.0, 0.0, s)
        bits = pltpu.bitcast(s, jnp.int32)
        key = jnp.where(bits < 0, bits ^ INT_MAX, bits)
        key = jnp.where(start + row <= qpos, key, INT_MIN)
        key_ref[pl.ds(start, tk), :] = key
        return carry

    lax.fori_loop(0, n_chunks, score_chunk, 0)

    def count(pred):
        def body(c, acc):
            start = pl.multiple_of(c * tk, tk)
            return acc + _col_sum_i32(pred(key_ref[pl.ds(start, tk), :], start + row))
        acc = lax.fori_loop(0, n_chunks, body, jnp.zeros((SUBLANES, tq), jnp.int32))
        return jnp.sum(acc, axis=0, keepdims=True)

    def bis_cond(st):
        it, lo, hi, _ = st
        return ((it < BISECT_VALUE_STEPS + BISECT_INT_STEPS)
                & (jnp.max(jnp.where(hi != lo + 1, 1, 0)) > 0))

    def bis_body(st):
        it, lo, hi, cnt_lo = st
        mid = _bisect_mid(it, lo, hi)
        c = count(lambda kc, _: kc >= mid)
        ge = c >= top
        lo2 = jnp.where(ge, mid, lo)
        hi2 = jnp.where(c == top, mid + 1, jnp.where(ge, hi, mid))
        return it + 1, lo2, hi2, jnp.where(ge, c, cnt_lo)

    def max_body(c, acc):
        kc = key_ref[pl.ds(pl.multiple_of(c * tk, tk), tk), :]
        return jnp.maximum(acc, jnp.max(kc.reshape(tk // SUBLANES, SUBLANES, tq), axis=0))

    kmax = jnp.max(lax.fori_loop(0, n_chunks, max_body, jnp.full((SUBLANES, tq), INT_MIN, jnp.int32)),
                   axis=0, keepdims=True)
    _, t, _, cnt_t = lax.while_loop(
        bis_cond, lambda st: bis_body(bis_body(st)),
        (jnp.int32(0), jnp.full((1, tq), INT_MIN, jnp.int32), kmax + 1,
         jnp.zeros((1, tq), jnp.int32) + n_chunks * tk))

    need = (cnt_t > top) & (t > INT_MIN)
    j_ref[...] = jnp.full((1, tq), INT_MAX, jnp.int32)

    @pl.when(jnp.max(need.astype(jnp.int32)) > 0)
    def _():
        r = top - count(lambda kc, _: kc > t)

        def jb(_, st):
            lo, hi = st
            mid = (lo + hi) >> 1
            ok = count(lambda kc, kp: (kc == t) & (kp <= mid)) >= r
            return jnp.where(ok, lo, mid), jnp.where(ok, mid, hi)

        n_bits = int(math.ceil(math.log2(key_ref.shape[0]))) + 1
        _, hi = lax.fori_loop(0, n_bits, jb, (jnp.full((1, tq), -1, jnp.int32),
                                              jnp.zeros((1, tq), jnp.int32) + (n_chunks * tk - 1)))
        j_ref[...] = jnp.where(need, hi, INT_MAX)

    _attend_init(m_ref, l_ref, acc_ref)
    jcut = j_ref[...]

    def att_chunk(c, carry):
        start = pl.multiple_of(c * tk, tk)
        kc = key_ref[pl.ds(start, tk), :]
        sel = (kc > INT_MIN) & ((kc > t) | ((kc == t) & (start + row <= jcut)))
        sel2 = jnp.concatenate([sel] * A_GROUP, axis=1)
        _attend_chunk(qt_ref, k_ref, vt_ref, start, c, lambda kv: sel2, s_ref, p_ref, a_ref, m_ref, l_ref, acc_ref)
        return carry

    lax.fori_loop(0, n_chunks, att_chunk, 0)
    _attend_finish(o_ref, l_ref, acc_ref)


def dsa_prompt(qit, wt, ki, qt, k, vt, top):
    n_b, _, _, n_s = qt.shape
    tq, tk = ATT_TQ, ATT_TK
    return pl.pallas_call(
        functools.partial(_dsa_prompt_kernel, top=top),
        grid=(n_b, n_s // tq),
        in_specs=[pl.BlockSpec((1, IDX_HEADS, IDX_DIM, tq), lambda b, i: (b, 0, 0, i)),
                  pl.BlockSpec((1, IDX_HEADS, tq), lambda b, i: (b, 0, i)),
                  pl.BlockSpec((1, n_s, IDX_DIM), lambda b, i: (b, 0, 0)),
                  pl.BlockSpec((1, A_HEADS, HEAD_DIM, tq), lambda b, i: (b, 0, 0, i)),
                  pl.BlockSpec((1, A_KV, n_s, HEAD_DIM), lambda b, i: (b, 0, 0, 0)),
                  pl.BlockSpec((1, A_KV, n_s // tk, HEAD_DIM, tk), lambda b, i: (b, 0, 0, 0, 0))],
        out_specs=pl.BlockSpec((1, tq, A_WIDTH), lambda b, i: (b, i, 0)),
        out_shape=jax.ShapeDtypeStruct((n_b, n_s, A_WIDTH), F32),
        scratch_shapes=[pltpu.VMEM((n_s, tq), jnp.int32),
                        pltpu.VMEM((1, tq), jnp.int32)] + _attend_scratch(A_KV, A_GROUP, tq, tk),
        compiler_params=pltpu.CompilerParams(dimension_semantics=("arbitrary", "arbitrary"),
                                             vmem_limit_bytes=VMEM_LIMIT),
        name="dsa_prompt",
    )(qit, wt, ki, qt, k, vt)


def _moba_prompt_kernel(qt_ref, k_ref, vt_ref, avg_ref, o_ref, kmean_ref, selm_ref,
                        s_ref, p_ref, a_ref, m_ref, l_ref, acc_ref, *, top):
    tq, tk = ATT_TQ, MOBA_BLOCK
    wide = B_GROUP * tq
    n_blk = avg_ref.shape[0]
    i = pl.program_id(1)
    n_own = (i * tq) // tk
    qpos = i * tq + (lax.broadcasted_iota(jnp.int32, (1, wide), 1) & (tq - 1))
    row = lax.broadcasted_iota(jnp.int32, (tk, wide), 0)
    nrow = lax.broadcasted_iota(jnp.int32, (n_blk, wide), 0)

    @pl.when(i == 0)
    def _():
        for kv in range(B_KV):
            kmean_ref[kv] = jnp.dot(avg_ref[...], k_ref[0, kv], preferred_element_type=F32).astype(BF16)

    for kv in range(B_KV):
        q2 = jnp.concatenate([qt_ref[0, kv * B_GROUP + g] for g in range(B_GROUP)], axis=1)
        g = jnp.dot(kmean_ref[kv], q2, preferred_element_type=F32)
        g = jnp.where(nrow < n_own, g, -jnp.inf)
        picked = jnp.zeros((n_blk, wide), jnp.bool_)
        for _ in range(top):
            mx = jnp.max(g, axis=0, keepdims=True)
            first = jnp.min(jnp.where(g == mx, nrow, n_blk), axis=0, keepdims=True)
            hit = nrow == first
            picked = picked | hit
            g = jnp.where(hit, -jnp.inf, g)
        selm_ref[kv] = jnp.where(picked & (nrow < n_own), 1.0, 0.0)

    _attend_init(m_ref, l_ref, acc_ref)

    def att_block(n, carry):
        start = pl.multiple_of(n * tk, tk)
        own_causal = ((jnp.zeros((1, wide), jnp.int32) + n) == n_own) & (start + row <= qpos)

        def sel_of(kv):
            picked_n = jnp.max(jnp.where(nrow == n, selm_ref[kv], 0.0), axis=0, keepdims=True) > 0.0
            return own_causal | picked_n

        _attend_chunk(qt_ref, k_ref, vt_ref, start, n, sel_of, s_ref, p_ref, a_ref, m_ref, l_ref, acc_ref)
        return carry

    lax.fori_loop(0, n_own + 1, att_block, 0)
    _attend_finish(o_ref, l_ref, acc_ref)


def moba_prompt(qt, k, vt):
    n_b, _, _, n_s = qt.shape
    tq, tk = ATT_TQ, MOBA_BLOCK
    n_blk = n_s // tk
    avg = (jnp.repeat(jnp.eye(n_blk, dtype=F32), tk, axis=1) / tk).astype(BF16)
    return pl.pallas_call(
        functools.partial(_moba_prompt_kernel, top=min(MOBA_TOPK, n_blk)),
        grid=(n_b, n_s // tq),
        in_specs=[pl.BlockSpec((1, B_HEADS, HEAD_DIM, tq), lambda b, i: (b, 0, 0, i)),
                  pl.BlockSpec((1, B_KV, n_s, HEAD_DIM), lambda b, i: (b, 0, 0, 0)),
                  pl.BlockSpec((1, B_KV, n_blk, HEAD_DIM, tk), lambda b, i: (b, 0, 0, 0, 0)),
                  pl.BlockSpec((n_blk, n_s), lambda b, i: (0, 0))],
        out_specs=pl.BlockSpec((1, tq, B_WIDTH), lambda b, i: (b, i, 0)),
        out_shape=jax.ShapeDtypeStruct((n_b, n_s, B_WIDTH), F32),
        scratch_shapes=[pltpu.VMEM((B_KV, n_blk, HEAD_DIM), BF16),
                        pltpu.VMEM((B_KV, n_blk, B_GROUP * tq), F32)] + _attend_scratch(B_KV, B_GROUP, tq, tk),
        compiler_params=pltpu.CompilerParams(dimension_semantics=("arbitrary", "arbitrary"),
                                             vmem_limit_bytes=VMEM_LIMIT),
        name="moba_prompt",
    )(qt, k, vt, avg)


PEER_TT = 256
PEER_TE = 1024
PEER_NSORT = PEER_TOPK + 1
PEER_SVROWS = 24


def _extract_sorted(s, n, emit):
    rows, tt = s.shape
    rowi = lax.broadcasted_iota(jnp.int32, (rows, tt), 0)
    for r in range(n):
        mx = jnp.max(s, axis=0, keepdims=True)
        emit(r, mx)
        if r + 1 < n:
            first = jnp.min(jnp.where(s == mx, rowi, rows), axis=0, keepdims=True)
            s = jnp.where(rowi == first, -jnp.inf, s)


def _peer_select_kernel(x_ref, g_ref, wqt_ref, sk_ref, hb_ref, s1_ref, e1_ref, thr_ref, e0_ref,
                        q_ref, s_ref, sv_ref, top_ref):
    half = PEER_QDIM // 2
    x = x_ref[...]
    hb = (x * lax.rsqrt(jnp.mean(x * x, axis=-1, keepdims=True) + EPS) * g_ref[...]).astype(BF16)
    hb_ref[...] = hb
    q_ref[...] = lax.dot_general(wqt_ref[...], hb, (((1,), (1,)), ((), ())),
                                 preferred_element_type=F32).astype(BF16)
    sv_ref[...] = jnp.full(sv_ref.shape, -jnp.inf, F32)

    def head(h, carry):
        for p in range(2):
            hp = h * 2 + p
            s = jnp.dot(sk_ref[hp], q_ref[pl.ds(pl.multiple_of(hp * half, half), half), :],
                        preferred_element_type=F32)
            s_ref[p] = s

            def emit(r, v, p=p):
                sv_ref[p, r:r + 1, :] = v
            _extract_sorted(s, PEER_NSORT, emit)
        sv0, sv1 = sv_ref[0], sv_ref[1]
        cand = jnp.concatenate(
            [sv0[0:1] + sv1] + [sv0[a:a + 1] + sv1[0:8] for a in range(1, 8)] + [sv0[8:24] + sv1[0:1]], axis=0)

        def emit_top(r, v):
            top_ref[r:r + 1, :] = v
        _extract_sorted(cand, PEER_NSORT, emit_top)
        tau = 0.5 * (top_ref[PEER_TOPK - 1:PEER_TOPK, :] + top_ref[PEER_TOPK:PEER_TOPK + 1, :])
        cmax = sv0[0:1] + sv1[0:1]
        z = jnp.sum(jnp.where(cand >= tau, jnp.exp(cand - cmax), 0.0), axis=0, keepdims=True)
        s0, s1 = s_ref[0], s_ref[1]
        s1_ref[h] = s1
        e1_ref[h] = jnp.exp(s1 - sv1[0:1]) / z
        thr_ref[h] = tau - s0
        e0_ref[h] = jnp.exp(s0 - sv0[0:1])
        return carry

    lax.fori_loop(0, PEER_HEADS, head, 0)


def _peer_apply_kernel(x_ref, hb_ref, u_ref, vt_ref, s1_ref, e1_ref, thr_ref, e0_ref, o_ref, g_ref, acc_ref):
    e = pl.program_id(1)
    tt = hb_ref.shape[0]
    hb = hb_ref[...]
    contrib = None
    for il in range(0, PEER_TE // PEER_NKEYS, 2):
        rows = slice(il * PEER_NKEYS, (il + 2) * PEER_NKEYS)
        a = lax.dot_general(u_ref[rows, :], hb, (((1,), (1,)), ((), ())), preferred_element_type=F32)
        for lt in range(tt // LANES):
            ls = slice(lt * LANES, (lt + 1) * LANES)
            acc = [jnp.zeros((PEER_NKEYS, LANES), F32) for _ in range(2)]
            for h in range(PEER_HEADS):
                s1 = s1_ref[h, :, ls]
                e1 = e1_ref[h, :, ls]
                for j in range(2):
                    thr_row = thr_ref[h, 0, il + j:il + j + 1, ls]
                    e0_row = e0_ref[h, 0, il + j:il + j + 1, ls]
                    acc[j] = acc[j] + jnp.where(s1 >= thr_row, e1 * e0_row, 0.0)
            for j in range(2):
                g_ref[j * PEER_NKEYS:(j + 1) * PEER_NKEYS, ls] = acc[j]
        ga = (g_ref[...] * jax.nn.gelu(a)).astype(BF16)
        part = jnp.dot(vt_ref[:, rows], ga, preferred_element_type=F32)
        contrib = part if contrib is None else contrib + part

    @pl.when(e == 0)
    def _():
        acc_ref[...] = contrib

    @pl.when(e != 0)
    def _():
        acc_ref[...] += contrib

    @pl.when(e == pl.num_programs(1) - 1)
    def _():
        o_ref[...] = x_ref[...] + acc_ref[...].T


def _transpose_cast_kernel(x_ref, o_ref):
    o_ref[...] = x_ref[...].T.astype(o_ref.dtype)


def transpose_cast(x, dtype, tile=512):
    n_r, n_c = x.shape
    return pl.pallas_call(
        _transpose_cast_kernel,
        grid=(n_r // tile, n_c // tile),
        in_specs=[pl.BlockSpec((tile, tile), lambda i, j: (i, j))],
        out_specs=pl.BlockSpec((tile, tile), lambda i, j: (j, i)),
        out_shape=jax.ShapeDtypeStruct((n_c, n_r), dtype),
        compiler_params=pltpu.CompilerParams(dimension_semantics=("arbitrary", "arbitrary")),
        name="transpose_cast",
    )(x)


def peer_tables(u_tab, v_tab):
    return u_tab.astype(BF16), transpose_cast(v_tab, BF16)


def peer(x, g_ffn, w_q, subkeys, u_b, vt_b):
    n_tok, d = x.shape
    n_exp = u_b.shape[0]
    tt = _pick_tile(n_tok, (PEER_TT, LANES))
    assert tt % LANES == 0, "PEER kernels keep tokens on whole lane tiles"
    half = PEER_QDIM // 2
    wqt = w_q.T.astype(BF16)
    sk = subkeys.reshape(PEER_HEADS * 2, PEER_NKEYS, half).astype(BF16)
    stat = jax.ShapeDtypeStruct((PEER_HEADS, PEER_NKEYS, n_tok), F32)
    stat_spec = pl.BlockSpec((PEER_HEADS, PEER_NKEYS, tt), lambda t: (0, 0, t))
    hb, s1, e1, thr, e0 = pl.pallas_call(
        _peer_select_kernel,
        grid=(n_tok // tt,),
        in_specs=[pl.BlockSpec((tt, d), lambda t: (t, 0)),
                  pl.BlockSpec((1, d), lambda t: (0, 0)),
                  pl.BlockSpec((PEER_HEADS * PEER_QDIM, d), lambda t: (0, 0)),
                  pl.BlockSpec((PEER_HEADS * 2, PEER_NKEYS, half), lambda t: (0, 0, 0))],
        out_specs=[pl.BlockSpec((tt, d), lambda t: (t, 0)), stat_spec, stat_spec, stat_spec, stat_spec],
        out_shape=[jax.ShapeDtypeStruct((n_tok, d), BF16), stat, stat, stat, stat],
        scratch_shapes=[pltpu.VMEM((PEER_HEADS * PEER_QDIM, tt), BF16),
                        pltpu.VMEM((2, PEER_NKEYS, tt), F32),
                        pltpu.VMEM((2, PEER_SVROWS, tt), F32),
                        pltpu.VMEM((PEER_SVROWS, tt), F32)],
        compiler_params=pltpu.CompilerParams(dimension_semantics=("arbitrary",),
                                             vmem_limit_bytes=VMEM_LIMIT),
        name="peer_select",
    )(x, g_ffn.astype(F32).reshape(1, d), wqt, sk)

    n_il = PEER_TE // PEER_NKEYS
    thr4 = thr.reshape(PEER_HEADS, PEER_NKEYS // n_il, n_il, n_tok)
    e04 = e0.reshape(PEER_HEADS, PEER_NKEYS // n_il, n_il, n_tok)
    stat2 = pl.BlockSpec((PEER_HEADS, PEER_NKEYS, tt), lambda t, e: (0, 0, t))
    stat4 = pl.BlockSpec((PEER_HEADS, 1, n_il, tt), lambda t, e: (0, e, 0, t))
    tok = pl.BlockSpec((tt, d), lambda t, e: (t, 0))
    return pl.pallas_call(
        _peer_apply_kernel,
        grid=(n_tok // tt, n_exp // PEER_TE),
        in_specs=[tok, tok,
                  pl.BlockSpec((PEER_TE, d), lambda t, e: (e, 0)),
                  pl.BlockSpec((d, PEER_TE), lambda t, e: (0, e)),
                  stat2, stat2, stat4, stat4],
        out_specs=tok,
        out_shape=jax.ShapeDtypeStruct((n_tok, d), F32),
        scratch_shapes=[pltpu.VMEM((2 * PEER_NKEYS, tt), F32), pltpu.VMEM((d, tt), F32)],
        compiler_params=pltpu.CompilerParams(dimension_semantics=("arbitrary", "arbitrary"),
                                             vmem_limit_bytes=VMEM_LIMIT),
        name="peer_apply",
    )(x, hb, u_b, vt_b, s1, e1, thr4, e04)


MLSTM_CHUNK = 128
HIGHEST = lax.Precision.HIGHEST


def _mlstm_kernel(q_ref, kt_ref, v_ref, co_ref, if_ref, gc_ref, s0_ref, m0_ref,
                  oc_ref, s_out_ref, m_out_ref, s_ref, m_ref, *, valid):
    L, D = MLSTM_CHUNK, C_HD
    c = pl.program_id(1)

    @pl.when(c == 0)
    def _():
        s_ref[...] = s0_ref[0]
        m_ref[...] = m0_ref[0]

    li = lax.broadcasted_iota(jnp.int32, (L, L), 0)
    si = lax.broadcasted_iota(jnp.int32, (L, L), 1)
    tri = si <= li
    upper = jnp.where(li <= si, 1.0, 0.0)
    ones_ll = jnp.ones((L, L), F32)
    live = c * L + lax.broadcasted_iota(jnp.int32, (1, L), 1) < valid
    ones_col = jnp.where(lax.broadcasted_iota(jnp.int32, (L, D), 1) == 0, 1.0, 0.0).astype(BF16)

    for h in range(C_HEADS):
        lanes = slice(h * D, (h + 1) * D)
        logf = jnp.where(live, jax.nn.log_sigmoid(if_ref[0, C_HEADS + h:C_HEADS + h + 1, :]), 0.0)
        i_row = jnp.where(live, if_ref[0, h:h + 1, :], -jnp.inf)
        f_b = jnp.broadcast_to(logf, (L, L))
        b_col_b = jnp.dot(jnp.where(tri, f_b, 0.0), ones_ll, precision=HIGHEST,
                          preferred_element_type=F32)
        b_row = jnp.dot(f_b[0:SUBLANES], upper, precision=HIGHEST,
                        preferred_element_type=F32)[0:1]
        b_col = b_col_b[:, 0:1]
        dmat = jnp.where(tri, b_col_b - b_row + i_row, -jnp.inf)
        m_prev = m_ref[h][0:1, 0:1]
        m_t = jnp.maximum(m_prev + b_col, jnp.max(dmat, axis=1, keepdims=True))
        inter = jnp.exp(m_prev + b_col - m_t)
        qb = q_ref[0, :, lanes].astype(BF16)
        kt = kt_ref[0, lanes, :]
        vaug = jnp.concatenate([v_ref[0, :, lanes].astype(BF16), ones_col], axis=1)
        s = jnp.dot(qb, kt.astype(BF16), preferred_element_type=F32) * jnp.exp(dmat - m_t)
        state = s_ref[h]
        tot = (jnp.dot(s.astype(BF16), vaug, preferred_element_type=F32)
               + inter * jnp.dot(qb, state.astype(BF16), preferred_element_type=F32))
        hh = tot[:, :D] / jnp.maximum(jnp.abs(tot[:, D:D + 1]), jnp.exp(-m_t))
        y = hh * lax.rsqrt(jnp.mean(hh * hh, axis=1, keepdims=True) + EPS) * gc_ref[0:1, lanes]
        oc_ref[0, :, lanes] = y * jax.nn.sigmoid(co_ref[0, :, lanes])
        m_new = m_t[L - 1:L, :]
        b_last = b_col[L - 1:L, :]
        wk_row = jnp.exp(b_last - b_row + i_row - m_new)
        s_ref[h] = (jnp.exp(m_prev + b_last - m_new) * state
                    + jnp.dot((kt * wk_row).astype(BF16), vaug, preferred_element_type=F32))
        m_ref[h] = jnp.broadcast_to(m_new, (SUBLANES, LANES))

    @pl.when(c == pl.num_programs(1) - 1)
    def _():
        s_out_ref[0] = s_ref[...]
        m_out_ref[0] = m_ref[...]


def mlstm(cq, ck, cv, co, ci, cf, g_c_out, C0, n0, m0):
    n_b, n_s, width = cq.shape
    L, D = MLSTM_CHUNK, C_HD
    n_c = -(-n_s // L)
    pad = n_c * L - n_s

    def padded(a):
        return jnp.pad(a, ((0, 0), (0, pad), (0, 0))) if pad else a

    kt = jnp.swapaxes(padded(ck) * (D ** -0.5), 1, 2)
    gates = jnp.swapaxes(padded(jnp.concatenate([ci, cf], axis=-1)), 1, 2)
    oc, C1, n1, m1 = mlstm_call((padded(cq), 0), kt, (padded(cv), 0), (padded(co), 0), gates,
                                g_c_out, C0, n0, m0, n_s)
    return oc[:, :n_s], C1, n1, m1


def mlstm_call(q_src, kt, v_src, co_src, gates, g_c_out, C0, n0, m0, n_s):
    n_b, width, s_pad = kt.shape
    L, D = MLSTM_CHUNK, C_HD
    n_c = s_pad // L
    s0 = jnp.concatenate([C0, n0[..., None], jnp.zeros((n_b, C_HEADS, D, D - 1), F32)], axis=-1)
    m0b = jnp.broadcast_to(m0[:, :, None, None], (n_b, C_HEADS, SUBLANES, LANES))

    def tok_spec(off):
        return pl.BlockSpec((1, L, width), lambda b, c: (b, c, off))

    tok = tok_spec(0)
    oc, s1, m1 = pl.pallas_call(
        functools.partial(_mlstm_kernel, valid=n_s),
        grid=(n_b, n_c),
        in_specs=[tok_spec(q_src[1]), pl.BlockSpec((1, width, L), lambda b, c: (b, 0, c)),
                  tok_spec(v_src[1]), tok_spec(co_src[1]),
                  pl.BlockSpec((1, 2 * C_HEADS, L), lambda b, c: (b, 0, c)),
                  pl.BlockSpec((1, width), lambda b, c: (0, 0)),
                  pl.BlockSpec((1, C_HEADS, D, 2 * D), lambda b, c: (b, 0, 0, 0)),
                  pl.BlockSpec((1, C_HEADS, SUBLANES, LANES), lambda b, c: (b, 0, 0, 0))],
        out_specs=[tok,
                   pl.BlockSpec((1, C_HEADS, D, 2 * D), lambda b, c: (b, 0, 0, 0)),
                   pl.BlockSpec((1, C_HEADS, SUBLANES, LANES), lambda b, c: (b, 0, 0, 0))],
        out_shape=[jax.ShapeDtypeStruct((n_b, n_c * L, width), F32),
                   jax.ShapeDtypeStruct((n_b, C_HEADS, D, 2 * D), F32),
                   jax.ShapeDtypeStruct((n_b, C_HEADS, SUBLANES, LANES), F32)],
        scratch_shapes=[pltpu.VMEM((C_HEADS, D, 2 * D), F32),
                        pltpu.VMEM((C_HEADS, SUBLANES, LANES), F32)],
        compiler_params=pltpu.CompilerParams(dimension_semantics=("arbitrary", "arbitrary"),
                                             vmem_limit_bytes=VMEM_LIMIT),
        name="mlstm",
    )(q_src[0], kt, v_src[0], co_src[0], gates, g_c_out.astype(F32).reshape(1, width), s0, m0b)
    return oc, s1[..., :D], s1[..., D], m1[:, :, 0, 0]


def _row_sum_i32(mask):
    return jnp.sum(mask.astype(jnp.int32), axis=1, keepdims=True)


def _masked_softmax_pv(logits, sel, v_pages):
    m = jnp.max(jnp.where(sel, logits, NEG_BIG), axis=1, keepdims=True)
    p = jnp.where(sel, jnp.exp(logits - m), 0.0)
    l = jnp.sum(p, axis=1, keepdims=True)
    pb = p.astype(BF16)
    acc = None
    for j, vp in enumerate(v_pages):
        part = jnp.dot(pb[:, j * PAGE_SIZE:(j + 1) * PAGE_SIZE], vp, preferred_element_type=F32)
        acc = part if acc is None else acc + part
    return acc / l


def _dsa_sample_kernel(pt_ref, *refs, n_pages, top, past_len):
    del pt_ref
    n_all = n_pages + 1
    idx_refs = refs[0:n_all]
    k_refs = refs[n_all:2 * n_all]
    v_refs = refs[2 * n_all:3 * n_all]
    qi_ref, w_ref, qbd_ref, o_ref = refs[3 * n_all:3 * n_all + 4]
    n_q = w_ref.shape[2]
    n_keys = n_all * PAGE_SIZE
    rows = IDX_HEADS * n_q

    w_b = jnp.broadcast_to(w_ref[0], (IDX_HEADS, n_q, PAGE_SIZE))
    qi = qi_ref[0]
    pieces = []
    for j in range(n_all):
        kip = idx_refs[j][...].reshape(PAGE_SIZE, IDX_DIM).astype(BF16)
        d = lax.dot_general(qi, kip, (((1,), (1,)), ((), ())), preferred_element_type=F32)
        pieces.append(jnp.sum(w_b * jnp.maximum(d, 0.0).reshape(IDX_HEADS, n_q, PAGE_SIZE), axis=0))
    s = jnp.concatenate(pieces, axis=1)
    s = jnp.where(s == 0.0, 0.0, s)
    bits = pltpu.bitcast(s, jnp.int32)
    kpos = lax.broadcasted_iota(jnp.int32, (n_q, n_keys), 1)
    qpos = past_len + lax.broadcasted_iota(jnp.int32, (n_q, n_keys), 0)
    key = jnp.where(kpos <= qpos, jnp.where(bits < 0, bits ^ INT_MAX, bits), INT_MIN)

    def bis_cond(st):
        it, lo, hi, _ = st
        return ((it < BISECT_VALUE_STEPS + BISECT_INT_STEPS)
                & (jnp.max(jnp.where(hi != lo + 1, 1, 0)) > 0))

    def bis_body(st):
        it, lo, hi, cnt_lo = st
        mid = _bisect_mid(it, lo, hi)
        c = _row_sum_i32(key >= mid)
        ge = c >= top
        return (it + 1, jnp.where(ge, mid, lo), jnp.where(c == top, mid + 1, jnp.where(ge, hi, mid)),
                jnp.where(ge, c, cnt_lo))

    _, t, _, cnt_t = lax.while_loop(
        bis_cond, lambda st: bis_body(bis_body(st)),
        (jnp.int32(0), jnp.full((n_q, 1), INT_MIN, jnp.int32), jnp.max(key, axis=1, keepdims=True) + 1,
         jnp.full((n_q, 1), n_keys, jnp.int32)))

    need = (cnt_t > top) & (t > INT_MIN)
    r = top - _row_sum_i32(key > t)

    def jb(_, st):
        lo, hi = st
        mid = (lo + hi) >> 1
        ok = _row_sum_i32((key == t) & (kpos <= mid)) >= r
        return jnp.where(ok, lo, mid), jnp.where(ok, mid, hi)

    n_bits = int(math.ceil(math.log2(n_keys))) + 1
    _, jhi = lax.fori_loop(0, n_bits, jb, (jnp.full((n_q, 1), -1, jnp.int32),
                                           jnp.full((n_q, 1), n_keys - 1, jnp.int32)))
    jcut = jnp.where(need, jhi, INT_MAX)
    sel = (key > INT_MIN) & ((key > t) | ((key == t) & (kpos <= jcut)))

    qbd = qbd_ref[0]
    logits = jnp.concatenate(
        [lax.dot_general(qbd, k_refs[j][...].reshape(PAGE_SIZE, A_KV * HEAD_DIM).astype(BF16),
                         (((1,), (1,)), ((), ())), preferred_element_type=F32) for j in range(n_all)], axis=1)
    sel_rows = jnp.broadcast_to(sel[None], (A_HEADS, n_q, n_keys)).reshape(rows, n_keys)
    v_pages = [v_refs[j][...].reshape(PAGE_SIZE, A_KV * HEAD_DIM).astype(BF16) for j in range(n_all)]
    o_ref[0] = _masked_softmax_pv(logits, sel_rows, v_pages)


def _moba_sample_kernel(pt_ref, *refs, n_pages):
    del pt_ref
    n_all = n_pages + 1
    k_refs = refs[0:n_all]
    v_refs = refs[n_all:2 * n_all]
    qbd_ref, o_ref = refs[2 * n_all:2 * n_all + 2]
    rows = qbd_ref.shape[1]
    n_q = rows // B_HEADS
    width = B_KV * HEAD_DIM
    pages_per_blk = MOBA_BLOCK // PAGE_SIZE
    n_blk = n_pages // pages_per_blk
    top = min(MOBA_TOPK, n_blk + 1)
    qbd = qbd_ref[0]

    k_pages = [k_refs[j][...].reshape(PAGE_SIZE, width) for j in range(n_all)]
    means = []
    for n in range(n_blk):
        tot = k_pages[n * pages_per_blk]
        for j in range(1, pages_per_blk):
            tot = tot + k_pages[n * pages_per_blk + j]
        means.append(jnp.sum(tot, axis=0, keepdims=True) * (1.0 / MOBA_BLOCK))
    kmean = jnp.concatenate(means + [jnp.zeros((LANES - n_blk, width), F32)], axis=0).astype(BF16)
    gate = lax.dot_general(qbd, kmean, (((1,), (1,)), ((), ())), preferred_element_type=F32)
    lane = lax.broadcasted_iota(jnp.int32, (rows, LANES), 1)
    gate = jnp.where(lane < n_blk, gate, -jnp.inf)
    picked = jnp.zeros((rows, LANES), jnp.bool_)
    for _ in range(top):
        mx = jnp.max(gate, axis=1, keepdims=True)
        first = jnp.min(jnp.where(gate == mx, lane, LANES), axis=1, keepdims=True)
        hit = lane == first
        picked = picked | hit
        gate = jnp.where(hit, -jnp.inf, gate)
    picked = jnp.where(picked & (lane < n_blk), 1.0, 0.0)

    kb = [kp.astype(BF16) for kp in k_pages]
    logits = jnp.concatenate(
        [lax.dot_general(qbd, kb[j], (((1,), (1,)), ((), ())), preferred_element_type=F32)
         for j in range(n_all)], axis=1)
    q_of_row = lax.broadcasted_iota(jnp.int32, (rows, PAGE_SIZE), 0) % n_q
    own = lax.broadcasted_iota(jnp.int32, (rows, PAGE_SIZE), 1) <= q_of_row
    sel = jnp.concatenate(
        [jnp.broadcast_to(picked[:, j // pages_per_blk:j // pages_per_blk + 1] > 0.0, (rows, PAGE_SIZE))
         for j in range(n_pages)] + [own], axis=1)
    v_pages = [v_refs[j][...].reshape(PAGE_SIZE, width).astype(BF16) for j in range(n_all)]
    o_ref[0] = _masked_softmax_pv(logits, sel, v_pages)


def _block_diag_queries(q):
    n_b, n_q, n_kv, n_g, hd = q.shape
    qh = jnp.transpose(q, (0, 2, 3, 1, 4))
    eye = jnp.eye(n_kv, dtype=q.dtype)
    bd = qh[:, :, :, :, None, :] * eye[None, :, None, None, :, None]
    return bd.reshape(n_b, n_kv * n_g * n_q, n_kv * hd).astype(BF16)


def _take_diag_heads(o, n_q, n_kv, n_g):
    n_b = o.shape[0]
    o6 = o.reshape(n_b, n_kv, n_g, n_q, n_kv, HEAD_DIM)
    d = jnp.stack([o6[:, kv, :, :, kv, :] for kv in range(n_kv)], axis=1)
    return jnp.transpose(d, (0, 3, 1, 2, 4)).reshape(n_b, n_q, n_kv * n_g * HEAD_DIM)


def _page_specs(layer, n_pages, tail_shape):
    def spec(p):
        return pl.BlockSpec((1, 1, PAGE_SIZE) + tail_shape,
                            lambda b, pt, p=p: (layer, pt[b, p]) + (0,) * (1 + len(tail_shape)))
    return [spec(p) for p in range(n_pages)]


def _new_page(a):
    return jnp.pad(a, ((0, 0), (0, PAGE_SIZE - a.shape[1]), (0, 0)))


def dsa_sample(layer, page_table, cache_idx, cache_k, cache_v, aq, aiq, aiw, ak, av, aik):
    n_b, n_q = aq.shape[:2]
    n_pages = page_table.shape[1]
    past_len = n_pages * PAGE_SIZE
    width = A_KV * HEAD_DIM
    ck = cache_k.reshape(cache_k.shape[:3] + (width,))
    cv = cache_v.reshape(cache_v.shape[:3] + (width,))
    qbd = _block_diag_queries(aq * (HEAD_DIM ** -0.5))
    qi = jnp.transpose(aiq, (0, 2, 1, 3)).reshape(n_b, IDX_HEADS * n_q, IDX_DIM).astype(BF16)
    w = jnp.transpose(aiw, (0, 2, 1))[..., None]
    new = lambda tail: pl.BlockSpec((1, PAGE_SIZE) + tail, lambda b, pt: (b, 0) + (0,) * len(tail))
    full = lambda shape: pl.BlockSpec((1,) + shape, lambda b, pt: (b,) + (0,) * len(shape))
    rows = A_HEADS * n_q
    out = pl.pallas_call(
        functools.partial(_dsa_sample_kernel, n_pages=n_pages, top=min(DSA_TOPK, (past_len + n_q) // 4),
                          past_len=past_len),
        grid_spec=pltpu.PrefetchScalarGridSpec(
            num_scalar_prefetch=1, grid=(n_b,),
            in_specs=(_page_specs(layer, n_pages, (IDX_DIM,)) + [new((IDX_DIM,))]
                      + _page_specs(layer, n_pages, (width,)) + [new((width,))]
                      + _page_specs(layer, n_pages, (width,)) + [new((width,))]
                      + [full((IDX_HEADS * n_q, IDX_DIM)), full((IDX_HEADS, n_q, 1)), full((rows, width))]),
            out_specs=pl.BlockSpec((1, rows, width), lambda b, pt: (b, 0, 0))),
        out_shape=jax.ShapeDtypeStruct((n_b, rows, width), F32),
        compiler_params=pltpu.CompilerParams(dimension_semantics=("arbitrary",),
                                             vmem_limit_bytes=VMEM_LIMIT),
        name="dsa_sample",
    )(page_table, *([cache_idx] * n_pages), _new_page(aik),
      *([ck] * n_pages), _new_page(ak.reshape(n_b, n_q, width)),
      *([cv] * n_pages), _new_page(av.reshape(n_b, n_q, width)), qi, w, qbd)
    return _take_diag_heads(out, n_q, A_KV, A_GROUP)


def moba_sample(layer, page_table, cache_k, cache_v, bq, bk, bv):
    n_b, n_q = bq.shape[:2]
    n_pages = page_table.shape[1]
    width = B_KV * HEAD_DIM
    ck = cache_k.reshape(cache_k.shape[:3] + (width,))
    cv = cache_v.reshape(cache_v.shape[:3] + (width,))
    qbd = _block_diag_queries(bq * (HEAD_DIM ** -0.5))
    new = pl.BlockSpec((1, PAGE_SIZE, width), lambda b, pt: (b, 0, 0))
    rows = B_HEADS * n_q
    out = pl.pallas_call(
        functools.partial(_moba_sample_kernel, n_pages=n_pages),
        grid_spec=pltpu.PrefetchScalarGridSpec(
            num_scalar_prefetch=1, grid=(n_b,),
            in_specs=(_page_specs(layer, n_pages, (width,)) + [new]
                      + _page_specs(layer, n_pages, (width,)) + [new]
                      + [pl.BlockSpec((1, rows, width), lambda b, pt: (b, 0, 0))]),
            out_specs=pl.BlockSpec((1, rows, width), lambda b, pt: (b, 0, 0))),
        out_shape=jax.ShapeDtypeStruct((n_b, rows, width), F32),
        compiler_params=pltpu.CompilerParams(dimension_semantics=("arbitrary",),
                                             vmem_limit_bytes=VMEM_LIMIT),
        name="moba_sample",
    )(page_table, *([ck] * n_pages), _new_page(bk.reshape(n_b, n_q, width)),
      *([cv] * n_pages), _new_page(bv.reshape(n_b, n_q, width)), qbd)
    return _take_diag_heads(out, n_q, B_KV, B_GROUP)


def _rms(x, g):
    xf = x.astype(F32)
    y = xf * lax.rsqrt(jnp.mean(xf * xf, axis=-1, keepdims=True) + EPS)
    return (y * g.astype(F32)).astype(x.dtype)


def _rope(x, pos):
    half = x.shape[-1] // 2
    freqs = jnp.power(jnp.float32(ROPE_THETA), -jnp.arange(half, dtype=F32) / half)
    ang = pos.astype(F32)[:, None] * freqs[None, :]
    cos = jnp.cos(ang)[None, :, None, :]
    sin = jnp.sin(ang)[None, :, None, :]
    xf = x.astype(F32)
    x1, x2 = xf[..., :half], xf[..., half:]
    return jnp.concatenate([x1 * cos - x2 * sin, x1 * sin + x2 * cos], axis=-1).astype(x.dtype)


def _to_blocks(a, nb):
    return jnp.moveaxis(a.reshape(a.shape[0], nb, a.shape[1] // nb, *a.shape[2:]), 1, 0)


def _from_blocks(a):
    a = jnp.moveaxis(a, 0, 1)
    return a.reshape(a.shape[0], a.shape[1] * a.shape[2], *a.shape[3:])


def _dsa_attend(q, qi, wi, k_all, v_all, ki_all, q_pos):
    n_q = q.shape[1]
    n_k = k_all.shape[1]
    top = min(DSA_TOPK, n_k // 4)
    blk = Q_BLOCK if n_q % Q_BLOCK == 0 else n_q
    nb = n_q // blk
    k_pos = jnp.arange(n_k, dtype=jnp.int32)
    scale = HEAD_DIM ** -0.5
    ki_f = ki_all.astype(F32)

    def block(args):
        qb, qib, wib, pb = args
        s = jax.nn.relu(jnp.einsum('bqhd,bld->bqhl', qib.astype(F32), ki_f))
        score = jnp.einsum('bqhl,bqh->bql', s, wib.astype(F32))
        score = jnp.where(k_pos[None, None, :] <= pb[None, :, None], score, -jnp.inf)
        _, sel = lax.top_k(score, top)
        valid = sel <= pb[None, :, None]
        kg = jax.vmap(lambda kk, ii: kk[ii])(k_all, sel)
        vg = jax.vmap(lambda vv, ii: vv[ii])(v_all, sel)
        logits = jnp.einsum('bqkgd,bqskd->bqkgs', qb.astype(F32), kg.astype(F32)) * scale
        logits = jnp.where(valid[:, :, None, None, :], logits, -jnp.inf)
        p = jax.nn.softmax(logits, axis=-1)
        return jnp.einsum('bqkgs,bqskd->bqkgd', p, vg.astype(F32)).astype(q.dtype)

    out = lax.map(block, (_to_blocks(q, nb), _to_blocks(qi, nb), _to_blocks(wi, nb), q_pos.reshape(nb, blk)))
    return _from_blocks(out)


def _moba_attend(q, k_all, v_all, q_pos):
    n_b, n_q = q.shape[:2]
    n_k = k_all.shape[1]
    n_blk = -(-n_k // MOBA_BLOCK)
    pad = n_blk * MOBA_BLOCK - n_k
    kb = jnp.pad(k_all, ((0, 0), (0, pad), (0, 0), (0, 0))).reshape(n_b, n_blk, MOBA_BLOCK, B_KV, HEAD_DIM)
    vb = jnp.pad(v_all, ((0, 0), (0, pad), (0, 0), (0, 0))).reshape(n_b, n_blk, MOBA_BLOCK, B_KV, HEAD_DIM)
    k_mean = jnp.mean(kb.astype(F32), axis=2)
    kb_h = jnp.moveaxis(kb, 3, 1)
    vb_h = jnp.moveaxis(vb, 3, 1)
    top = min(MOBA_TOPK, n_blk)
    blk = MOBA_Q_BLOCK if n_q % MOBA_Q_BLOCK == 0 else n_q
    nb = n_q // blk
    blk_ids = jnp.arange(n_blk, dtype=jnp.int32)
    in_blk = jnp.arange(MOBA_BLOCK, dtype=jnp.int32)
    b_ix = jnp.arange(n_b)
    kv_ix = jnp.arange(B_KV)
    scale = HEAD_DIM ** -0.5

    def block(args):
        qb, pb = args
        qf = qb.astype(F32)
        own = pb // MOBA_BLOCK
        gate = jnp.einsum('bqkgd,bnkd->bqkgn', qf, k_mean)
        past = blk_ids[None, :] < own[:, None]
        gate = jnp.where(past[None, :, None, None, :], gate, -jnp.inf)
        _, sel = lax.top_k(gate, top)
        sel_ok = sel < own[None, :, None, None, None]
        bi = b_ix[:, None, None, None, None]
        ki = kv_ix[None, None, :, None, None]
        kg = kb_h[bi, ki, sel]
        vg = vb_h[bi, ki, sel]
        l_sel = jnp.einsum('bqkgd,bqkgnsd->bqkgns', qf, kg.astype(F32)) * scale
        l_sel = jnp.where(sel_ok[..., None], l_sel, -jnp.inf).reshape(n_b, blk, B_KV, B_GROUP, top * MOBA_BLOCK)
        ko = kb[b_ix[:, None], own[None, :]]
        vo = vb[b_ix[:, None], own[None, :]]
        l_own = jnp.einsum('bqkgd,bqskd->bqkgs', qf, ko.astype(F32)) * scale
        own_pos = own[:, None] * MOBA_BLOCK + in_blk[None, :]
        l_own = jnp.where((own_pos <= pb[:, None])[None, :, None, None, :], l_own, -jnp.inf)
        p = jax.nn.softmax(jnp.concatenate([l_sel, l_own], axis=-1), axis=-1)
        p_sel = p[..., :top * MOBA_BLOCK].reshape(n_b, blk, B_KV, B_GROUP, top, MOBA_BLOCK)
        p_own = p[..., top * MOBA_BLOCK:]
        out = (jnp.einsum('bqkgns,bqkgnsd->bqkgd', p_sel, vg.astype(F32))
               + jnp.einsum('bqkgs,bqskd->bqkgd', p_own, vo.astype(F32)))
        return out.astype(q.dtype)

    out = lax.map(block, (_to_blocks(q, nb), q_pos.reshape(nb, blk)))
    return _from_blocks(out)


def _mlstm(q, k, v, i_pre, f_pre, C0, n0, m0):
    n_s = q.shape[1]
    L = C_CHUNK if n_s % C_CHUNK == 0 else n_s
    nc = n_s // L
    q = q.astype(F32)
    k = k.astype(F32) * (C_HD ** -0.5)
    v = v.astype(F32)
    i_pre = i_pre.astype(F32)
    logf = jax.nn.log_sigmoid(f_pre.astype(F32))
    tri = jnp.tril(jnp.ones((L, L), dtype=bool))

    def step(carry, xs):
        C, n, m = carry
        qc, kc, vc, ic, fc = xs
        b = jnp.moveaxis(jnp.cumsum(fc, axis=1), 1, 2)
        it = jnp.moveaxis(ic, 1, 2)
        dmat = jnp.where(tri, b[..., :, None] - b[..., None, :] + it[..., None, :], -jnp.inf)
        m_t = jnp.maximum(m[..., None] + b, jnp.max(dmat, axis=-1))
        inter = jnp.exp(m[..., None] + b - m_t)
        s = jnp.einsum('blhd,bshd->bhls', qc, kc) * jnp.exp(dmat - m_t[..., None])
        num = jnp.einsum('bhls,bshd->bhld', s, vc) + inter[..., None] * jnp.einsum('blhd,bhde->bhle', qc, C)
        den = jnp.sum(s, axis=-1) + inter * jnp.einsum('blhd,bhd->bhl', qc, n)
        h = num / jnp.maximum(jnp.abs(den), jnp.exp(-m_t))[..., None]
        m_new = m_t[..., -1]
        decay = jnp.exp(m + b[..., -1] - m_new)
        wk = jnp.exp(b[..., -1:] - b + it - m_new[..., None])
        C_new = decay[..., None, None] * C + jnp.einsum('bhs,bshd,bshe->bhde', wk, kc, vc)
        n_new = decay[..., None] * n + jnp.einsum('bhs,bshd->bhd', wk, kc)
        return (C_new, n_new, m_new), jnp.moveaxis(h, 1, 2)

    xs = (_to_blocks(q, nc), _to_blocks(k, nc), _to_blocks(v, nc), _to_blocks(i_pre, nc), _to_blocks(logf, nc))
    (C, n, m), h = lax.scan(step, (C0.astype(F32), n0.astype(F32), m0.astype(F32)), xs)
    return _from_blocks(h), C, n, m


def _peer(h_in, g_ffn, w_q, subkeys, u_tab, v_tab):
    n_b, n_s, d = h_in.shape
    hq = _mm3(h_in, w_q, g_ffn)
    hn = _rms(h_in, g_ffn)
    hf = hn.reshape(n_b * n_s, d)
    qf = hq.reshape(n_b * n_s, -1)
    n_tok = hf.shape[0]
    blk = min(TOKEN_BLOCK, n_tok)
    nb = n_tok // blk
    half = PEER_QDIM // 2
    sk = subkeys.astype(F32)

    def block(args):
        hb, qb = args
        q = qb.reshape(blk, PEER_HEADS, 2, half).astype(F32)
        s = jnp.einsum('nhpd,hpkd->nhpk', q, sk)
        sv, si = lax.top_k(s, PEER_TOPK)
        cand = (sv[:, :, 0, :, None] + sv[:, :, 1, None, :]).reshape(blk, PEER_HEADS, PEER_TOPK * PEER_TOPK)
        cid = (si[:, :, 0, :, None] * PEER_NKEYS + si[:, :, 1, None, :]).reshape(blk, PEER_HEADS, PEER_TOPK * PEER_TOPK)
        tv, ti = lax.top_k(cand, PEER_TOPK)
        eid = jnp.take_along_axis(cid, ti, axis=-1)
        g = jax.nn.softmax(tv, axis=-1)
        u = u_tab[eid].astype(F32)
        a = jax.nn.gelu(jnp.einsum('nd,nhkd->nhk', hb.astype(F32), u))
        out = jnp.einsum('nhk,nhkd->nd', g * a, v_tab[eid].astype(F32))
        return out.astype(h_in.dtype)

    out = lax.map(block, (hf.reshape(nb, blk, d), qf.reshape(nb, blk, -1))).reshape(nb * blk, d)
    return out.reshape(n_b, n_s, d)


def _heads_t(a):
    return jnp.transpose(a.astype(BF16), (0, 2, 3, 1))


def _values_t(v, tk):
    n_b, n_s, n_kv, d = v.shape
    vt = jnp.transpose(v.astype(BF16), (0, 2, 3, 1)).reshape(n_b, n_kv, d, n_s // tk, tk)
    return jnp.transpose(vt, (0, 1, 3, 2, 4))


def _dsa_prompt_glue(aq, aiq, aiw, ak, av, aik):
    n_b, n_s = aq.shape[:2]
    scale = HEAD_DIM ** -0.5
    qt = _heads_t(aq.reshape(n_b, n_s, A_HEADS, HEAD_DIM) * scale)
    return dsa_prompt(_heads_t(aiq), jnp.transpose(aiw, (0, 2, 1)), aik.astype(BF16), qt,
                      jnp.transpose(ak.astype(BF16), (0, 2, 1, 3)), _values_t(av, ATT_TK),
                      top=min(DSA_TOPK, n_s // 4))


def _moba_prompt_glue(bq, bk, bv):
    n_b, n_s = bq.shape[:2]
    scale = HEAD_DIM ** -0.5
    qt = _heads_t(bq.reshape(n_b, n_s, B_HEADS, HEAD_DIM) * scale)
    return moba_prompt(qt, jnp.transpose(bk.astype(BF16), (0, 2, 1, 3)), _values_t(bv, MOBA_BLOCK))


def _mix_ffn(x, p_l, lw, oa, ob, oc, gate, gate_block):
    n_b, n_s, d = x.shape
    flat = lambda a: a.reshape(n_b * n_s, a.shape[-1])
    x2 = mix_out(flat(x), flat(oa), flat(ob), flat(oc), flat(gate), gate_block, lw)
    x2 = peer(x2, lw['g_ffn'], lw['w_peer_q'], lw['peer_subkeys'], lw['peer_u_b'], lw['peer_vt_b'])
    return ple(x2, flat(p_l), lw).reshape(n_b, n_s, d)


def _layer_prompt(x, p_l, lw):
    n_b, n_s, _ = x.shape
    pr = prompt_projection(x, lw)
    oa = dsa_prompt(pr['qi'], pr['wi'], pr['ki_h'], pr['qa'], pr['ka_h'], pr['va_t'],
                    top=min(DSA_TOPK, n_s // 4))
    ob = moba_prompt(pr['qb'], pr['kb_h'], pr['vb_t'])
    zc = pr['zc']
    c0 = N_BRANCH * D_MODEL // C_WIDTH
    oc, C1, n1, m1 = mlstm_call((zc, c0), pr['kc_t'], (zc, c0 + 1), (zc, c0 + 2), pr['gates'], lw['g_c_out'],
                                jnp.zeros((n_b, C_HEADS, C_HD, C_HD), F32),
                                jnp.zeros((n_b, C_HEADS, C_HD), F32), jnp.zeros((n_b, C_HEADS), F32), n_s)
    x = _mix_ffn(x, p_l, lw, oa, ob, oc, zc, 0)
    kv4 = lambda a: a.reshape(n_b, n_s, A_KV, HEAD_DIM)
    return x, (kv4(pr['ka']), kv4(pr['va']), pr['ki'], kv4(pr['kb']), kv4(pr['vb']), C1, n1, m1)


def _layer_sample(x, p_l, pos, lw, past):
    n_b, n_s, _ = x.shape
    z = _mm3(x, lw['w_in'], lw['g_mix'])
    (aq, ak, av, aiq, aiw, aik, bq, bk, bv, cq, ck, cv, co, ci, cf, gate) = jnp.split(z, IN_SPLITS, axis=-1)
    aq = _rope(_rms(aq.reshape(n_b, n_s, A_HEADS, HEAD_DIM), lw['g_qa']), pos).reshape(n_b, n_s, A_KV, A_GROUP, HEAD_DIM)
    ak = _rope(_rms(ak.reshape(n_b, n_s, A_KV, HEAD_DIM), lw['g_ka']), pos)
    av = av.reshape(n_b, n_s, A_KV, HEAD_DIM)
    aiq = _rope(aiq.reshape(n_b, n_s, IDX_HEADS, IDX_DIM), pos)
    aik = _rope(aik.reshape(n_b, n_s, 1, IDX_DIM), pos)[:, :, 0]
    aiw = aiw * IDX_W_SCALE
    bq = _rope(_rms(bq.reshape(n_b, n_s, B_HEADS, HEAD_DIM), lw['g_qb']), pos).reshape(n_b, n_s, B_KV, B_GROUP, HEAD_DIM)
    bk = _rope(_rms(bk.reshape(n_b, n_s, B_KV, HEAD_DIM), lw['g_kb']), pos)
    bv = bv.reshape(n_b, n_s, B_KV, HEAD_DIM)
    cq = cq.reshape(n_b, n_s, C_HEADS, C_HD)
    ck = ck.reshape(n_b, n_s, C_HEADS, C_HD)
    cv = cv.reshape(n_b, n_s, C_HEADS, C_HD)
    ci = ci + lw['b_if'][:C_HEADS]
    cf = cf + lw['b_if'][C_HEADS:]
    oa = dsa_sample(past['layer'], past['page_table'], past['a_idx'], past['a_k'], past['a_v'],
                    aq, aiq, aiw, ak, av, aik)
    ob = moba_sample(past['layer'], past['page_table'], past['b_k'], past['b_v'], bq, bk, bv)
    oc, C1, n1, m1 = mlstm(cq.reshape(n_b, n_s, C_WIDTH), ck.reshape(n_b, n_s, C_WIDTH),
                           cv.reshape(n_b, n_s, C_WIDTH), co, ci, cf, lw['g_c_out'],
                           past['c_C'], past['c_n'], past['c_m'])
    return _mix_ffn(x, p_l, lw, oa, ob, oc, gate, 0), (ak, av, aik, bk, bv, C1, n1, m1)


def kernel(x_prompt, x_sample, p_prompt, p_sample, cache_a_k, cache_a_v, cache_a_idx, cache_b_k, cache_b_v,
           state_c_C, state_c_n, state_c_m, page_table, g_mix, w_in, b_if, g_qa, g_ka, g_qb, g_kb, g_c_out,
           w_br_a, w_br_b, w_br_c, w_out, g_ffn, w_peer_q, peer_subkeys, peer_u, peer_v, w_ple, w_ple_gate):
    past_len = page_table.shape[1] * PAGE_SIZE
    pos_s = past_len + jnp.arange(x_sample.shape[1], dtype=jnp.int32)
    yp, ys = x_prompt, x_sample
    states_p, states_s = [], []
    for l in range(DEPTH):
        lw = {'g_mix': g_mix[l], 'w_in': w_in[l], 'b_if': b_if[l], 'g_qa': g_qa[l], 'g_ka': g_ka[l],
              'g_qb': g_qb[l], 'g_kb': g_kb[l], 'g_c_out': g_c_out[l], 'w_br_a': w_br_a[l], 'w_br_b': w_br_b[l],
              'w_br_c': w_br_c[l], 'w_out': w_out[l], 'g_ffn': g_ffn[l], 'w_peer_q': w_peer_q[l],
              'peer_subkeys': peer_subkeys[l], 'peer_u': peer_u[l], 'peer_v': peer_v[l], 'w_ple': w_ple[l],
              'w_ple_gate': w_ple_gate[l]}
        lw['peer_u_b'], lw['peer_vt_b'] = peer_tables(peer_u[l], peer_v[l])
        yp, st_p = _layer_prompt(yp, p_prompt[l], lw)
        past = {'layer': l, 'page_table': page_table, 'a_k': cache_a_k, 'a_v': cache_a_v, 'a_idx': cache_a_idx,
                'b_k': cache_b_k, 'b_v': cache_b_v,
                'c_C': state_c_C[l], 'c_n': state_c_n[l], 'c_m': state_c_m[l]}
        ys, st_s = _layer_sample(ys, p_sample[l], pos_s, lw, past)
        states_p.append(st_p)
        states_s.append(st_s)
    (pa_k, pa_v, pa_i, pb_k, pb_v, pc_C, pc_n, pc_m) = [jnp.stack(a) for a in zip(*states_p)]
    (sa_k, sa_v, sa_i, sb_k, sb_v, sc_C, sc_n, sc_m) = [jnp.stack(a) for a in zip(*states_s)]
    return (yp, ys, pa_k, pa_v, pa_i, pb_k, pb_v, pc_C, pc_n, pc_m,
            sa_k, sa_v, sa_i, sb_k, sb_v, sc_C, sc_n, sc_m)
```

```python
import functools
import math

import jax
import jax.numpy as jnp
import numpy as np
from jax import lax
from jax.experimental import pallas as pl
from jax.experimental.pallas import tpu as pltpu

D_MODEL = 1024
DEPTH = 4
PAGE_SIZE = 128
HEAD_DIM = 64
A_HEADS = 8
A_KV = 4
A_GROUP = A_HEADS // A_KV
A_WIDTH = A_HEADS * HEAD_DIM
IDX_HEADS = 8
IDX_DIM = 64
IDX_W_SCALE = (IDX_HEADS ** -0.5) * (IDX_DIM ** -0.5)
DSA_TOPK = 256
B_HEADS = 8
B_KV = 4
B_GROUP = B_HEADS // B_KV
B_WIDTH = B_HEADS * HEAD_DIM
MOBA_BLOCK = 256
MOBA_TOPK = 3
C_HEADS = 4
C_HD = 128
C_WIDTH = C_HEADS * C_HD
C_CHUNK = 64
N_BRANCH = 3
PEER_HEADS = 8
PEER_NKEYS = 128
PEER_QDIM = 256
PEER_TOPK = 16
PLE_DIM = 256
ROPE_THETA = 10000.0
EPS = 1e-6

IN_SIZES = (A_WIDTH, A_KV * HEAD_DIM, A_KV * HEAD_DIM, IDX_HEADS * IDX_DIM, IDX_HEADS, IDX_DIM,
            B_WIDTH, B_KV * HEAD_DIM, B_KV * HEAD_DIM,
            C_WIDTH, C_WIDTH, C_WIDTH, C_WIDTH, C_HEADS, C_HEADS,
            N_BRANCH * D_MODEL)
IN_SPLITS = tuple(int(s) for s in np.cumsum(IN_SIZES)[:-1])

F32 = jnp.float32
BF16 = jnp.bfloat16

LANES = 128
SUBLANES = 8
VMEM_LIMIT = 48 * 1024 * 1024


def _mm_kernel(x_ref, g_ref, w_ref, o_ref, h_ref, *, normalize):
    @pl.when(pl.program_id(1) == 0)
    def _():
        x = x_ref[...]
        if normalize:
            x = x * lax.rsqrt(jnp.mean(x * x, axis=-1, keepdims=True) + EPS) * g_ref[...]
        h_ref[...] = x.astype(BF16)

    o_ref[...] = jnp.dot(h_ref[...], w_ref[...], preferred_element_type=F32)


def _pick_tile(n, cands):
    for c in cands:
        if n % c == 0:
            return c
    return n


def matmul(x, w, g=None):
    m, k = x.shape
    n = w.shape[1]
    n_pad = -(-n // LANES) * LANES
    wb = w.astype(BF16)
    if n_pad != n:
        wb = jnp.pad(wb, ((0, 0), (0, n_pad - n)))
    tm = _pick_tile(m, (1024, 512, 256, 128))
    tn = _pick_tile(n_pad, (512, 256, 128))
    gg = (jnp.ones((k,), F32) if g is None else g.astype(F32)).reshape(1, k)
    out = pl.pallas_call(
        functools.partial(_mm_kernel, normalize=g is not None),
        grid=(m // tm, n_pad // tn),
        in_specs=[pl.BlockSpec((tm, k), lambda i, j: (i, 0)),
                  pl.BlockSpec((1, k), lambda i, j: (0, 0)),
                  pl.BlockSpec((k, tn), lambda i, j: (0, j))],
        out_specs=pl.BlockSpec((tm, tn), lambda i, j: (i, j)),
        out_shape=jax.ShapeDtypeStruct((m, n_pad), F32),
        scratch_shapes=[pltpu.VMEM((tm, k), BF16)],
        compiler_params=pltpu.CompilerParams(dimension_semantics=("arbitrary", "arbitrary"),
                                             vmem_limit_bytes=VMEM_LIMIT),
        name="matmul",
    )(x, gg, wb)
    return out[:, :n] if n_pad != n else out


def _mix_out_kernel(x_ref, oa_ref, ob_ref, oc_ref, ga_ref, gb_ref, gc_ref, wa_ref, wb_ref, wc_ref, wo_ref, o_ref):
    mix = None
    for o_b, g_b, w_b in ((oa_ref, ga_ref, wa_ref), (ob_ref, gb_ref, wb_ref), (oc_ref, gc_ref, wc_ref)):
        term = jax.nn.sigmoid(g_b[...]) * jnp.dot(o_b[...].astype(BF16), w_b[...], preferred_element_type=F32)
        mix = term if mix is None else mix + term
    o_ref[...] = x_ref[...] + jnp.dot(mix.astype(BF16), wo_ref[...], preferred_element_type=F32)


def mix_out(x, oa, ob, oc, gate, gate_block, lw):
    n_tok, d = x.shape
    tm = _pick_tile(n_tok, (512, 256, 128))
    tok = lambda n: pl.BlockSpec((tm, n), lambda i: (i, 0))
    gate_spec = lambda j: pl.BlockSpec((tm, d), lambda i: (i, gate_block + j))
    const = lambda a: pl.BlockSpec(a.shape, lambda i: (0, 0))
    ws = [lw[k].astype(BF16) for k in ('w_br_a', 'w_br_b', 'w_br_c', 'w_out')]
    return pl.pallas_call(
        _mix_out_kernel,
        grid=(n_tok // tm,),
        in_specs=[tok(d), tok(oa.shape[1]), tok(ob.shape[1]), tok(oc.shape[1]),
                  gate_spec(0), gate_spec(1), gate_spec(2)] + [const(w) for w in ws],
        out_specs=tok(d),
        out_shape=jax.ShapeDtypeStruct((n_tok, d), F32),
        compiler_params=pltpu.CompilerParams(dimension_semantics=("arbitrary",), vmem_limit_bytes=VMEM_LIMIT),
        name="mix_out",
    )(x, oa, ob, oc, gate, gate, gate, *ws)


def _ple_kernel(x_ref, p_ref, wg_ref, wp_ref, o_ref):
    x = x_ref[...]
    gate = jax.nn.sigmoid(jnp.dot(x.astype(BF16), wg_ref[...], preferred_element_type=F32))
    o_ref[...] = x + gate * jnp.dot(p_ref[...].astype(BF16), wp_ref[...], preferred_element_type=F32)


def ple(x, p, lw):
    n_tok, d = x.shape
    tm = _pick_tile(n_tok, (512, 256, 128))
    wg, wp = lw['w_ple_gate'].astype(BF16), lw['w_ple'].astype(BF16)
    return pl.pallas_call(
        _ple_kernel,
        grid=(n_tok // tm,),
        in_specs=[pl.BlockSpec((tm, d), lambda i: (i, 0)), pl.BlockSpec((tm, p.shape[1]), lambda i: (i, 0)),
                  pl.BlockSpec(wg.shape, lambda i: (0, 0)), pl.BlockSpec(wp.shape, lambda i: (0, 0))],
        out_specs=pl.BlockSpec((tm, d), lambda i: (i, 0)),
        out_shape=jax.ShapeDtypeStruct((n_tok, d), F32),
        compiler_params=pltpu.CompilerParams(dimension_semantics=("arbitrary",), vmem_limit_bytes=VMEM_LIMIT),
        name="ple",
    )(x, p, wg, wp)


def _mm3(x, w, g=None):
    b, s, k = x.shape
    return matmul(x.reshape(b * s, k), w, g).reshape(b, s, w.shape[1])


PROJ_TM = 512
HALF = HEAD_DIM // 2
T_ROWS = (A_WIDTH, IDX_HEADS * IDX_DIM, B_WIDTH, A_KV * HEAD_DIM, B_KV * HEAD_DIM, C_WIDTH, 2 * SUBLANES)
T_OFFS = tuple(int(v) for v in np.cumsum((0,) + T_ROWS))


def _rms_rows(x, g):
    return x * lax.rsqrt(jnp.mean(x * x, axis=0, keepdims=True) + EPS) * g


def _rope_rows(x, cos, sin):
    x1, x2 = x[:HALF], x[HALF:]
    return jnp.concatenate([x1 * cos - x2 * sin, x1 * sin + x2 * cos], axis=0)


def _proj_t_kernel(x_ref, g_ref, wt_ref, cos_ref, sin_ref, gq_ref, bias_ref,
                   qa_ref, qi_ref, wi_ref, qb_ref, va_ref, vb_ref, kc_ref, if_ref):
    x = x_ref[0]
    hb = (x * lax.rsqrt(jnp.mean(x * x, axis=-1, keepdims=True) + EPS) * g_ref[...]).astype(BF16)
    tm = hb.shape[0]
    cos, sin = cos_ref[...], sin_ref[...]
    scale = HEAD_DIM ** -0.5

    def seg(k):
        return lax.dot_general(wt_ref[T_OFFS[k]:T_OFFS[k + 1], :], hb, (((1,), (1,)), ((), ())),
                               preferred_element_type=F32)

    for k, (out_ref, gain) in enumerate(((qa_ref, 0), (qi_ref, None), (qb_ref, 1))):
        z = seg((0, 1, 2)[k])
        for h in range(A_HEADS):
            xh = z[h * HEAD_DIM:(h + 1) * HEAD_DIM]
            if gain is not None:
                xh = _rms_rows(xh, gq_ref[gain]) * scale
            out_ref[0, h] = _rope_rows(xh, cos, sin).astype(BF16)
    for k, out_ref in ((3, va_ref), (4, vb_ref)):
        z = seg(k).astype(BF16)
        for kv in range(A_KV):
            for j in range(tm // ATT_TK):
                out_ref[0, kv, j] = z[kv * HEAD_DIM:(kv + 1) * HEAD_DIM, j * ATT_TK:(j + 1) * ATT_TK]
    kc_ref[0] = seg(5) * (C_HD ** -0.5)
    small = seg(6)
    wi_ref[0] = small[0:IDX_HEADS] * IDX_W_SCALE
    if_ref[0] = small[IDX_HEADS:] + bias_ref[...]


def _rope_lanes(x, cos, sin_lo, sin_hi):
    return x * cos + pltpu.roll(x, LANES - HALF, 1) * sin_lo + pltpu.roll(x, HALF, 1) * sin_hi


def _proj_kv_kernel(x_ref, g_ref, w_ref, cos_ref, sinlo_ref, sinhi_ref, gk_ref, bd_ref,
                    ka_ref, va_ref, ki_ref, kb_ref, vb_ref, kah_ref, kih_ref, kbh_ref):
    x = x_ref[0]
    hb = (x * lax.rsqrt(jnp.mean(x * x, axis=-1, keepdims=True) + EPS) * g_ref[...]).astype(BF16)
    cos, sin_lo, sin_hi = cos_ref[...], sinlo_ref[...], sinhi_ref[...]
    width = A_KV * HEAD_DIM

    def seg(c0, n):
        return jnp.dot(hb, w_ref[:, c0:c0 + n], preferred_element_type=F32)

    def rope(z):
        return jnp.concatenate([_rope_lanes(z[:, j * LANES:(j + 1) * LANES], cos, sin_lo, sin_hi)
                                for j in range(z.shape[1] // LANES)], axis=1)

    for k, (f_ref, h_ref) in enumerate(((ka_ref, kah_ref), (kb_ref, kbh_ref))):
        z = seg(k * width, width)
        ms = jnp.dot(z * z, bd_ref[...], precision=HIGHEST, preferred_element_type=F32)
        z = rope(z * lax.rsqrt(ms + EPS) * gk_ref[k:k + 1, :])
        f_ref[0] = z
        zb = z.astype(BF16)
        for kv in range(A_KV):
            h_ref[0, kv] = zb[:, kv * HEAD_DIM:(kv + 1) * HEAD_DIM]
    va_ref[0] = seg(2 * width, width)
    vb_ref[0] = seg(3 * width, width)
    zi = rope(seg(4 * width, LANES))[:, :IDX_DIM]
    ki_ref[0] = zi
    kih_ref[0] = zi.astype(BF16)


def _rope_tables(n_s):
    freqs = jnp.power(jnp.float32(ROPE_THETA), -jnp.arange(HALF, dtype=F32) / HALF)
    ang = jnp.arange(n_s, dtype=F32)[:, None] * freqs[None, :]
    return jnp.cos(ang), jnp.sin(ang)


def prompt_projection(x, lw):
    n_b, n_s, d = x.shape
    tm = PROJ_TM
    n_t = n_s // tm
    w = lw['w_in']
    cols = dict(zip(('aq', 'ak', 'av', 'aiq', 'aiw', 'aik', 'bq', 'bk', 'bv', 'cq', 'ck', 'cv', 'co', 'ci', 'cf',
                     'gate'), jnp.split(w, IN_SPLITS, axis=1)))
    g = lw['g_mix'].astype(F32).reshape(1, d)
    cos, sin = _rope_tables(n_s)
    x_spec = pl.BlockSpec((1, tm, d), lambda b, i: (b, i, 0))
    g_spec = pl.BlockSpec((1, d), lambda b, i: (0, 0))
    params = pltpu.CompilerParams(dimension_semantics=("arbitrary", "arbitrary"), vmem_limit_bytes=VMEM_LIMIT)

    wt = jnp.concatenate([cols[k] for k in ('aq', 'aiq', 'bq', 'av', 'bv', 'ck', 'aiw', 'ci', 'cf')],
                         axis=1).T.astype(BF16)
    gq = jnp.stack([jnp.broadcast_to(lw[k].astype(F32)[:, None], (HEAD_DIM, tm)) for k in ('g_qa', 'g_qb')])
    bias = jnp.broadcast_to(lw['b_if'].astype(F32)[:, None], (2 * C_HEADS, tm))
    heads_t = lambda: pl.BlockSpec((1, A_HEADS, HEAD_DIM, tm), lambda b, i: (b, 0, 0, i))
    vals_t = lambda: pl.BlockSpec((1, A_KV, tm // ATT_TK, HEAD_DIM, ATT_TK), lambda b, i: (b, 0, i, 0, 0))
    rows_t = lambda n: pl.BlockSpec((1, n, tm), lambda b, i: (b, 0, i))
    tab_t = pl.BlockSpec((HALF, tm), lambda b, i: (0, i))
    const = lambda shape: pl.BlockSpec(shape, lambda b, i: (0,) * len(shape))
    heads_shape = jax.ShapeDtypeStruct((n_b, A_HEADS, HEAD_DIM, n_s), BF16)
    vals_shape = jax.ShapeDtypeStruct((n_b, A_KV, n_s // ATT_TK, HEAD_DIM, ATT_TK), BF16)
    qa, qi, wi, qb, va_t, vb_t, kc_t, gates = pl.pallas_call(
        _proj_t_kernel,
        grid=(n_b, n_t),
        in_specs=[x_spec, g_spec, const(wt.shape), tab_t, tab_t, const(gq.shape), const(bias.shape)],
        out_specs=[heads_t(), heads_t(), rows_t(IDX_HEADS), heads_t(), vals_t(), vals_t(),
                   rows_t(C_WIDTH), rows_t(2 * C_HEADS)],
        out_shape=[heads_shape, heads_shape, jax.ShapeDtypeStruct((n_b, IDX_HEADS, n_s), F32), heads_shape,
                   vals_shape, vals_shape, jax.ShapeDtypeStruct((n_b, C_WIDTH, n_s), F32),
                   jax.ShapeDtypeStruct((n_b, 2 * C_HEADS, n_s), F32)],
        compiler_params=params,
        name="proj_t",
    )(x, g, wt, cos.T, sin.T, gq, bias)

    width = A_KV * HEAD_DIM
    wkv = jnp.concatenate([cols['ak'], cols['bk'], cols['av'], cols['bv'], cols['aik'],
                           jnp.zeros((d, LANES - IDX_DIM), w.dtype)], axis=1).astype(BF16)
    lane = jnp.arange(LANES)
    cos_l = jnp.tile(cos, (1, LANES // HALF))
    sin_l = jnp.tile(sin, (1, LANES // HALF))
    sin_lo = jnp.where((lane % HEAD_DIM) < HALF, -sin_l, 0.0)
    sin_hi = jnp.where((lane % HEAD_DIM) >= HALF, sin_l, 0.0)
    gk = jnp.stack([jnp.tile(lw[k].astype(F32), A_KV) for k in ('g_ka', 'g_kb')])
    head_of = jnp.arange(width) // HEAD_DIM
    bd = jnp.where(head_of[:, None] == head_of[None, :], 1.0 / HEAD_DIM, 0.0).astype(F32)
    tab = pl.BlockSpec((tm, LANES), lambda b, i: (i, 0))
    tok = lambda n: pl.BlockSpec((1, tm, n), lambda b, i: (b, i, 0))
    heads = lambda: pl.BlockSpec((1, A_KV, tm, HEAD_DIM), lambda b, i: (b, 0, i, 0))
    tok_shape = lambda n, dt=F32: jax.ShapeDtypeStruct((n_b, n_s, n), dt)
    heads_kv = jax.ShapeDtypeStruct((n_b, A_KV, n_s, HEAD_DIM), BF16)
    ka, va, ki, kb, vb, ka_h, ki_h, kb_h = pl.pallas_call(
        _proj_kv_kernel,
        grid=(n_b, n_t),
        in_specs=[x_spec, g_spec, const(wkv.shape), tab, tab, tab, const(gk.shape), const(bd.shape)],
        out_specs=[tok(width), tok(width), tok(IDX_DIM), tok(width), tok(width), heads(), tok(IDX_DIM), heads()],
        out_shape=[tok_shape(width), tok_shape(width), tok_shape(IDX_DIM), tok_shape(width), tok_shape(width),
                   heads_kv, tok_shape(IDX_DIM, BF16), heads_kv],
        compiler_params=params,
        name="proj_kv",
    )(x, g, wkv, cos_l, sin_lo, sin_hi, gk, bd)

    wc = jnp.concatenate([cols['gate'], cols['cq'], cols['cv'], cols['co']], axis=1)
    zc = matmul(x.reshape(n_b * n_s, d), wc, lw['g_mix']).reshape(n_b, n_s, -1)
    return dict(qa=qa, qi=qi, wi=wi, qb=qb, va_t=va_t, vb_t=vb_t, kc_t=kc_t, gates=gates,
                ka=ka, va=va, ki=ki, kb=kb, vb=vb, ka_h=ka_h, ki_h=ki_h, kb_h=kb_h, zc=zc)


INT_MIN = -2 ** 31
INT_MAX = 2 ** 31 - 1
NEG_BIG = -1e30
ATT_TQ = 128
ATT_TK = 256
BISECT_VALUE_STEPS = 12
BISECT_INT_STEPS = 33


def _key_value(k):
    return pltpu.bitcast(jnp.where(k < 0, k ^ INT_MAX, k), F32)


def _bisect_mid(it, lo, hi):
    mid = (lo >> 1) + (hi >> 1) + (lo & hi & 1)
    vbits = pltpu.bitcast(0.5 * (_key_value(lo) + _key_value(hi)), jnp.int32)
    vmid = jnp.where(vbits < 0, vbits ^ INT_MAX, vbits)
    vmid_cap = jnp.where(it < BISECT_VALUE_STEPS, INT_MAX, INT_MIN)
    return jnp.where((vmid > lo) & (vmid < jnp.minimum(hi, vmid_cap)), vmid, mid)


def _col_sum_i32(mask):
    rows, tq = mask.shape
    return jnp.sum(mask.astype(jnp.int32).reshape(rows // SUBLANES, SUBLANES, tq), axis=0)


def _attend_init(m_ref, l_ref, acc_ref):
    m_ref[...] = jnp.full(m_ref.shape, NEG_BIG, F32)
    l_ref[...] = jnp.zeros(l_ref.shape, F32)
    acc_ref[...] = jnp.zeros(acc_ref.shape, F32)


def _attend_chunk(qt_ref, k_ref, vt_ref, start, c, sel_of, s_ref, p_ref, a_ref, m_ref, l_ref, acc_ref):
    n_kv, tk = s_ref.shape[0], s_ref.shape[1]
    n_g = qt_ref.shape[1] // n_kv
    for kv in range(n_kv):
        q2 = jnp.concatenate([qt_ref[0, kv * n_g + g] for g in range(n_g)], axis=1)
        s_ref[kv] = jnp.dot(k_ref[0, kv, pl.ds(start, tk), :], q2, preferred_element_type=F32)
    for kv in range(n_kv):
        sel = sel_of(kv)
        s = s_ref[kv]
        m_old = m_ref[kv]
        m_new = jnp.maximum(m_old, jnp.max(jnp.where(sel, s, NEG_BIG), axis=0, keepdims=True))
        p = jnp.where(sel, jnp.exp(s - m_new), 0.0)
        alpha = jnp.exp(m_old - m_new)
        l_ref[kv] = alpha * l_ref[kv] + jnp.sum(p, axis=0, keepdims=True)
        p_ref[kv] = p.astype(BF16)
        a_ref[kv] = alpha
        m_ref[kv] = m_new
    for kv in range(n_kv):
        acc_ref[kv] = a_ref[kv] * acc_ref[kv] + jnp.dot(vt_ref[0, kv, c], p_ref[kv],
                                                        preferred_element_type=F32)


def _attend_finish(o_ref, l_ref, acc_ref):
    n_kv, hd, wide = acc_ref.shape
    tq = o_ref.shape[1]
    n_g = wide // tq
    for kv in range(n_kv):
        out = acc_ref[kv] / l_ref[kv]
        rows = jnp.concatenate([out[:, g * tq:(g + 1) * tq] for g in range(n_g)], axis=0)
        o_ref[0, :, kv * n_g * hd:(kv + 1) * n_g * hd] = rows.T


def _attend_scratch(n_kv, n_g, tq, tk):
    wide = n_g * tq
    return [pltpu.VMEM((n_kv, tk, wide), F32),
            pltpu.VMEM((n_kv, tk, wide), BF16),
            pltpu.VMEM((n_kv, 1, wide), F32),
            pltpu.VMEM((n_kv, 1, wide), F32),
            pltpu.VMEM((n_kv, 1, wide), F32),
            pltpu.VMEM((n_kv, HEAD_DIM, wide), F32)]


def _dsa_prompt_kernel(qit_ref, wt_ref, ki_ref, qt_ref, k_ref, vt_ref, o_ref,
                       key_ref, j_ref, s_ref, p_ref, a_ref, m_ref, l_ref, acc_ref, *, top):
    tq, tk = ATT_TQ, ATT_TK
    i = pl.program_id(1)
    n_chunks = ((i + 1) * tq + tk - 1) // tk
    qpos = i * tq + lax.broadcasted_iota(jnp.int32, (1, tq), 1)
    row = lax.broadcasted_iota(jnp.int32, (tk, tq), 0)

    def score_chunk(c, carry):
        start = pl.multiple_of(c * tk, tk)
        kic = ki_ref[0, pl.ds(start, tk), :]
        s = jnp.zeros((tk, tq), F32)
        for h in range(IDX_HEADS):
            d = jnp.dot(kic, qit_ref[0, h], preferred_element_type=F32)
            s = s + wt_ref[0, h:h + 1, :] * jnp.maximum(d, 0.0)
        s = jnp.where(s == 0.0, 0.0, s)
        bits = pltpu.bitcast(s, jnp.int32)
        key = jnp.where(bits < 0, bits ^ INT_MAX, bits)
        key = jnp.where(start + row <= qpos, key, INT_MIN)
        key_ref[pl.ds(start, tk), :] = key
        return carry

    lax.fori_loop(0, n_chunks, score_chunk, 0)

    def count(pred):
        def body(c, acc):
            start = pl.multiple_of(c * tk, tk)
            return acc + _col_sum_i32(pred(key_ref[pl.ds(start, tk), :], start + row))
        acc = lax.fori_loop(0, n_chunks, body, jnp.zeros((SUBLANES, tq), jnp.int32))
        return jnp.sum(acc, axis=0, keepdims=True)

    def bis_cond(st):
        it, lo, hi, _ = st
        return ((it < BISECT_VALUE_STEPS + BISECT_INT_STEPS)
                & (jnp.max(jnp.where(hi != lo + 1, 1, 0)) > 0))

    def bis_body(st):
        it, lo, hi, cnt_lo = st
        mid = _bisect_mid(it, lo, hi)
        c = count(lambda kc, _: kc >= mid)
        ge = c >= top
        lo2 = jnp.where(ge, mid, lo)
        hi2 = jnp.where(c == top, mid + 1, jnp.where(ge, hi, mid))
        return it + 1, lo2, hi2, jnp.where(ge, c, cnt_lo)

    def max_body(c, acc):
        kc = key_ref[pl.ds(pl.multiple_of(c * tk, tk), tk), :]
        return jnp.maximum(acc, jnp.max(kc.reshape(tk // SUBLANES, SUBLANES, tq), axis=0))

    kmax = jnp.max(lax.fori_loop(0, n_chunks, max_body, jnp.full((SUBLANES, tq), INT_MIN, jnp.int32)),
                   axis=0, keepdims=True)
    _, t, _, cnt_t = lax.while_loop(
        bis_cond, lambda st: bis_body(bis_body(st)),
        (jnp.int32(0), jnp.full((1, tq), INT_MIN, jnp.int32), kmax + 1,
         jnp.zeros((1, tq), jnp.int32) + n_chunks * tk))

    need = (cnt_t > top) & (t > INT_MIN)
    j_ref[...] = jnp.full((1, tq), INT_MAX, jnp.int32)

    @pl.when(jnp.max(need.astype(jnp.int32)) > 0)
    def _():
        r = top - count(lambda kc, _: kc > t)

        def jb(_, st):
            lo, hi = st
            mid = (lo + hi) >> 1
            ok = count(lambda kc, kp: (kc == t) & (kp <= mid)) >= r
            return jnp.where(ok, lo, mid), jnp.where(ok, mid, hi)

        n_bits = int(math.ceil(math.log2(key_ref.shape[0]))) + 1
        _, hi = lax.fori_loop(0, n_bits, jb, (jnp.full((1, tq), -1, jnp.int32),
                                              jnp.zeros((1, tq), jnp.int32) + (n_chunks * tk - 1)))
        j_ref[...] = jnp.where(need, hi, INT_MAX)

    _attend_init(m_ref, l_ref, acc_ref)
    jcut = j_ref[...]

    def att_chunk(c, carry):
        start = pl.multiple_of(c * tk, tk)
        kc = key_ref[pl.ds(start, tk), :]
        sel = (kc > INT_MIN) & ((kc > t) | ((kc == t) & (start + row <= jcut)))
        sel2 = jnp.concatenate([sel] * A_GROUP, axis=1)
        _attend_chunk(qt_ref, k_ref, vt_ref, start, c, lambda kv: sel2, s_ref, p_ref, a_ref, m_ref, l_ref, acc_ref)
        return carry

    lax.fori_loop(0, n_chunks, att_chunk, 0)
    _attend_finish(o_ref, l_ref, acc_ref)


def dsa_prompt(qit, wt, ki, qt, k, vt, top):
    n_b, _, _, n_s = qt.shape
    tq, tk = ATT_TQ, ATT_TK
    return pl.pallas_call(
        functools.partial(_dsa_prompt_kernel, top=top),
        grid=(n_b, n_s // tq),
        in_specs=[pl.BlockSpec((1, IDX_HEADS, IDX_DIM, tq), lambda b, i: (b, 0, 0, i)),
                  pl.BlockSpec((1, IDX_HEADS, tq), lambda b, i: (b, 0, i)),
                  pl.BlockSpec((1, n_s, IDX_DIM), lambda b, i: (b, 0, 0)),
                  pl.BlockSpec((1, A_HEADS, HEAD_DIM, tq), lambda b, i: (b, 0, 0, i)),
                  pl.BlockSpec((1, A_KV, n_s, HEAD_DIM), lambda b, i: (b, 0, 0, 0)),
                  pl.BlockSpec((1, A_KV, n_s // tk, HEAD_DIM, tk), lambda b, i: (b, 0, 0, 0, 0))],
        out_specs=pl.BlockSpec((1, tq, A_WIDTH), lambda b, i: (b, i, 0)),
        out_shape=jax.ShapeDtypeStruct((n_b, n_s, A_WIDTH), F32),
        scratch_shapes=[pltpu.VMEM((n_s, tq), jnp.int32),
                        pltpu.VMEM((1, tq), jnp.int32)] + _attend_scratch(A_KV, A_GROUP, tq, tk),
        compiler_params=pltpu.CompilerParams(dimension_semantics=("arbitrary", "arbitrary"),
                                             vmem_limit_bytes=VMEM_LIMIT),
        name="dsa_prompt",
    )(qit, wt, ki, qt, k, vt)


def _moba_prompt_kernel(qt_ref, k_ref, vt_ref, avg_ref, o_ref, kmean_ref, selm_ref,
                        s_ref, p_ref, a_ref, m_ref, l_ref, acc_ref, *, top):
    tq, tk = ATT_TQ, MOBA_BLOCK
    wide = B_GROUP * tq
    n_blk = avg_ref.shape[0]
    i = pl.program_id(1)
    n_own = (i * tq) // tk
    qpos = i * tq + (lax.broadcasted_iota(jnp.int32, (1, wide), 1) & (tq - 1))
    row = lax.broadcasted_iota(jnp.int32, (tk, wide), 0)
    nrow = lax.broadcasted_iota(jnp.int32, (n_blk, wide), 0)

    @pl.when(i == 0)
    def _():
        for kv in range(B_KV):
            kmean_ref[kv] = jnp.dot(avg_ref[...], k_ref[0, kv], preferred_element_type=F32).astype(BF16)

    for kv in range(B_KV):
        q2 = jnp.concatenate([qt_ref[0, kv * B_GROUP + g] for g in range(B_GROUP)], axis=1)
        g = jnp.dot(kmean_ref[kv], q2, preferred_element_type=F32)
        g = jnp.where(nrow < n_own, g, -jnp.inf)
        picked = jnp.zeros((n_blk, wide), jnp.bool_)
        for _ in range(top):
            mx = jnp.max(g, axis=0, keepdims=True)
            first = jnp.min(jnp.where(g == mx, nrow, n_blk), axis=0, keepdims=True)
            hit = nrow == first
            picked = picked | hit
            g = jnp.where(hit, -jnp.inf, g)
        selm_ref[kv] = jnp.where(picked & (nrow < n_own), 1.0, 0.0)

    _attend_init(m_ref, l_ref, acc_ref)

    def att_block(n, carry):
        start = pl.multiple_of(n * tk, tk)
        own_causal = ((jnp.zeros((1, wide), jnp.int32) + n) == n_own) & (start + row <= qpos)

        def sel_of(kv):
            picked_n = jnp.max(jnp.where(nrow == n, selm_ref[kv], 0.0), axis=0, keepdims=True) > 0.0
            return own_causal | picked_n

        _attend_chunk(qt_ref, k_ref, vt_ref, start, n, sel_of, s_ref, p_ref, a_ref, m_ref, l_ref, acc_ref)
        return carry

    lax.fori_loop(0, n_own + 1, att_block, 0)
    _attend_finish(o_ref, l_ref, acc_ref)


def moba_prompt(qt, k, vt):
    n_b, _, _, n_s = qt.shape
    tq, tk = ATT_TQ, MOBA_BLOCK
    n_blk = n_s // tk
    avg = (jnp.repeat(jnp.eye(n_blk, dtype=F32), tk, axis=1) / tk).astype(BF16)
    return pl.pallas_call(
        functools.partial(_moba_prompt_kernel, top=min(MOBA_TOPK, n_blk)),
        grid=(n_b, n_s // tq),
        in_specs=[pl.BlockSpec((1, B_HEADS, HEAD_DIM, tq), lambda b, i: (b, 0, 0, i)),
                  pl.BlockSpec((1, B_KV, n_s, HEAD_DIM), lambda b, i: (b, 0, 0, 0)),
                  pl.BlockSpec((1, B_KV, n_blk, HEAD_DIM, tk), lambda b, i: (b, 0, 0, 0, 0)),
                  pl.BlockSpec((n_blk, n_s), lambda b, i: (0, 0))],
        out_specs=pl.BlockSpec((1, tq, B_WIDTH), lambda b, i: (b, i, 0)),
        out_shape=jax.ShapeDtypeStruct((n_b, n_s, B_WIDTH), F32),
        scratch_shapes=[pltpu.VMEM((B_KV, n_blk, HEAD_DIM), BF16),
                        pltpu.VMEM((B_KV, n_blk, B_GROUP * tq), F32)] + _attend_scratch(B_KV, B_GROUP, tq, tk),
        compiler_params=pltpu.CompilerParams(dimension_semantics=("arbitrary", "arbitrary"),
                                             vmem_limit_bytes=VMEM_LIMIT),
        name="moba_prompt",
    )(qt, k, vt, avg)


PEER_TT = 256
PEER_TE = 1024
PEER_NSORT = PEER_TOPK + 1
PEER_SVROWS = 24


def _extract_sorted(s, n, emit):
    rows, tt = s.shape
    rowi = lax.broadcasted_iota(jnp.int32, (rows, tt), 0)
    for r in range(n):
        mx = jnp.max(s, axis=0, keepdims=True)
        emit(r, mx)
        if r + 1 < n:
            first = jnp.min(jnp.where(s == mx, rowi, rows), axis=0, keepdims=True)
            s = jnp.where(rowi == first, -jnp.inf, s)


def _peer_select_kernel(x_ref, g_ref, wqt_ref, sk_ref, hb_ref, s1_ref, e1_ref, thr_ref, e0_ref,
                        q_ref, s_ref, sv_ref, top_ref):
    half = PEER_QDIM // 2
    x = x_ref[...]
    hb = (x * lax.rsqrt(jnp.mean(x * x, axis=-1, keepdims=True) + EPS) * g_ref[...]).astype(BF16)
    hb_ref[...] = hb
    q_ref[...] = lax.dot_general(wqt_ref[...], hb, (((1,), (1,)), ((), ())),
                                 preferred_element_type=F32).astype(BF16)
    sv_ref[...] = jnp.full(sv_ref.shape, -jnp.inf, F32)

    def head(h, carry):
        for p in range(2):
            hp = h * 2 + p
            s = jnp.dot(sk_ref[hp], q_ref[pl.ds(pl.multiple_of(hp * half, half), half), :],
                        preferred_element_type=F32)
            s_ref[p] = s

            def emit(r, v, p=p):
                sv_ref[p, r:r + 1, :] = v
            _extract_sorted(s, PEER_NSORT, emit)
        sv0, sv1 = sv_ref[0], sv_ref[1]
        cand = jnp.concatenate(
            [sv0[0:1] + sv1] + [sv0[a:a + 1] + sv1[0:8] for a in range(1, 8)] + [sv0[8:24] + sv1[0:1]], axis=0)

        def emit_top(r, v):
            top_ref[r:r + 1, :] = v
        _extract_sorted(cand, PEER_NSORT, emit_top)
        tau = 0.5 * (top_ref[PEER_TOPK - 1:PEER_TOPK, :] + top_ref[PEER_TOPK:PEER_TOPK + 1, :])
        cmax = sv0[0:1] + sv1[0:1]
        z = jnp.sum(jnp.where(cand >= tau, jnp.exp(cand - cmax), 0.0), axis=0, keepdims=True)
        s0, s1 = s_ref[0], s_ref[1]
        s1_ref[h] = s1
        e1_ref[h] = jnp.exp(s1 - sv1[0:1]) / z
        thr_ref[h] = tau - s0
        e0_ref[h] = jnp.exp(s0 - sv0[0:1])
        return carry

    lax.fori_loop(0, PEER_HEADS, head, 0)


def _peer_apply_kernel(x_ref, hb_ref, u_ref, vt_ref, s1_ref, e1_ref, thr_ref, e0_ref, o_ref, g_ref, acc_ref):
    e = pl.program_id(1)
    tt = hb_ref.shape[0]
    hb = hb_ref[...]
    contrib = None
    for il in range(0, PEER_TE // PEER_NKEYS, 2):
        rows = slice(il * PEER_NKEYS, (il + 2) * PEER_NKEYS)
        a = lax.dot_general(u_ref[rows, :], hb, (((1,), (1,)), ((), ())), preferred_element_type=F32)
        for lt in range(tt // LANES):
            ls = slice(lt * LANES, (lt + 1) * LANES)
            acc = [jnp.zeros((PEER_NKEYS, LANES), F32) for _ in range(2)]
            for h in range(PEER_HEADS):
                s1 = s1_ref[h, :, ls]
                e1 = e1_ref[h, :, ls]
                for j in range(2):
                    thr_row = thr_ref[h, 0, il + j:il + j + 1, ls]
                    e0_row = e0_ref[h, 0, il + j:il + j + 1, ls]
                    acc[j] = acc[j] + jnp.where(s1 >= thr_row, e1 * e0_row, 0.0)
            for j in range(2):
                g_ref[j * PEER_NKEYS:(j + 1) * PEER_NKEYS, ls] = acc[j]
        ga = (g_ref[...] * jax.nn.gelu(a)).astype(BF16)
        part = jnp.dot(vt_ref[:, rows], ga, preferred_element_type=F32)
        contrib = part if contrib is None else contrib + part

    @pl.when(e == 0)
    def _():
        acc_ref[...] = contrib

    @pl.when(e != 0)
    def _():
        acc_ref[...] += contrib

    @pl.when(e == pl.num_programs(1) - 1)
    def _():
        o_ref[...] = x_ref[...] + acc_ref[...].T


def _transpose_cast_kernel(x_ref, o_ref):
    o_ref[...] = x_ref[...].T.astype(o_ref.dtype)


def transpose_cast(x, dtype, tile=512):
    n_r, n_c = x.shape
    return pl.pallas_call(
        _transpose_cast_kernel,
        grid=(n_r // tile, n_c // tile),
        in_specs=[pl.BlockSpec((tile, tile), lambda i, j: (i, j))],
        out_specs=pl.BlockSpec((tile, tile), lambda i, j: (j, i)),
        out_shape=jax.ShapeDtypeStruct((n_c, n_r), dtype),
        compiler_params=pltpu.CompilerParams(dimension_semantics=("arbitrary", "arbitrary")),
        name="transpose_cast",
    )(x)


def peer_tables(u_tab, v_tab):
    return u_tab.astype(BF16), transpose_cast(v_tab, BF16)


def peer(x, g_ffn, w_q, subkeys, u_b, vt_b):
    n_tok, d = x.shape
    n_exp = u_b.shape[0]
    tt = _pick_tile(n_tok, (PEER_TT, LANES))
    assert tt % LANES == 0, "PEER kernels keep tokens on whole lane tiles"
    half = PEER_QDIM // 2
    wqt = w_q.T.astype(BF16)
    sk = subkeys.reshape(PEER_HEADS * 2, PEER_NKEYS, half).astype(BF16)
    stat = jax.ShapeDtypeStruct((PEER_HEADS, PEER_NKEYS, n_tok), F32)
    stat_spec = pl.BlockSpec((PEER_HEADS, PEER_NKEYS, tt), lambda t: (0, 0, t))
    hb, s1, e1, thr, e0 = pl.pallas_call(
        _peer_select_kernel,
        grid=(n_tok // tt,),
        in_specs=[pl.BlockSpec((tt, d), lambda t: (t, 0)),
                  pl.BlockSpec((1, d), lambda t: (0, 0)),
                  pl.BlockSpec((PEER_HEADS * PEER_QDIM, d), lambda t: (0, 0)),
                  pl.BlockSpec((PEER_HEADS * 2, PEER_NKEYS, half), lambda t: (0, 0, 0))],
        out_specs=[pl.BlockSpec((tt, d), lambda t: (t, 0)), stat_spec, stat_spec, stat_spec, stat_spec],
        out_shape=[jax.ShapeDtypeStruct((n_tok, d), BF16), stat, stat, stat, stat],
        scratch_shapes=[pltpu.VMEM((PEER_HEADS * PEER_QDIM, tt), BF16),
                        pltpu.VMEM((2, PEER_NKEYS, tt), F32),
                        pltpu.VMEM((2, PEER_SVROWS, tt), F32),
                        pltpu.VMEM((PEER_SVROWS, tt), F32)],
        compiler_params=pltpu.CompilerParams(dimension_semantics=("arbitrary",),
                                             vmem_limit_bytes=VMEM_LIMIT),
        name="peer_select",
    )(x, g_ffn.astype(F32).reshape(1, d), wqt, sk)

    n_il = PEER_TE // PEER_NKEYS
    thr4 = thr.reshape(PEER_HEADS, PEER_NKEYS // n_il, n_il, n_tok)
    e04 = e0.reshape(PEER_HEADS, PEER_NKEYS // n_il, n_il, n_tok)
    stat2 = pl.BlockSpec((PEER_HEADS, PEER_NKEYS, tt), lambda t, e: (0, 0, t))
    stat4 = pl.BlockSpec((PEER_HEADS, 1, n_il, tt), lambda t, e: (0, e, 0, t))
    tok = pl.BlockSpec((tt, d), lambda t, e: (t, 0))
    return pl.pallas_call(
        _peer_apply_kernel,
        grid=(n_tok // tt, n_exp // PEER_TE),
        in_specs=[tok, tok,
                  pl.BlockSpec((PEER_TE, d), lambda t, e: (e, 0)),
                  pl.BlockSpec((d, PEER_TE), lambda t, e: (0, e)),
                  stat2, stat2, stat4, stat4],
        out_specs=tok,
        out_shape=jax.ShapeDtypeStruct((n_tok, d), F32),
        scratch_shapes=[pltpu.VMEM((2 * PEER_NKEYS, tt), F32), pltpu.VMEM((d, tt), F32)],
        compiler_params=pltpu.CompilerParams(dimension_semantics=("arbitrary", "arbitrary"),
                                             vmem_limit_bytes=VMEM_LIMIT),
        name="peer_apply",
    )(x, hb, u_b, vt_b, s1, e1, thr4, e04)


MLSTM_CHUNK = 128
HIGHEST = lax.Precision.HIGHEST


def _mlstm_kernel(q_ref, kt_ref, v_ref, co_ref, if_ref, gc_ref, s0_ref, m0_ref,
                  oc_ref, s_out_ref, m_out_ref, s_ref, m_ref, *, valid):
    L, D = MLSTM_CHUNK, C_HD
    c = pl.program_id(1)

    @pl.when(c == 0)
    def _():
        s_ref[...] = s0_ref[0]
        m_ref[...] = m0_ref[0]

    li = lax.broadcasted_iota(jnp.int32, (L, L), 0)
    si = lax.broadcasted_iota(jnp.int32, (L, L), 1)
    tri = si <= li
    upper = jnp.where(li <= si, 1.0, 0.0)
    ones_ll = jnp.ones((L, L), F32)
    live = c * L + lax.broadcasted_iota(jnp.int32, (1, L), 1) < valid
    ones_col = jnp.where(lax.broadcasted_iota(jnp.int32, (L, D), 1) == 0, 1.0, 0.0).astype(BF16)

    for h in range(C_HEADS):
        lanes = slice(h * D, (h + 1) * D)
        logf = jnp.where(live, jax.nn.log_sigmoid(if_ref[0, C_HEADS + h:C_HEADS + h + 1, :]), 0.0)
        i_row = jnp.where(live, if_ref[0, h:h + 1, :], -jnp.inf)
        f_b = jnp.broadcast_to(logf, (L, L))
        b_col_b = jnp.dot(jnp.where(tri, f_b, 0.0), ones_ll, precision=HIGHEST,
                          preferred_element_type=F32)
        b_row = jnp.dot(f_b[0:SUBLANES], upper, precision=HIGHEST,
                        preferred_element_type=F32)[0:1]
        b_col = b_col_b[:, 0:1]
        dmat = jnp.where(tri, b_col_b - b_row + i_row, -jnp.inf)
        m_prev = m_ref[h][0:1, 0:1]
        m_t = jnp.maximum(m_prev + b_col, jnp.max(dmat, axis=1, keepdims=True))
        inter = jnp.exp(m_prev + b_col - m_t)
        qb = q_ref[0, :, lanes].astype(BF16)
        kt = kt_ref[0, lanes, :]
        vaug = jnp.concatenate([v_ref[0, :, lanes].astype(BF16), ones_col], axis=1)
        s = jnp.dot(qb, kt.astype(BF16), preferred_element_type=F32) * jnp.exp(dmat - m_t)
        state = s_ref[h]
        tot = (jnp.dot(s.astype(BF16), vaug, preferred_element_type=F32)
               + inter * jnp.dot(qb, state.astype(BF16), preferred_element_type=F32))
        hh = tot[:, :D] / jnp.maximum(jnp.abs(tot[:, D:D + 1]), jnp.exp(-m_t))
        y = hh * lax.rsqrt(jnp.mean(hh * hh, axis=1, keepdims=True) + EPS) * gc_ref[0:1, lanes]
        oc_ref[0, :, lanes] = y * jax.nn.sigmoid(co_ref[0, :, lanes])
        m_new = m_t[L - 1:L, :]
        b_last = b_col[L - 1:L, :]
        wk_row = jnp.exp(b_last - b_row + i_row - m_new)
        s_ref[h] = (jnp.exp(m_prev + b_last - m_new) * state
                    + jnp.dot((kt * wk_row).astype(BF16), vaug, preferred_element_type=F32))
        m_ref[h] = jnp.broadcast_to(m_new, (SUBLANES, LANES))

    @pl.when(c == pl.num_programs(1) - 1)
    def _():
        s_out_ref[0] = s_ref[...]
        m_out_ref[0] = m_ref[...]


def mlstm(cq, ck, cv, co, ci, cf, g_c_out, C0, n0, m0):
    n_b, n_s, width = cq.shape
    L, D = MLSTM_CHUNK, C_HD
    n_c = -(-n_s // L)
    pad = n_c * L - n_s

    def padded(a):
        return jnp.pad(a, ((0, 0), (0, pad), (0, 0))) if pad else a

    kt = jnp.swapaxes(padded(ck) * (D ** -0.5), 1, 2)
    gates = jnp.swapaxes(padded(jnp.concatenate([ci, cf], axis=-1)), 1, 2)
    oc, C1, n1, m1 = mlstm_call((padded(cq), 0), kt, (padded(cv), 0), (padded(co), 0), gates,
                                g_c_out, C0, n0, m0, n_s)
    return oc[:, :n_s], C1, n1, m1


def mlstm_call(q_src, kt, v_src, co_src, gates, g_c_out, C0, n0, m0, n_s):
    n_b, width, s_pad = kt.shape
    L, D = MLSTM_CHUNK, C_HD
    n_c = s_pad // L
    s0 = jnp.concatenate([C0, n0[..., None], jnp.zeros((n_b, C_HEADS, D, D - 1), F32)], axis=-1)
    m0b = jnp.broadcast_to(m0[:, :, None, None], (n_b, C_HEADS, SUBLANES, LANES))

    def tok_spec(off):
        return pl.BlockSpec((1, L, width), lambda b, c: (b, c, off))

    tok = tok_spec(0)
    oc, s1, m1 = pl.pallas_call(
        functools.partial(_mlstm_kernel, valid=n_s),
        grid=(n_b, n_c),
        in_specs=[tok_spec(q_src[1]), pl.BlockSpec((1, width, L), lambda b, c: (b, 0, c)),
                  tok_spec(v_src[1]), tok_spec(co_src[1]),
                  pl.BlockSpec((1, 2 * C_HEADS, L), lambda b, c: (b, 0, c)),
                  pl.BlockSpec((1, width), lambda b, c: (0, 0)),
                  pl.BlockSpec((1, C_HEADS, D, 2 * D), lambda b, c: (b, 0, 0, 0)),
                  pl.BlockSpec((1, C_HEADS, SUBLANES, LANES), lambda b, c: (b, 0, 0, 0))],
        out_specs=[tok,
                   pl.BlockSpec((1, C_HEADS, D, 2 * D), lambda b, c: (b, 0, 0, 0)),
                   pl.BlockSpec((1, C_HEADS, SUBLANES, LANES), lambda b, c: (b, 0, 0, 0))],
        out_shape=[jax.ShapeDtypeStruct((n_b, n_c * L, width), F32),
                   jax.ShapeDtypeStruct((n_b, C_HEADS, D, 2 * D), F32),
                   jax.ShapeDtypeStruct((n_b, C_HEADS, SUBLANES, LANES), F32)],
        scratch_shapes=[pltpu.VMEM((C_HEADS, D, 2 * D), F32),
                        pltpu.VMEM((C_HEADS, SUBLANES, LANES), F32)],
        compiler_params=pltpu.CompilerParams(dimension_semantics=("arbitrary", "arbitrary"),
                                             vmem_limit_bytes=VMEM_LIMIT),
        name="mlstm",
    )(q_src[0], kt, v_src[0], co_src[0], gates, g_c_out.astype(F32).reshape(1, width), s0, m0b)
    return oc, s1[..., :D], s1[..., D], m1[:, :, 0, 0]


def _row_sum_i32(mask):
    return jnp.sum(mask.astype(jnp.int32), axis=1, keepdims=True)


def _masked_softmax_pv(logits, sel, v_pages):
    m = jnp.max(jnp.where(sel, logits, NEG_BIG), axis=1, keepdims=True)
    p = jnp.where(sel, jnp.exp(logits - m), 0.0)
    l = jnp.sum(p, axis=1, keepdims=True)
    pb = p.astype(BF16)
    acc = None
    for j, vp in enumerate(v_pages):
        part = jnp.dot(pb[:, j * PAGE_SIZE:(j + 1) * PAGE_SIZE], vp, preferred_element_type=F32)
        acc = part if acc is None else acc + part
    return acc / l


def _dsa_sample_kernel(pt_ref, *refs, n_pages, top, past_len):
    del pt_ref
    n_all = n_pages + 1
    idx_refs = refs[0:n_all]
    k_refs = refs[n_all:2 * n_all]
    v_refs = refs[2 * n_all:3 * n_all]
    qi_ref, w_ref, qbd_ref, o_ref = refs[3 * n_all:3 * n_all + 4]
    n_q = w_ref.shape[2]
    n_keys = n_all * PAGE_SIZE
    rows = IDX_HEADS * n_q

    w_b = jnp.broadcast_to(w_ref[0], (IDX_HEADS, n_q, PAGE_SIZE))
    qi = qi_ref[0]
    pieces = []
    for j in range(n_all):
        kip = idx_refs[j][...].reshape(PAGE_SIZE, IDX_DIM).astype(BF16)
        d = lax.dot_general(qi, kip, (((1,), (1,)), ((), ())), preferred_element_type=F32)
        pieces.append(jnp.sum(w_b * jnp.maximum(d, 0.0).reshape(IDX_HEADS, n_q, PAGE_SIZE), axis=0))
    s = jnp.concatenate(pieces, axis=1)
    s = jnp.where(s == 0.0, 0.0, s)
    bits = pltpu.bitcast(s, jnp.int32)
    kpos = lax.broadcasted_iota(jnp.int32, (n_q, n_keys), 1)
    qpos = past_len + lax.broadcasted_iota(jnp.int32, (n_q, n_keys), 0)
    key = jnp.where(kpos <= qpos, jnp.where(bits < 0, bits ^ INT_MAX, bits), INT_MIN)

    def bis_cond(st):
        it, lo, hi, _ = st
        return ((it < BISECT_VALUE_STEPS + BISECT_INT_STEPS)
                & (jnp.max(jnp.where(hi != lo + 1, 1, 0)) > 0))

    def bis_body(st):
        it, lo, hi, cnt_lo = st
        mid = _bisect_mid(it, lo, hi)
        c = _row_sum_i32(key >= mid)
        ge = c >= top
        return (it + 1, jnp.where(ge, mid, lo), jnp.where(c == top, mid + 1, jnp.where(ge, hi, mid)),
                jnp.where(ge, c, cnt_lo))

    _, t, _, cnt_t = lax.while_loop(
        bis_cond, lambda st: bis_body(bis_body(st)),
        (jnp.int32(0), jnp.full((n_q, 1), INT_MIN, jnp.int32), jnp.max(key, axis=1, keepdims=True) + 1,
         jnp.full((n_q, 1), n_keys, jnp.int32)))

    need = (cnt_t > top) & (t > INT_MIN)
    r = top - _row_sum_i32(key > t)

    def jb(_, st):
        lo, hi = st
        mid = (lo + hi) >> 1
        ok = _row_sum_i32((key == t) & (kpos <= mid)) >= r
        return jnp.where(ok, lo, mid), jnp.where(ok, mid, hi)

    n_bits = int(math.ceil(math.log2(n_keys))) + 1
    _, jhi = lax.fori_loop(0, n_bits, jb, (jnp.full((n_q, 1), -1, jnp.int32),
                                           jnp.full((n_q, 1), n_keys - 1, jnp.int32)))
    jcut = jnp.where(need, jhi, INT_MAX)
    sel = (key > INT_MIN) & ((key > t) | ((key == t) & (kpos <= jcut)))

    qbd = qbd_ref[0]
    logits = jnp.concatenate(
        [lax.dot_general(qbd, k_refs[j][...].reshape(PAGE_SIZE, A_KV * HEAD_DIM).astype(BF16),
                         (((1,), (1,)), ((), ())), preferred_element_type=F32) for j in range(n_all)], axis=1)
    sel_rows = jnp.broadcast_to(sel[None], (A_HEADS, n_q, n_keys)).reshape(rows, n_keys)
    v_pages = [v_refs[j][...].reshape(PAGE_SIZE, A_KV * HEAD_DIM).astype(BF16) for j in range(n_all)]
    o_ref[0] = _masked_softmax_pv(logits, sel_rows, v_pages)


def _moba_sample_kernel(pt_ref, *refs, n_pages):
    del pt_ref
    n_all = n_pages + 1
    k_refs = refs[0:n_all]
    v_refs = refs[n_all:2 * n_all]
    qbd_ref, o_ref = refs[2 * n_all:2 * n_all + 2]
    rows = qbd_ref.shape[1]
    n_q = rows // B_HEADS
    width = B_KV * HEAD_DIM
    pages_per_blk = MOBA_BLOCK // PAGE_SIZE
    n_blk = n_pages // pages_per_blk
    top = min(MOBA_TOPK, n_blk + 1)
    qbd = qbd_ref[0]

    kb = [k_refs[j][...].reshape(PAGE_SIZE, width).astype(BF16) for j in range(n_all)]
    avg = jnp.full((SUBLANES, PAGE_SIZE), 1.0 / MOBA_BLOCK, BF16)
    means = []
    for n in range(n_blk):
        tot = None
        for j in range(pages_per_blk):
            part = jnp.dot(avg, kb[n * pages_per_blk + j], preferred_element_type=F32)
            tot = part if tot is None else tot + part
        means.append(tot[0:1])
    kmean = jnp.concatenate(means + [jnp.zeros((LANES - n_blk, width), F32)], axis=0).astype(BF16)
    gate = lax.dot_general(qbd, kmean, (((1,), (1,)), ((), ())), preferred_element_type=F32)
    lane = lax.broadcasted_iota(jnp.int32, (rows, LANES), 1)
    gate = jnp.where(lane < n_blk, gate, -jnp.inf)
    picked = jnp.zeros((rows, LANES), jnp.bool_)
    for _ in range(top):
        mx = jnp.max(gate, axis=1, keepdims=True)
        first = jnp.min(jnp.where(gate == mx, lane, LANES), axis=1, keepdims=True)
        hit = lane == first
        picked = picked | hit
        gate = jnp.where(hit, -jnp.inf, gate)
    picked = jnp.where(picked & (lane < n_blk), 1.0, 0.0)

    logits = jnp.concatenate(
        [lax.dot_general(qbd, kb[j], (((1,), (1,)), ((), ())), preferred_element_type=F32)
         for j in range(n_all)], axis=1)
    q_of_row = lax.broadcasted_iota(jnp.int32, (rows, PAGE_SIZE), 0) % n_q
    own = lax.broadcasted_iota(jnp.int32, (rows, PAGE_SIZE), 1) <= q_of_row
    sel = jnp.concatenate(
        [jnp.broadcast_to(picked[:, j // pages_per_blk:j // pages_per_blk + 1] > 0.0, (rows, PAGE_SIZE))
         for j in range(n_pages)] + [own], axis=1)
    v_pages = [v_refs[j][...].reshape(PAGE_SIZE, width).astype(BF16) for j in range(n_all)]
    o_ref[0] = _masked_softmax_pv(logits, sel, v_pages)


def _block_diag_queries(q):
    n_b, n_q, n_kv, n_g, hd = q.shape
    qh = jnp.transpose(q, (0, 2, 3, 1, 4))
    eye = jnp.eye(n_kv, dtype=q.dtype)
    bd = qh[:, :, :, :, None, :] * eye[None, :, None, None, :, None]
    return bd.reshape(n_b, n_kv * n_g * n_q, n_kv * hd).astype(BF16)


def _take_diag_heads(o, n_q, n_kv, n_g):
    n_b = o.shape[0]
    o6 = o.reshape(n_b, n_kv, n_g, n_q, n_kv, HEAD_DIM)
    d = jnp.stack([o6[:, kv, :, :, kv, :] for kv in range(n_kv)], axis=1)
    return jnp.transpose(d, (0, 3, 1, 2, 4)).reshape(n_b, n_q, n_kv * n_g * HEAD_DIM)


def _page_specs(layer, n_pages, tail_shape):
    def spec(p):
        return pl.BlockSpec((1, 1, PAGE_SIZE) + tail_shape,
                            lambda b, pt, p=p: (layer, pt[b, p]) + (0,) * (1 + len(tail_shape)))
    return [spec(p) for p in range(n_pages)]


def _pages_2d(cache):
    return cache.astype(BF16).reshape(cache.shape[:3] + (cache.shape[3] * cache.shape[4],))


def _new_page(a):
    return jnp.pad(a, ((0, 0), (0, PAGE_SIZE - a.shape[1]), (0, 0)))


def dsa_sample(layer, page_table, cache_idx, cache_k, cache_v, aq, aiq, aiw, ak, av, aik):
    n_b, n_q = aq.shape[:2]
    n_pages = page_table.shape[1]
    past_len = n_pages * PAGE_SIZE
    width = A_KV * HEAD_DIM
    ck = _pages_2d(cache_k)
    cv = _pages_2d(cache_v)
    qbd = _block_diag_queries(aq * (HEAD_DIM ** -0.5))
    qi = jnp.transpose(aiq, (0, 2, 1, 3)).reshape(n_b, IDX_HEADS * n_q, IDX_DIM).astype(BF16)
    w = jnp.transpose(aiw, (0, 2, 1))[..., None]
    new = lambda tail: pl.BlockSpec((1, PAGE_SIZE) + tail, lambda b, pt: (b, 0) + (0,) * len(tail))
    full = lambda shape: pl.BlockSpec((1,) + shape, lambda b, pt: (b,) + (0,) * len(shape))
    rows = A_HEADS * n_q
    out = pl.pallas_call(
        functools.partial(_dsa_sample_kernel, n_pages=n_pages, top=min(DSA_TOPK, (past_len + n_q) // 4),
                          past_len=past_len),
        grid_spec=pltpu.PrefetchScalarGridSpec(
            num_scalar_prefetch=1, grid=(n_b,),
            in_specs=(_page_specs(layer, n_pages, (IDX_DIM,)) + [new((IDX_DIM,))]
                      + _page_specs(layer, n_pages, (width,)) + [new((width,))]
                      + _page_specs(layer, n_pages, (width,)) + [new((width,))]
                      + [full((IDX_HEADS * n_q, IDX_DIM)), full((IDX_HEADS, n_q, 1)), full((rows, width))]),
            out_specs=pl.BlockSpec((1, rows, width), lambda b, pt: (b, 0, 0))),
        out_shape=jax.ShapeDtypeStruct((n_b, rows, width), F32),
        compiler_params=pltpu.CompilerParams(dimension_semantics=("arbitrary",),
                                             vmem_limit_bytes=VMEM_LIMIT),
        name="dsa_sample",
    )(page_table, *([cache_idx] * n_pages), _new_page(aik),
      *([ck] * n_pages), _new_page(ak.reshape(n_b, n_q, width)),
      *([cv] * n_pages), _new_page(av.reshape(n_b, n_q, width)), qi, w, qbd)
    return _take_diag_heads(out, n_q, A_KV, A_GROUP)


def moba_sample(layer, page_table, cache_k, cache_v, bq, bk, bv):
    n_b, n_q = bq.shape[:2]
    n_pages = page_table.shape[1]
    width = B_KV * HEAD_DIM
    ck = _pages_2d(cache_k)
    cv = _pages_2d(cache_v)
    qbd = _block_diag_queries(bq * (HEAD_DIM ** -0.5))
    new = pl.BlockSpec((1, PAGE_SIZE, width), lambda b, pt: (b, 0, 0))
    rows = B_HEADS * n_q
    out = pl.pallas_call(
        functools.partial(_moba_sample_kernel, n_pages=n_pages),
        grid_spec=pltpu.PrefetchScalarGridSpec(
            num_scalar_prefetch=1, grid=(n_b,),
            in_specs=(_page_specs(layer, n_pages, (width,)) + [new]
                      + _page_specs(layer, n_pages, (width,)) + [new]
                      + [pl.BlockSpec((1, rows, width), lambda b, pt: (b, 0, 0))]),
            out_specs=pl.BlockSpec((1, rows, width), lambda b, pt: (b, 0, 0))),
        out_shape=jax.ShapeDtypeStruct((n_b, rows, width), F32),
        compiler_params=pltpu.CompilerParams(dimension_semantics=("arbitrary",),
                                             vmem_limit_bytes=VMEM_LIMIT),
        name="moba_sample",
    )(page_table, *([ck] * n_pages), _new_page(bk.reshape(n_b, n_q, width)),
      *([cv] * n_pages), _new_page(bv.reshape(n_b, n_q, width)), qbd)
    return _take_diag_heads(out, n_q, B_KV, B_GROUP)


def _rms(x, g):
    xf = x.astype(F32)
    y = xf * lax.rsqrt(jnp.mean(xf * xf, axis=-1, keepdims=True) + EPS)
    return (y * g.astype(F32)).astype(x.dtype)


def _rope(x, pos):
    half = x.shape[-1] // 2
    freqs = jnp.power(jnp.float32(ROPE_THETA), -jnp.arange(half, dtype=F32) / half)
    ang = pos.astype(F32)[:, None] * freqs[None, :]
    cos = jnp.cos(ang)[None, :, None, :]
    sin = jnp.sin(ang)[None, :, None, :]
    xf = x.astype(F32)
    x1, x2 = xf[..., :half], xf[..., half:]
    return jnp.concatenate([x1 * cos - x2 * sin, x1 * sin + x2 * cos], axis=-1).astype(x.dtype)


def _mix_ffn(x, p_l, lw, oa, ob, oc, gate, gate_block):
    n_b, n_s, d = x.shape
    flat = lambda a: a.reshape(n_b * n_s, a.shape[-1])
    x2 = mix_out(flat(x), flat(oa), flat(ob), flat(oc), flat(gate), gate_block, lw)
    x2 = peer(x2, lw['g_ffn'], lw['w_peer_q'], lw['peer_subkeys'], lw['peer_u_b'], lw['peer_vt_b'])
    return ple(x2, flat(p_l), lw).reshape(n_b, n_s, d)


def _layer_prompt(x, p_l, lw):
    n_b, n_s, _ = x.shape
    pr = prompt_projection(x, lw)
    oa = dsa_prompt(pr['qi'], pr['wi'], pr['ki_h'], pr['qa'], pr['ka_h'], pr['va_t'],
                    top=min(DSA_TOPK, n_s // 4))
    ob = moba_prompt(pr['qb'], pr['kb_h'], pr['vb_t'])
    zc = pr['zc']
    c0 = N_BRANCH * D_MODEL // C_WIDTH
    oc, C1, n1, m1 = mlstm_call((zc, c0), pr['kc_t'], (zc, c0 + 1), (zc, c0 + 2), pr['gates'], lw['g_c_out'],
                                jnp.zeros((n_b, C_HEADS, C_HD, C_HD), F32),
                                jnp.zeros((n_b, C_HEADS, C_HD), F32), jnp.zeros((n_b, C_HEADS), F32), n_s)
    x = _mix_ffn(x, p_l, lw, oa, ob, oc, zc, 0)
    kv4 = lambda a: a.reshape(n_b, n_s, A_KV, HEAD_DIM)
    return x, (kv4(pr['ka']), kv4(pr['va']), pr['ki'], kv4(pr['kb']), kv4(pr['vb']), C1, n1, m1)


def _layer_sample(x, p_l, pos, lw, past):
    n_b, n_s, _ = x.shape
    z = _mm3(x, lw['w_in'], lw['g_mix'])
    (aq, ak, av, aiq, aiw, aik, bq, bk, bv, cq, ck, cv, co, ci, cf, gate) = jnp.split(z, IN_SPLITS, axis=-1)
    aq = _rope(_rms(aq.reshape(n_b, n_s, A_HEADS, HEAD_DIM), lw['g_qa']), pos).reshape(n_b, n_s, A_KV, A_GROUP, HEAD_DIM)
    ak = _rope(_rms(ak.reshape(n_b, n_s, A_KV, HEAD_DIM), lw['g_ka']), pos)
    av = av.reshape(n_b, n_s, A_KV, HEAD_DIM)
    aiq = _rope(aiq.reshape(n_b, n_s, IDX_HEADS, IDX_DIM), pos)
    aik = _rope(aik.reshape(n_b, n_s, 1, IDX_DIM), pos)[:, :, 0]
    aiw = aiw * IDX_W_SCALE
    bq = _rope(_rms(bq.reshape(n_b, n_s, B_HEADS, HEAD_DIM), lw['g_qb']), pos).reshape(n_b, n_s, B_KV, B_GROUP, HEAD_DIM)
    bk = _rope(_rms(bk.reshape(n_b, n_s, B_KV, HEAD_DIM), lw['g_kb']), pos)
    bv = bv.reshape(n_b, n_s, B_KV, HEAD_DIM)
    cq = cq.reshape(n_b, n_s, C_HEADS, C_HD)
    ck = ck.reshape(n_b, n_s, C_HEADS, C_HD)
    cv = cv.reshape(n_b, n_s, C_HEADS, C_HD)
    ci = ci + lw['b_if'][:C_HEADS]
    cf = cf + lw['b_if'][C_HEADS:]
    oa = dsa_sample(past['layer'], past['page_table'], past['a_idx'], past['a_k'], past['a_v'],
                    aq, aiq, aiw, ak, av, aik)
    ob = moba_sample(past['layer'], past['page_table'], past['b_k'], past['b_v'], bq, bk, bv)
    oc, C1, n1, m1 = mlstm(cq.reshape(n_b, n_s, C_WIDTH), ck.reshape(n_b, n_s, C_WIDTH),
                           cv.reshape(n_b, n_s, C_WIDTH), co, ci, cf, lw['g_c_out'],
                           past['c_C'], past['c_n'], past['c_m'])
    return _mix_ffn(x, p_l, lw, oa, ob, oc, gate, 0), (ak, av, aik, bk, bv, C1, n1, m1)


def kernel(x_prompt, x_sample, p_prompt, p_sample, cache_a_k, cache_a_v, cache_a_idx, cache_b_k, cache_b_v,
           state_c_C, state_c_n, state_c_m, page_table, g_mix, w_in, b_if, g_qa, g_ka, g_qb, g_kb, g_c_out,
           w_br_a, w_br_b, w_br_c, w_out, g_ffn, w_peer_q, peer_subkeys, peer_u, peer_v, w_ple, w_ple_gate):
    past_len = page_table.shape[1] * PAGE_SIZE
    pos_s = past_len + jnp.arange(x_sample.shape[1], dtype=jnp.int32)
    yp, ys = x_prompt, x_sample
    states_p, states_s = [], []
    for l in range(DEPTH):
        lw = {'g_mix': g_mix[l], 'w_in': w_in[l], 'b_if': b_if[l], 'g_qa': g_qa[l], 'g_ka': g_ka[l],
              'g_qb': g_qb[l], 'g_kb': g_kb[l], 'g_c_out': g_c_out[l], 'w_br_a': w_br_a[l], 'w_br_b': w_br_b[l],
              'w_br_c': w_br_c[l], 'w_out': w_out[l], 'g_ffn': g_ffn[l], 'w_peer_q': w_peer_q[l],
              'peer_subkeys': peer_subkeys[l], 'peer_u': peer_u[l], 'peer_v': peer_v[l], 'w_ple': w_ple[l],
              'w_ple_gate': w_ple_gate[l]}
        lw['peer_u_b'], lw['peer_vt_b'] = peer_tables(peer_u[l], peer_v[l])
        yp, st_p = _layer_prompt(yp, p_prompt[l], lw)
        past = {'layer': l, 'page_table': page_table, 'a_k': cache_a_k, 'a_v': cache_a_v, 'a_idx': cache_a_idx,
                'b_k': cache_b_k, 'b_v': cache_b_v,
                'c_C': state_c_C[l], 'c_n': state_c_n[l], 'c_m': state_c_m[l]}
        ys, st_s = _layer_sample(ys, p_sample[l], pos_s, lw, past)
        states_p.append(st_p)
        states_s.append(st_s)
    (pa_k, pa_v, pa_i, pb_k, pb_v, pc_C, pc_n, pc_m) = [jnp.stack(a) for a in zip(*states_p)]
    (sa_k, sa_v, sa_i, sb_k, sb_v, sc_C, sc_n, sc_m) = [jnp.stack(a) for a in zip(*states_s)]
    return (yp, ys, pa_k, pa_v, pa_i, pb_k, pb_v, pc_C, pc_n, pc_m,
            sa_k, sa_v, sa_i, sb_k, sb_v, sc_C, sc_n, sc_m)
```
